```python
import math
import jax, jax.numpy as jnp
from jax import lax
import numpy as np

D_MODEL = 1024
BATCH = 8
SEQ = 2048
DEPTH = 2

N_MIXERS = 2
N_A = (DEPTH + 1) // 2
N_B = DEPTH // 2
N_SUB = 3
D_FF = 2816
FFN_RES = 0.5
EPS = 1e-6

MLA_HEADS = 16
Q_LORA = 384
KV_LORA = 256
QK_NOPE = 64
QK_ROPE = 32
V_HEAD = 64
ROPE_THETA = 10000.0
Q_BLOCK = 128

DIL_GROUPS = ((128, 1), (512, 4), (2048, 16))
N_GROUPS = 3
DIL_HEADS = 16
DIL_HEAD_DIM = 64
DIL_BLOCK = 128
DIL_WIDTH = DIL_HEADS * DIL_HEAD_DIM

N_BUCKETS = 32
MAX_DISTANCE = 2048

kernel_name = "hybrid_mla_dilated_macaron"


def rmsnorm(x, g):
    xf = x.astype(jnp.float32)
    y = xf * lax.rsqrt(jnp.mean(xf * xf, axis=-1, keepdims=True) + EPS)
    return (y * g.astype(jnp.float32)).astype(x.dtype)


def swiglu(h, w_gate, w_up, w_down):
    return (jax.nn.silu(h @ w_gate) * (h @ w_up)) @ w_down


def rope(x, pos):
    half = x.shape[-1] // 2
    freqs = ROPE_THETA ** (-jnp.arange(half, dtype=jnp.float32) / half)
    ang = pos[:, None] * freqs[None, :]
    cos = jnp.cos(ang)[None, :, None, :]
    sin = jnp.sin(ang)[None, :, None, :]
    x1 = x[..., :half].astype(jnp.float32)
    x2 = x[..., half:].astype(jnp.float32)
    return jnp.concatenate([x1 * cos - x2 * sin, x1 * sin + x2 * cos], axis=-1).astype(x.dtype)


def causal_block_attention(q, k, v, scale):
    B, S, H, dq = q.shape
    nb = S // Q_BLOCK
    qb = q.reshape(B, nb, Q_BLOCK, H, dq).transpose(1, 0, 3, 2, 4)
    kt = k.transpose(0, 2, 1, 3)
    vt = v.transpose(0, 2, 1, 3)
    kpos = jnp.arange(S)

    def one_block(args):
        qi, n = args
        s = jnp.einsum('bhqd,bhkd->bhqk', qi, kt).astype(jnp.float32) * scale
        qpos = n * Q_BLOCK + jnp.arange(Q_BLOCK)
        s = jnp.where(kpos[None, :] <= qpos[:, None], s, -jnp.inf)
        p = jax.nn.softmax(s, axis=-1).astype(vt.dtype)
        return jnp.einsum('bhqk,bhkd->bhqd', p, vt)

    out = lax.map(one_block, (qb, jnp.arange(nb)))
    return out.transpose(1, 0, 3, 2, 4).reshape(B, S, H, v.shape[-1])


def mla(h, w_in, q_norm, w_q_up, kv_norm, w_kv_up, w_o):
    B, S, _ = h.shape
    lat = h @ w_in
    cq = lat[..., :Q_LORA]
    ckv = lat[..., Q_LORA:Q_LORA + KV_LORA]
    k_rope = lat[..., Q_LORA + KV_LORA:][:, :, None, :]
    q = (rmsnorm(cq, q_norm) @ w_q_up).reshape(B, S, MLA_HEADS, QK_NOPE + QK_ROPE)
    kv = (rmsnorm(ckv, kv_norm) @ w_kv_up).reshape(B, S, MLA_HEADS, QK_NOPE + V_HEAD)
    pos = jnp.arange(S, dtype=jnp.float32)
    q = jnp.concatenate([q[..., :QK_NOPE], rope(q[..., QK_NOPE:], pos)], axis=-1)
    k_rope = jnp.broadcast_to(rope(k_rope, pos), (B, S, MLA_HEADS, QK_ROPE))
    k = jnp.concatenate([kv[..., :QK_NOPE], k_rope.astype(kv.dtype)], axis=-1)
    v = kv[..., QK_NOPE:]
    o = causal_block_attention(q, k, v, (QK_NOPE + QK_ROPE) ** -0.5)
    return o.reshape(B, S, MLA_HEADS * V_HEAD) @ w_o


def t5_bucket(dist):
    max_exact = N_BUCKETS // 2
    d = jnp.maximum(dist, 1).astype(jnp.float32)
    large = max_exact + (jnp.log(d / max_exact) / math.log(MAX_DISTANCE / max_exact)
                         * (N_BUCKETS - max_exact)).astype(jnp.int32)
    large = jnp.minimum(large, N_BUCKETS - 1)
    return jnp.where(dist < max_exact, dist, large)


def strided_window_attention(q, k, v, dilation, span, bias_table):
    B, S, H, E = q.shape
    L = S // dilation
    nb = -(-L // DIL_BLOCK)
    Lp = nb * DIL_BLOCK
    qs = q.reshape(B, L, dilation, H, E)
    qb = jnp.pad(qs, ((0, 0), (0, Lp - L), (0, 0), (0, 0), (0, 0))).reshape(B, nb, DIL_BLOCK, dilation, H, E)

    def windows(t):
        tp = jnp.pad(t.reshape(B, L, dilation, H, E),
                     ((0, 0), (DIL_BLOCK, Lp - L), (0, 0), (0, 0), (0, 0)))
        tp = tp.reshape(B, nb + 1, DIL_BLOCK, dilation, H, E)
        return jnp.concatenate([tp[:, :-1], tp[:, 1:]], axis=2)

    kw, vw = windows(k), windows(v)
    s = jnp.einsum('bnqrhe,bnkrhe->bnrhqk', qb, kw).astype(jnp.float32) * (E ** -0.5)
    iq = jnp.arange(DIL_BLOCK)[:, None]
    ik = jnp.arange(2 * DIL_BLOCK)[None, :]
    rel = DIL_BLOCK + iq - ik
    in_window = (rel >= 0) & (rel <= span)
    bucket = t5_bucket(jnp.maximum(rel, 0) * dilation)
    bias = jnp.transpose(bias_table[bucket], (2, 0, 1)).astype(jnp.float32)
    key_m = (jnp.arange(nb)[:, None] - 1) * DIL_BLOCK + jnp.arange(2 * DIL_BLOCK)[None, :]
    valid = in_window[None] & (key_m >= 0)[:, None, :]
    logits = jnp.where(valid[None, :, None, None], s + bias, -jnp.inf)
    lse = jax.nn.logsumexp(logits, axis=-1)
    p = jnp.exp(logits - lse[..., None]).astype(v.dtype)
    o = jnp.einsum('bnrhqk,bnkrhe->bnqrhe', p, vw)
    o = o.reshape(B, Lp, dilation, H, E)[:, :L].reshape(B, S, H, E)
    lse = jnp.transpose(lse, (0, 1, 4, 2, 3)).reshape(B, Lp, dilation, H)[:, :L].reshape(B, S, H)
    return o, lse


def dilated_attention(h, w_in, w_o, rel_bias):
    B, S, _ = h.shape
    proj = (h @ w_in).reshape(B, S, N_GROUPS, 3, DIL_HEADS, DIL_HEAD_DIM)
    outs, lses = [], []
    for g, (window, dilation) in enumerate(DIL_GROUPS):
        o, lse = strided_window_attention(
            proj[:, :, g, 0], proj[:, :, g, 1], proj[:, :, g, 2],
            dilation, window // dilation, rel_bias[:, g * DIL_HEADS:(g + 1) * DIL_HEADS])
        outs.append(o)
        lses.append(lse)
    alpha = jax.nn.softmax(jnp.stack(lses, axis=0), axis=0)
    o = jnp.sum(alpha[..., None] * jnp.stack(outs, axis=0).astype(jnp.float32), axis=0)
    return o.astype(h.dtype).reshape(B, S, DIL_WIDTH) @ w_o


def sandwich(x, fn, pre_g, post_g, shift, scale, gate, res_w):
    hn = rmsnorm(x, pre_g) * (1 + scale[:, None, :]) + shift[:, None, :]
    y = rmsnorm(fn(hn), post_g)
    return x + res_w * gate[:, None, :] * y


def setup_inputs(seed: int = 0) -> dict:
    key = jax.random.key(seed)
    ks = jax.random.split(key, 20)
    D = D_MODEL
    nrm = lambda k, shape, fan: jax.random.normal(k, shape, jnp.float32) * fan ** -0.5
    return {
        "x": jax.random.normal(ks[0], (BATCH, SEQ, D), jnp.float32),
        "c": jax.random.normal(ks[1], (BATCH, D), jnp.float32),
        "norm_pre": 1.0 + 0.05 * jax.random.normal(ks[2], (DEPTH, N_SUB, D), jnp.float32),
        "norm_post": 1.0 + 0.05 * jax.random.normal(ks[3], (DEPTH, N_SUB, D), jnp.float32),
        "w_mod": nrm(ks[4], (DEPTH, D, N_SUB * 3 * D), D) * 0.5,
        "b_mod": 0.02 * jax.random.normal(ks[5], (DEPTH, N_SUB * 3 * D), jnp.float32),
        "ffn_w_gate": nrm(ks[6], (DEPTH, 2, D, D_FF), D),
        "ffn_w_up": nrm(ks[7], (DEPTH, 2, D, D_FF), D),
        "ffn_w_down": nrm(ks[8], (DEPTH, 2, D_FF, D), D_FF),
        "mla_w_in": nrm(ks[9], (N_A, D, Q_LORA + KV_LORA + QK_ROPE), D),
        "mla_q_norm": 1.0 + 0.05 * jax.random.normal(ks[10], (N_A, Q_LORA), jnp.float32),
        "mla_w_q_up": nrm(ks[11], (N_A, Q_LORA, MLA_HEADS * (QK_NOPE + QK_ROPE)), Q_LORA),
        "mla_kv_norm": 1.0 + 0.05 * jax.random.normal(ks[12], (N_A, KV_LORA), jnp.float32),
        "mla_w_kv_up": nrm(ks[13], (N_A, KV_LORA, MLA_HEADS * (QK_NOPE + V_HEAD)), KV_LORA),
        "mla_w_o": nrm(ks[14], (N_A, MLA_HEADS * V_HEAD, D), MLA_HEADS * V_HEAD),
        "dil_w_in": nrm(ks[15], (N_B, D, N_GROUPS * 3 * DIL_WIDTH), D),
        "dil_w_o": nrm(ks[16], (N_B, DIL_WIDTH, D), DIL_WIDTH),
        "rel_bias": 0.5 * jax.random.normal(ks[17], (N_BUCKETS, N_GROUPS * DIL_HEADS), jnp.float32),
    }


def reference(x, c, norm_pre, norm_post, w_mod, b_mod, ffn_w_gate, ffn_w_up, ffn_w_down,
              mla_w_in, mla_q_norm, mla_w_q_up, mla_kv_norm, mla_w_kv_up, mla_w_o,
              dil_w_in, dil_w_o, rel_bias):
    B = x.shape[0]
    for i in range(DEPTH):
        mod = (jax.nn.silu(c) @ w_mod[i] + b_mod[i]).reshape(B, N_SUB, 3, D_MODEL)

        def ffn_first(h, i=i):
            return swiglu(h, ffn_w_gate[i, 0], ffn_w_up[i, 0], ffn_w_down[i, 0])

        def ffn_second(h, i=i):
            return swiglu(h, ffn_w_gate[i, 1], ffn_w_up[i, 1], ffn_w_down[i, 1])

        if i % N_MIXERS == 0:
            a = i // N_MIXERS
            def mixer(h, a=a):
                return mla(h, mla_w_in[a], mla_q_norm[a], mla_w_q_up[a],
                           mla_kv_norm[a], mla_w_kv_up[a], mla_w_o[a])
        else:
            b = i // N_MIXERS
            def mixer(h, b=b):
                return dilated_attention(h, dil_w_in[b], dil_w_o[b], rel_bias)

        x = sandwich(x, ffn_first, norm_pre[i, 0], norm_post[i, 0],
                     mod[:, 0, 0], mod[:, 0, 1], mod[:, 0, 2], FFN_RES)
        x = sandwich(x, mixer, norm_pre[i, 1], norm_post[i, 1],
                     mod[:, 1, 0], mod[:, 1, 1], mod[:, 1, 2], 1.0)
        x = sandwich(x, ffn_second, norm_pre[i, 2], norm_post[i, 2],
                     mod[:, 2, 0], mod[:, 2, 1], mod[:, 2, 2], FFN_RES)
    return x
```

```python
import functools
import math

import jax
import jax.numpy as jnp
from jax import lax
from jax.experimental import pallas as pl
from jax.experimental.pallas import tpu as pltpu

F32 = jnp.float32
BF16 = jnp.bfloat16

D_MODEL = 1024
BATCH = 8
SEQ = 2048
DEPTH = 2
N_SUB = 3
D_FF = 2816
FFN_RES = 0.5
EPS = 1e-6

MLA_HEADS = 16
Q_LORA = 384
KV_LORA = 256
QK_NOPE = 64
QK_ROPE = 32
V_HEAD = 64
ROPE_THETA = 10000.0

DIL_GROUPS = ((128, 1), (512, 4), (2048, 16))
N_GROUPS = 3
DIL_HEADS = 16
DIL_HEAD_DIM = 64
DIL_BLOCK = 128
DIL_WIDTH = DIL_HEADS * DIL_HEAD_DIM
N_BUCKETS = 32
MAX_DISTANCE = 2048

TOKENS = BATCH * SEQ
LANES = 128
HEAD_SLOT = 128
ROPE_HALF = QK_ROPE // 2
MASK_VALUE = -1e30
VMEM_LIMIT = 56 * 1024 * 1024

TM_FFN = 512
TM_PROJ = 512
TM_DIL_PROJ = 1024
TN_DIL_PROJ = 1024
TM_OUT = 512
TQ_MLA = 256
TN_MOD = 1536
FF_CHUNK = 512


def _params(*sem):
    return pltpu.CompilerParams(dimension_semantics=sem, vmem_limit_bytes=VMEM_LIMIT)


def _rms(x, g):
    ms = jnp.mean(x * x, axis=-1, keepdims=True)
    return x * lax.rsqrt(ms + EPS) * g


def _dot(a, b):
    return jnp.dot(a, b, preferred_element_type=F32)


def _dot_nt(a, b):
    return lax.dot_general(a, b, (((1,), (1,)), ((), ())), preferred_element_type=F32)


def _silu(x):
    return x * jax.nn.sigmoid(x)


def _mod_spec(layer, sub, kind, tm):
    tiles_per_batch = SEQ // tm
    return pl.BlockSpec(
        (None, 1, D_MODEL),
        lambda t, *_: ((layer * BATCH + t // tiles_per_batch) * (N_SUB * 3) + sub * 3 + kind, 0, 0))


def _row_spec(index):
    return pl.BlockSpec((None, 1, D_MODEL), lambda *_: (index, 0, 0))


def _mod_kernel(c_ref, w_ref, b_ref, o_ref):
    h = _silu(c_ref[...])
    o_ref[...] = jnp.dot(h, w_ref[...], preferred_element_type=F32,
                         precision=lax.Precision.HIGHEST) + b_ref[...]


def _modulation(c, w_mod, b_mod):
    n = N_SUB * 3 * D_MODEL
    out = pl.pallas_call(
        _mod_kernel,
        grid=(DEPTH, n // TN_MOD),
        in_specs=[
            pl.BlockSpec((BATCH, D_MODEL), lambda i, j: (0, 0)),
            pl.BlockSpec((None, D_MODEL, TN_MOD), lambda i, j: (i, 0, j)),
            pl.BlockSpec((None, 1, TN_MOD), lambda i, j: (i, 0, j)),
        ],
        out_specs=pl.BlockSpec((None, BATCH, TN_MOD), lambda i, j: (i, 0, j)),
        out_shape=jax.ShapeDtypeStruct((DEPTH, BATCH, n), F32),
        compiler_params=_params("arbitrary", "arbitrary"),
        name="modulation",
    )(c, w_mod, b_mod.reshape(DEPTH, 1, n))
    return out.reshape(DEPTH * BATCH * N_SUB * 3, 1, D_MODEL)


def _ffn_kernel(x_ref, shift_ref, scale_ref, gate_ref, pre_ref, post_ref,
                wg_ref, wu_ref, wd_ref, o_ref, a_ref):
    x = x_ref[...]
    hn = (_rms(x, pre_ref[...]) * (1.0 + scale_ref[...]) + shift_ref[...]).astype(BF16)
    for c0 in range(0, D_FF, FF_CHUNK):
        c1 = min(c0 + FF_CHUNK, D_FF)
        g = _dot(hn, wg_ref[:, c0:c1])
        u = _dot(hn, wu_ref[:, c0:c1])
        a_ref[:, c0:c1] = (_silu(g) * u).astype(BF16)
    y = _dot(a_ref[...], wd_ref[...])
    o_ref[...] = x + FFN_RES * gate_ref[...] * _rms(y, post_ref[...])


def _ffn(x, mod, pre, post, wg, wu, wd, layer, which):
    sub = 0 if which == 0 else 2
    tm = TM_FFN
    w_in_spec = pl.BlockSpec((None, None, D_MODEL, D_FF), lambda t: (layer, which, 0, 0),
                             pipeline_mode=pl.Buffered(1))
    w_out_spec = pl.BlockSpec((None, None, D_FF, D_MODEL), lambda t: (layer, which, 0, 0),
                              pipeline_mode=pl.Buffered(1))
    return pl.pallas_call(
        _ffn_kernel,
        grid=(TOKENS // tm,),
        in_specs=[
            pl.BlockSpec((tm, D_MODEL), lambda t: (t, 0)),
            _mod_spec(layer, sub, 0, tm), _mod_spec(layer, sub, 1, tm), _mod_spec(layer, sub, 2, tm),
            _row_spec(layer * N_SUB + sub), _row_spec(layer * N_SUB + sub),
            w_in_spec, w_in_spec, w_out_spec,
        ],
        out_specs=pl.BlockSpec((tm, D_MODEL), lambda t: (t, 0)),
        out_shape=jax.ShapeDtypeStruct((TOKENS, D_MODEL), F32),
        scratch_shapes=[pltpu.VMEM((tm, D_FF), BF16)],
        compiler_params=_params("arbitrary"),
        name="ffn",
    )(x, mod, mod, mod, pre, post, wg, wu, wd)


def _rope(z, ta, tb, tc):
    return z * ta + pltpu.roll(z, ROPE_HALF, 1) * tb + pltpu.roll(z, HEAD_SLOT - ROPE_HALF, 1) * tc


def _mla_proj_kernel(x_ref, shift_ref, scale_ref, pre_ref, win_ref, qn_ref, kvn_ref,
                     wq_ref, wk_ref, wv_ref, ta_ref, tb_ref, tc_ref, q_ref, k_ref, v_ref):
    x = x_ref[...]
    hn = (_rms(x, pre_ref[...]) * (1.0 + scale_ref[...]) + shift_ref[...]).astype(BF16)
    lat = _dot(hn, win_ref[...])
    cq = _rms(lat[:, :Q_LORA], qn_ref[...]).astype(BF16)
    ckv = _rms(lat[:, Q_LORA:Q_LORA + KV_LORA], kvn_ref[...]).astype(BF16)
    ta, tb, tc = ta_ref[...], tb_ref[...], tc_ref[...]
    kr = _rope(lat[:, Q_LORA + KV_LORA:], ta, tb, tc)
    q = _dot(cq, wq_ref[...]) * ((QK_NOPE + QK_ROPE) ** -0.5)
    kn = _dot(ckv, wk_ref[...])
    v_ref[...] = _dot(ckv, wv_ref[...]).astype(BF16)
    for h in range(MLA_HEADS):
        sl = slice(h * HEAD_SLOT, (h + 1) * HEAD_SLOT)
        q_ref[:, sl] = _rope(q[:, sl], ta, tb, tc).astype(BF16)
        k_ref[:, sl] = (kn[:, sl] + kr).astype(BF16)


def _rope_tables():
    pos = jnp.arange(SEQ, dtype=F32)
    freqs = ROPE_THETA ** (-jnp.arange(ROPE_HALF, dtype=F32) / ROPE_HALF)
    ang = pos[:, None] * freqs[None, :]
    cos, sin = jnp.cos(ang), jnp.sin(ang)
    one = jnp.ones((SEQ, QK_NOPE), F32)
    zero = jnp.zeros((SEQ, QK_NOPE), F32)
    z16 = jnp.zeros((SEQ, ROPE_HALF), F32)
    pad = HEAD_SLOT - QK_NOPE - QK_ROPE
    ta = jnp.concatenate([one, cos, cos, one[:, :pad]], axis=1)
    tb = jnp.concatenate([zero, z16, sin, zero[:, :pad]], axis=1)
    tc = jnp.concatenate([zero, -sin, z16, zero[:, :pad]], axis=1)
    return ta, tb, tc


def _mla_weights(w_in, w_q_up, w_kv_up):
    pad = HEAD_SLOT - QK_NOPE - QK_ROPE
    kr_cols = jnp.pad(w_in[:, Q_LORA + KV_LORA:], ((0, 0), (QK_NOPE, pad)))
    win = jnp.concatenate([w_in[:, :Q_LORA + KV_LORA], kr_cols], axis=1).astype(BF16)
    wq = jnp.pad(w_q_up.reshape(Q_LORA, MLA_HEADS, QK_NOPE + QK_ROPE), ((0, 0), (0, 0), (0, pad)))
    wq = wq.reshape(Q_LORA, MLA_HEADS * HEAD_SLOT).astype(BF16)
    wkv = w_kv_up.reshape(KV_LORA, MLA_HEADS, QK_NOPE + V_HEAD)
    wk = jnp.pad(wkv[:, :, :QK_NOPE], ((0, 0), (0, 0), (0, HEAD_SLOT - QK_NOPE)))
    wk = wk.reshape(KV_LORA, MLA_HEADS * HEAD_SLOT).astype(BF16)
    wv = wkv[:, :, QK_NOPE:].reshape(KV_LORA, MLA_HEADS * V_HEAD).astype(BF16)
    return win, wq, wk, wv


def _mla_proj(x, mod, pre, win, qn, kvn, wq, wk, wv, tables, layer):
    tm = TM_PROJ
    tiles_per_batch = SEQ // tm
    full = lambda a: pl.BlockSpec(a.shape, lambda t: (0,) * a.ndim)
    tab_spec = pl.BlockSpec((tm, HEAD_SLOT), lambda t: (t % tiles_per_batch, 0))
    wide = MLA_HEADS * HEAD_SLOT
    return pl.pallas_call(
        _mla_proj_kernel,
        grid=(TOKENS // tm,),
        in_specs=[
            pl.BlockSpec((tm, D_MODEL), lambda t: (t, 0)),
            _mod_spec(layer, 1, 0, tm), _mod_spec(layer, 1, 1, tm),
            _row_spec(layer * N_SUB + 1),
            full(win), full(qn), full(kvn), full(wq), full(wk), full(wv),
            tab_spec, tab_spec, tab_spec,
        ],
        out_specs=[
            pl.BlockSpec((tm, wide), lambda t: (t, 0)),
            pl.BlockSpec((tm, wide), lambda t: (t, 0)),
            pl.BlockSpec((tm, MLA_HEADS * V_HEAD), lambda t: (t, 0)),
        ],
        out_shape=[
            jax.ShapeDtypeStruct((TOKENS, wide), BF16),
            jax.ShapeDtypeStruct((TOKENS, wide), BF16),
            jax.ShapeDtypeStruct((TOKENS, MLA_HEADS * V_HEAD), BF16),
        ],
        compiler_params=_params("arbitrary"),
        name="mla_proj",
    )(x, mod, mod, pre, win, qn, kvn, wq, wk, wv, *tables)


def _mla_attn_kernel(q_ref, k_ref, v_ref, o_ref):
    tq = TQ_MLA
    row = lax.broadcasted_iota(jnp.int32, (tq, tq), 0)
    col = lax.broadcasted_iota(jnp.int32, (tq, tq), 1)
    causal = col <= row
    first_head = lax.broadcasted_iota(jnp.int32, (tq, 2 * V_HEAD), 1) < V_HEAD
    for qi in range(SEQ // tq):
        r0 = qi * tq
        rows = slice(r0, r0 + tq)
        outs = []
        for hh in range(2):
            hs = slice(hh * HEAD_SLOT, (hh + 1) * HEAD_SLOT)
            q = q_ref[rows, hs]
            s_d = jnp.where(causal, _dot_nt(q, k_ref[rows, hs]), MASK_VALUE)
            m = jnp.max(s_d, axis=-1, keepdims=True)
            if qi > 0:
                s_o = _dot_nt(q, k_ref[0:r0, hs])
                m = jnp.maximum(m, jnp.max(s_o, axis=-1, keepdims=True))
            p_d = jnp.exp(s_d - m)
            l = jnp.sum(p_d, axis=-1, keepdims=True)
            o = _dot(p_d.astype(BF16), v_ref[rows, :])
            if qi > 0:
                p_o = jnp.exp(s_o - m)
                l = l + jnp.sum(p_o, axis=-1, keepdims=True)
                o = o + _dot(p_o.astype(BF16), v_ref[0:r0, :])
            outs.append(o * (1.0 / l))
        o_ref[rows, :] = jnp.where(first_head, outs[0], outs[1]).astype(BF16)


def _mla_attn(q, k, v):
    pairs = MLA_HEADS // 2
    return pl.pallas_call(
        _mla_attn_kernel,
        grid=(BATCH, pairs),
        in_specs=[
            pl.BlockSpec((SEQ, 2 * HEAD_SLOT), lambda b, p: (b, p)),
            pl.BlockSpec((SEQ, 2 * HEAD_SLOT), lambda b, p: (b, p)),
            pl.BlockSpec((SEQ, 2 * V_HEAD), lambda b, p: (b, p)),
        ],
        out_specs=pl.BlockSpec((SEQ, 2 * V_HEAD), lambda b, p: (b, p)),
        out_shape=jax.ShapeDtypeStruct((TOKENS, MLA_HEADS * V_HEAD), BF16),
        compiler_params=_params("arbitrary", "arbitrary"),
        name="mla_attn",
    )(q, k, v)


def _out_proj_kernel(x_ref, a_ref, gate_ref, post_ref, w_ref, o_ref):
    y = _dot(a_ref[...], w_ref[...])
    o_ref[...] = x_ref[...] + gate_ref[...] * _rms(y, post_ref[...])


def _out_proj(x, a, mod, post, w, layer):
    tm = TM_OUT
    return pl.pallas_call(
        _out_proj_kernel,
        grid=(TOKENS // tm,),
        in_specs=[
            pl.BlockSpec((tm, D_MODEL), lambda t: (t, 0)),
            pl.BlockSpec((tm, a.shape[1]), lambda t: (t, 0)),
            _mod_spec(layer, 1, 2, tm),
            _row_spec(layer * N_SUB + 1),
            pl.BlockSpec(w.shape, lambda t: (0, 0)),
        ],
        out_specs=pl.BlockSpec((tm, D_MODEL), lambda t: (t, 0)),
        out_shape=jax.ShapeDtypeStruct((TOKENS, D_MODEL), F32),
        compiler_params=_params("arbitrary"),
        name="out_proj",
    )(x, a, mod, post, w)


def _t5_bucket(dist):
    max_exact = N_BUCKETS // 2
    d = jnp.maximum(dist, 1).astype(F32)
    large = max_exact + (jnp.log(d / max_exact) / math.log(MAX_DISTANCE / max_exact)
                         * (N_BUCKETS - max_exact)).astype(jnp.int32)
    large = jnp.minimum(large, N_BUCKETS - 1)
    return jnp.where(dist < max_exact, dist, large)


def _bias_selectors():
    j = jnp.arange(2 * DIL_BLOCK)
    rel = DIL_BLOCK - j
    sel = []
    for window, dilation in DIL_GROUPS:
        assert window // dilation == DIL_BLOCK
        bucket = _t5_bucket(jnp.maximum(rel, 0) * dilation)
        onehot = (bucket[None, :] == jnp.arange(N_BUCKETS)[:, None]) & (rel >= 0)[None, :]
        sel.append(onehot.astype(F32))
    return jnp.stack(sel)


def _bias_kernel(rbt_ref, sel_ref, o_ref):
    rbt = rbt_ref[...]
    sel = sel_ref[...]
    row0 = jnp.zeros((DIL_HEADS, 2 * DIL_BLOCK), F32)
    for b in range(N_BUCKETS):
        row0 = row0 + rbt[:, b:b + 1] * sel[b:b + 1, :]
    future = jnp.sum(sel, axis=0, keepdims=True) < 0.5
    row0 = jnp.where(future, MASK_VALUE, row0)
    sub = lax.broadcasted_iota(jnp.int32, (DIL_BLOCK, 2 * DIL_BLOCK), 0)
    for h in range(DIL_HEADS):
        t = jnp.broadcast_to(row0[h:h + 1, :], (DIL_BLOCK, 2 * DIL_BLOCK))
        for bit in range(DIL_BLOCK.bit_length() - 1):
            t = jnp.where(((sub >> bit) & 1) == 1, pltpu.roll(t, 1 << bit, 1), t)
        o_ref[h] = t


def _bias_tables(rel_bias):
    rbt = rel_bias.T.reshape(N_GROUPS, DIL_HEADS, N_BUCKETS)
    return pl.pallas_call(
        _bias_kernel,
        grid=(N_GROUPS,),
        in_specs=[
            pl.BlockSpec((None, DIL_HEADS, N_BUCKETS), lambda g: (g, 0, 0)),
            pl.BlockSpec((None, N_BUCKETS, 2 * DIL_BLOCK), lambda g: (g, 0, 0)),
        ],
        out_specs=pl.BlockSpec((DIL_HEADS, DIL_BLOCK, 2 * DIL_BLOCK), lambda g: (g, 0, 0)),
        out_shape=jax.ShapeDtypeStruct((N_GROUPS * DIL_HEADS, DIL_BLOCK, 2 * DIL_BLOCK), F32),
        compiler_params=_params("arbitrary"),
        name="dil_bias",
    )(rbt, _bias_selectors())


def _dil_proj_kernel(x_ref, shift_ref, scale_ref, pre_ref, w_ref, o_ref, hn_ref):
    j = pl.program_id(1)

    @pl.when(j == 0)
    def _():
        hn = _rms(x_ref[...], pre_ref[...]) * (1.0 + scale_ref[...]) + shift_ref[...]
        hn_ref[...] = hn.astype(BF16)

    y = _dot(hn_ref[...], w_ref[...])
    qscale = jnp.where(j % 3 == 0, DIL_HEAD_DIM ** -0.5, 1.0)
    o_ref[...] = (y * qscale).astype(BF16)


def _dil_proj(x, mod, pre, w, layer):
    tm, tn = TM_DIL_PROJ, TN_DIL_PROJ
    assert tn == DIL_WIDTH
    n = N_GROUPS * 3 * DIL_WIDTH
    return pl.pallas_call(
        _dil_proj_kernel,
        grid=(TOKENS // tm, n // tn),
        in_specs=[
            pl.BlockSpec((tm, D_MODEL), lambda t, j: (t, 0)),
            _mod_spec(layer, 1, 0, tm), _mod_spec(layer, 1, 1, tm),
            _row_spec(layer * N_SUB + 1),
            pl.BlockSpec((D_MODEL, tn), lambda t, j: (0, j)),
        ],
        out_specs=pl.BlockSpec((tm, tn), lambda t, j: (t, j)),
        out_shape=jax.ShapeDtypeStruct((TOKENS, n), BF16),
        scratch_shapes=[pltpu.VMEM((tm, D_MODEL), BF16)],
        compiler_params=_params("arbitrary", "arbitrary"),
        name="dil_proj",
    )(x, mod, mod, pre, w)


def _dil_attn_kernel(q_ref, k_ref, v_ref, bias_ref, o_ref, lse_ref, *, n_blocks):
    blk = DIL_BLOCK
    lane = lax.broadcasted_iota(jnp.int32, (blk, LANES), 1)
    first_head = lane < DIL_HEAD_DIM

    def block(r0, first):
        lse_tile = jnp.zeros((blk, LANES), F32)
        for hp in range(DIL_HEADS // 2):
            cs = slice(hp * LANES, (hp + 1) * LANES)
            q2 = q_ref[pl.ds(r0, blk), cs]
            if first:
                k2, v2 = k_ref[0:blk, cs], v_ref[0:blk, cs]
            else:
                prev = pl.multiple_of(r0 - blk, blk)
                k2, v2 = k_ref[pl.ds(prev, 2 * blk), cs], v_ref[pl.ds(prev, 2 * blk), cs]
            pair = None
            for hh in range(2):
                h = 2 * hp + hh
                qh = jnp.where(first_head if hh == 0 else jnp.logical_not(first_head), q2, 0)
                bias = bias_ref[h, :, blk:] if first else bias_ref[h]
                s = _dot_nt(qh, k2) + bias
                m = jnp.max(s, axis=-1, keepdims=True)
                p = jnp.exp(s - m)
                l = jnp.sum(p, axis=-1, keepdims=True)
                o = _dot(p.astype(BF16), v2) * (1.0 / l)
                lse_tile = jnp.where(lane == h, m + jnp.log(l), lse_tile)
                pair = o if hh == 0 else jnp.where(first_head, pair, o)
            o_ref[pl.ds(r0, blk), cs] = pair.astype(BF16)
        lse_ref[pl.ds(r0, blk), :] = lse_tile

    block(0, True)
    if n_blocks > 1:
        def body(n, carry):
            block(pl.multiple_of(n * blk, blk), False)
            return carry
        lax.fori_loop(1, n_blocks, body, 0)


def _dil_attn(proj, bias, group):
    window, dilation = DIL_GROUPS[group]
    length = SEQ // dilation
    n_cols = N_GROUPS * 3
    view = proj.reshape(BATCH, length, dilation * n_cols * DIL_WIDTH)
    qkv_spec = lambda kind: pl.BlockSpec(
        (None, length, DIL_WIDTH), lambda b, r: (b, 0, r * n_cols + group * 3 + kind))
    o, lse = pl.pallas_call(
        functools.partial(_dil_attn_kernel, n_blocks=length // DIL_BLOCK),
        grid=(BATCH, dilation),
        in_specs=[
            qkv_spec(0), qkv_spec(1), qkv_spec(2),
            pl.BlockSpec((DIL_HEADS, DIL_BLOCK, 2 * DIL_BLOCK), lambda b, r: (group, 0, 0)),
        ],
        out_specs=[
            pl.BlockSpec((None, length, DIL_WIDTH), lambda b, r: (b, 0, r)),
            pl.BlockSpec((None, length, LANES), lambda b, r: (b, 0, r)),
        ],
        out_shape=[
            jax.ShapeDtypeStruct((BATCH, length, dilation * DIL_WIDTH), BF16),
            jax.ShapeDtypeStruct((BATCH, length, dilation * LANES), F32),
        ],
        compiler_params=_params("arbitrary", "arbitrary"),
        name=f"dil_attn_g{group}",
    )(view, view, view, bias)
    return o.reshape(TOKENS, DIL_WIDTH), lse.reshape(TOKENS, LANES)


def _dil_out_kernel(x_ref, o0_ref, o1_ref, o2_ref, l0_ref, l1_ref, l2_ref, e_ref,
                    gate_ref, post_ref, w_ref, out_ref):
    lses = [l0_ref[...], l1_ref[...], l2_ref[...]]
    m = jnp.maximum(jnp.maximum(lses[0], lses[1]), lses[2])
    es = [jnp.exp(l - m) for l in lses]
    inv = 1.0 / (es[0] + es[1] + es[2])
    e = e_ref[...]
    mix = None
    for eg, o_ref in zip(es, (o0_ref, o1_ref, o2_ref)):
        alpha = eg * inv
        hi = alpha.astype(BF16)
        lo = (alpha - hi.astype(F32)).astype(BF16)
        wide = _dot(hi, e) + _dot(lo, e)
        term = wide * o_ref[...].astype(F32)
        mix = term if mix is None else mix + term
    y = _dot(mix.astype(BF16), w_ref[...])
    out_ref[...] = x_ref[...] + gate_ref[...] * _rms(y, post_ref[...])


def _dil_out(x, outs, lses, mod, post, w, layer):
    tm = TM_OUT
    head_of_col = jnp.arange(DIL_WIDTH) // DIL_HEAD_DIM
    expand = (jnp.arange(LANES)[:, None] == head_of_col[None, :]).astype(BF16)
    tok = lambda width: pl.BlockSpec((tm, width), lambda t: (t, 0))
    return pl.pallas_call(
        _dil_out_kernel,
        grid=(TOKENS // tm,),
        in_specs=[
            tok(D_MODEL), tok(DIL_WIDTH), tok(DIL_WIDTH), tok(DIL_WIDTH),
            tok(LANES), tok(LANES), tok(LANES),
            pl.BlockSpec((LANES, DIL_WIDTH), lambda t: (0, 0)),
            _mod_spec(layer, 1, 2, tm),
            _row_spec(layer * N_SUB + 1),
            pl.BlockSpec(w.shape, lambda t: (0, 0)),
        ],
        out_specs=tok(D_MODEL),
        out_shape=jax.ShapeDtypeStruct((TOKENS, D_MODEL), F32),
        compiler_params=_params("arbitrary"),
        name="dil_out",
    )(x, *outs, *lses, expand, mod, post, w)


def kernel(x, c, norm_pre, norm_post, w_mod, b_mod, ffn_w_gate, ffn_w_up, ffn_w_down,
           mla_w_in, mla_q_norm, mla_w_q_up, mla_kv_norm, mla_w_kv_up, mla_w_o,
           dil_w_in, dil_w_o, rel_bias):
    assert x.shape == (BATCH, SEQ, D_MODEL) and x.dtype == F32
    h = x.reshape(TOKENS, D_MODEL)
    mod = _modulation(c, w_mod, b_mod)
    pre = norm_pre.reshape(DEPTH * N_SUB, 1, D_MODEL)
    post = norm_post.reshape(DEPTH * N_SUB, 1, D_MODEL)
    wg, wu, wd = ffn_w_gate.astype(BF16), ffn_w_up.astype(BF16), ffn_w_down.astype(BF16)
    tables = _rope_tables()
    bias = _bias_tables(rel_bias)

    for layer in range(DEPTH):
        h = _ffn(h, mod, pre, post, wg, wu, wd, layer, 0)
        idx = layer // 2
        if layer % 2 == 0:
            win, wq, wk, wv = _mla_weights(mla_w_in[idx], mla_w_q_up[idx], mla_w_kv_up[idx])
            q, k, v = _mla_proj(h, mod, pre, win, mla_q_norm[idx][None, :], mla_kv_norm[idx][None, :],
                                wq, wk, wv, tables, layer)
            a = _mla_attn(q, k, v)
            h = _out_proj(h, a, mod, post, mla_w_o[idx].astype(BF16), layer)
        else:
            proj = _dil_proj(h, mod, pre, dil_w_in[idx].astype(BF16), layer)
            outs, lses = zip(*[_dil_attn(proj, bias, g) for g in range(N_GROUPS)])
            h = _dil_out(h, outs, lses, mod, post, dil_w_o[idx].astype(BF16), layer)
        h = _ffn(h, mod, pre, post, wg, wu, wd, layer, 1)
    return h.reshape(BATCH, SEQ, D_MODEL)
```

```python
import functools
import math

import jax
import jax.numpy as jnp
from jax import lax
from jax.experimental import pallas as pl
from jax.experimental.pallas import tpu as pltpu

F32 = jnp.float32
BF16 = jnp.bfloat16

D_MODEL = 1024
BATCH = 8
SEQ = 2048
DEPTH = 2
N_SUB = 3
D_FF = 2816
FFN_RES = 0.5
EPS = 1e-6

MLA_HEADS = 16
Q_LORA = 384
KV_LORA = 256
QK_NOPE = 64
QK_ROPE = 32
V_HEAD = 64
ROPE_THETA = 10000.0

DIL_GROUPS = ((128, 1), (512, 4), (2048, 16))
N_GROUPS = 3
DIL_HEADS = 16
DIL_HEAD_DIM = 64
DIL_BLOCK = 128
DIL_WIDTH = DIL_HEADS * DIL_HEAD_DIM
N_BUCKETS = 32
MAX_DISTANCE = 2048

TOKENS = BATCH * SEQ
LANES = 128
HEAD_SLOT = 128
ROPE_HALF = QK_ROPE // 2
MASK_VALUE = -1e30
VMEM_LIMIT = 56 * 1024 * 1024

TM_FFN = 512
TM_PROJ = 512
TM_DIL_PROJ = 1024
TN_DIL_PROJ = 1024
TM_OUT = 512
TQ_MLA = 256
TN_MOD = 1536
FF_CHUNK = 512


def _params(*sem):
    return pltpu.CompilerParams(dimension_semantics=sem, vmem_limit_bytes=VMEM_LIMIT)


def _rms(x, g):
    ms = jnp.mean(x * x, axis=-1, keepdims=True)
    return x * lax.rsqrt(ms + EPS) * g


def _dot(a, b):
    return jnp.dot(a, b, preferred_element_type=F32)


def _dot_nt(a, b):
    return lax.dot_general(a, b, (((1,), (1,)), ((), ())), preferred_element_type=F32)


def _silu(x):
    return x * jax.nn.sigmoid(x)


def _mod_spec(layer, sub, kind, tm):
    tiles_per_batch = SEQ // tm
    return pl.BlockSpec(
        (None, 1, D_MODEL),
        lambda t, *_: ((layer * BATCH + t // tiles_per_batch) * (N_SUB * 3) + sub * 3 + kind, 0, 0))


def _row_spec(index):
    return pl.BlockSpec((None, 1, D_MODEL), lambda *_: (index, 0, 0))


def _mod_kernel(c_ref, w_ref, b_ref, o_ref):
    h = _silu(c_ref[...])
    o_ref[...] = jnp.dot(h, w_ref[...], preferred_element_type=F32,
                         precision=lax.Precision.HIGHEST) + b_ref[...]


def _modulation(c, w_mod, b_mod):
    n = N_SUB * 3 * D_MODEL
    out = pl.pallas_call(
        _mod_kernel,
        grid=(DEPTH, n // TN_MOD),
        in_specs=[
            pl.BlockSpec((BATCH, D_MODEL), lambda i, j: (0, 0)),
            pl.BlockSpec((None, D_MODEL, TN_MOD), lambda i, j: (i, 0, j)),
            pl.BlockSpec((None, 1, TN_MOD), lambda i, j: (i, 0, j)),
        ],
        out_specs=pl.BlockSpec((None, BATCH, TN_MOD), lambda i, j: (i, 0, j)),
        out_shape=jax.ShapeDtypeStruct((DEPTH, BATCH, n), F32),
        compiler_params=_params("arbitrary", "arbitrary"),
        name="modulation",
    )(c, w_mod, b_mod.reshape(DEPTH, 1, n))
    return out.reshape(DEPTH * BATCH * N_SUB * 3, 1, D_MODEL)


def _ffn_kernel(x_ref, shift_ref, scale_ref, gate_ref, pre_ref, post_ref,
                wg_ref, wu_ref, wd_ref, o_ref, a_ref):
    x = x_ref[...]
    hn = (_rms(x, pre_ref[...]) * (1.0 + scale_ref[...]) + shift_ref[...]).astype(BF16)
    for c0 in range(0, D_FF, FF_CHUNK):
        c1 = min(c0 + FF_CHUNK, D_FF)
        g = _dot(hn, wg_ref[:, c0:c1])
        u = _dot(hn, wu_ref[:, c0:c1])
        a_ref[:, c0:c1] = (_silu(g) * u).astype(BF16)
    y = _dot(a_ref[...], wd_ref[...])
    o_ref[...] = x + FFN_RES * gate_ref[...] * _rms(y, post_ref[...])


def _ffn(x, mod, pre, post, wg, wu, wd, layer, which):
    sub = 0 if which == 0 else 2
    tm = TM_FFN
    w_in_spec = pl.BlockSpec((None, None, D_MODEL, D_FF), lambda t: (layer, which, 0, 0),
                             pipeline_mode=pl.Buffered(1))
    w_out_spec = pl.BlockSpec((None, None, D_FF, D_MODEL), lambda t: (layer, which, 0, 0),
                              pipeline_mode=pl.Buffered(1))
    return pl.pallas_call(
        _ffn_kernel,
        grid=(TOKENS // tm,),
        in_specs=[
            pl.BlockSpec((tm, D_MODEL), lambda t: (t, 0)),
            _mod_spec(layer, sub, 0, tm), _mod_spec(layer, sub, 1, tm), _mod_spec(layer, sub, 2, tm),
            _row_spec(layer * N_SUB + sub), _row_spec(layer * N_SUB + sub),
            w_in_spec, w_in_spec, w_out_spec,
        ],
        out_specs=pl.BlockSpec((tm, D_MODEL), lambda t: (t, 0)),
        out_shape=jax.ShapeDtypeStruct((TOKENS, D_MODEL), F32),
        scratch_shapes=[pltpu.VMEM((tm, D_FF), BF16)],
        compiler_params=_params("arbitrary"),
        name="ffn",
    )(x, mod, mod, mod, pre, post, wg, wu, wd)


def _rope(z, ta, tb, tc):
    return z * ta + pltpu.roll(z, ROPE_HALF, 1) * tb + pltpu.roll(z, HEAD_SLOT - ROPE_HALF, 1) * tc


def _mla_proj_kernel(x_ref, shift_ref, scale_ref, pre_ref, win_ref, qn_ref, kvn_ref,
                     wq_ref, wk_ref, wv_ref, ta_ref, tb_ref, tc_ref, q_ref, k_ref, v_ref):
    x = x_ref[...]
    hn = (_rms(x, pre_ref[...]) * (1.0 + scale_ref[...]) + shift_ref[...]).astype(BF16)
    lat = _dot(hn, win_ref[...])
    cq = _rms(lat[:, :Q_LORA], qn_ref[...]).astype(BF16)
    ckv = _rms(lat[:, Q_LORA:Q_LORA + KV_LORA], kvn_ref[...]).astype(BF16)
    ta, tb, tc = ta_ref[...], tb_ref[...], tc_ref[...]
    kr = _rope(lat[:, Q_LORA + KV_LORA:], ta, tb, tc)
    q = _dot(cq, wq_ref[...]) * ((QK_NOPE + QK_ROPE) ** -0.5)
    kn = _dot(ckv, wk_ref[...])
    v_ref[...] = _dot(ckv, wv_ref[...]).astype(BF16)
    for h in range(MLA_HEADS):
        sl = slice(h * HEAD_SLOT, (h + 1) * HEAD_SLOT)
        q_ref[:, sl] = _rope(q[:, sl], ta, tb, tc).astype(BF16)
        k_ref[:, sl] = (kn[:, sl] + kr).astype(BF16)


def _rope_tables():
    pos = jnp.arange(SEQ, dtype=F32)
    freqs = ROPE_THETA ** (-jnp.arange(ROPE_HALF, dtype=F32) / ROPE_HALF)
    ang = pos[:, None] * freqs[None, :]
    cos, sin = jnp.cos(ang), jnp.sin(ang)
    one = jnp.ones((SEQ, QK_NOPE), F32)
    zero = jnp.zeros((SEQ, QK_NOPE), F32)
    z16 = jnp.zeros((SEQ, ROPE_HALF), F32)
    pad = HEAD_SLOT - QK_NOPE - QK_ROPE
    ta = jnp.concatenate([one, cos, cos, one[:, :pad]], axis=1)
    tb = jnp.concatenate([zero, z16, sin, zero[:, :pad]], axis=1)
    tc = jnp.concatenate([zero, -sin, z16, zero[:, :pad]], axis=1)
    return ta, tb, tc


def _mla_weights(w_in, w_q_up, w_kv_up):
    pad = HEAD_SLOT - QK_NOPE - QK_ROPE
    kr_cols = jnp.pad(w_in[:, Q_LORA + KV_LORA:], ((0, 0), (QK_NOPE, pad)))
    win = jnp.concatenate([w_in[:, :Q_LORA + KV_LORA], kr_cols], axis=1).astype(BF16)
    wq = jnp.pad(w_q_up.reshape(Q_LORA, MLA_HEADS, QK_NOPE + QK_ROPE), ((0, 0), (0, 0), (0, pad)))
    wq = wq.reshape(Q_LORA, MLA_HEADS * HEAD_SLOT).astype(BF16)
    wkv = w_kv_up.reshape(KV_LORA, MLA_HEADS, QK_NOPE + V_HEAD)
    wk = jnp.pad(wkv[:, :, :QK_NOPE], ((0, 0), (0, 0), (0, HEAD_SLOT - QK_NOPE)))
    wk = wk.reshape(KV_LORA, MLA_HEADS * HEAD_SLOT).astype(BF16)
    wv = wkv[:, :, QK_NOPE:].reshape(KV_LORA, MLA_HEADS * V_HEAD).astype(BF16)
    return win, wq, wk, wv


def _mla_proj(x, mod, pre, win, qn, kvn, wq, wk, wv, tables, layer):
    tm = TM_PROJ
    tiles_per_batch = SEQ // tm
    full = lambda a: pl.BlockSpec(a.shape, lambda t: (0,) * a.ndim)
    tab_spec = pl.BlockSpec((tm, HEAD_SLOT), lambda t: (t % tiles_per_batch, 0))
    wide = MLA_HEADS * HEAD_SLOT
    return pl.pallas_call(
        _mla_proj_kernel,
        grid=(TOKENS // tm,),
        in_specs=[
            pl.BlockSpec((tm, D_MODEL), lambda t: (t, 0)),
            _mod_spec(layer, 1, 0, tm), _mod_spec(layer, 1, 1, tm),
            _row_spec(layer * N_SUB + 1),
            full(win), full(qn), full(kvn), full(wq), full(wk), full(wv),
            tab_spec, tab_spec, tab_spec,
        ],
        out_specs=[
            pl.BlockSpec((tm, wide), lambda t: (t, 0)),
            pl.BlockSpec((tm, wide), lambda t: (t, 0)),
            pl.BlockSpec((tm, MLA_HEADS * V_HEAD), lambda t: (t, 0)),
        ],
        out_shape=[
            jax.ShapeDtypeStruct((TOKENS, wide), BF16),
            jax.ShapeDtypeStruct((TOKENS, wide), BF16),
            jax.ShapeDtypeStruct((TOKENS, MLA_HEADS * V_HEAD), BF16),
        ],
        compiler_params=_params("arbitrary"),
        name="mla_proj",
    )(x, mod, mod, pre, win, qn, kvn, wq, wk, wv, *tables)


def _mla_attn_kernel(q_ref, k_ref, v_ref, o_ref, vaug_ref):
    tq = TQ_MLA
    row = lax.broadcasted_iota(jnp.int32, (tq, tq), 0)
    col = lax.broadcasted_iota(jnp.int32, (tq, tq), 1)
    causal = col <= row
    first_head = lax.broadcasted_iota(jnp.int32, (tq, 2 * V_HEAD), 1) < V_HEAD
    vaug_ref[:, :LANES] = v_ref[...]
    vaug_ref[:, LANES:] = jnp.ones((SEQ, LANES), BF16)

    def logits(qi, hh):
        r0 = qi * tq
        hs = slice(hh * HEAD_SLOT, (hh + 1) * HEAD_SLOT)
        q = q_ref[r0:r0 + tq, hs]
        s_d = jnp.where(causal, _dot_nt(q, k_ref[r0:r0 + tq, hs]), MASK_VALUE)
        s_o = _dot_nt(q, k_ref[0:r0, hs]) if qi > 0 else None
        return s_d, s_o

    def attend(qi, s_d, s_o):
        r0 = qi * tq
        m = jnp.max(s_d, axis=-1, keepdims=True)
        if s_o is not None:
            m = jnp.maximum(m, jnp.max(s_o, axis=-1, keepdims=True))
        acc = _dot(jnp.exp(s_d - m).astype(BF16), vaug_ref[r0:r0 + tq, :])
        if s_o is not None:
            acc = acc + _dot(jnp.exp(s_o - m).astype(BF16), vaug_ref[0:r0, :])
        return acc[:, :LANES] * (1.0 / acc[:, LANES:])

    units = [(qi, hh) for qi in range(SEQ // tq) for hh in range(2)]
    pending = logits(*units[0])
    for i, (qi, hh) in enumerate(units):
        ahead = logits(*units[i + 1]) if i + 1 < len(units) else None
        o = attend(qi, *pending)
        if hh == 0:
            o_first = o
        else:
            o_ref[qi * tq:(qi + 1) * tq, :] = jnp.where(first_head, o_first, o).astype(BF16)
        pending = ahead


def _mla_attn(q, k, v):
    pairs = MLA_HEADS // 2
    return pl.pallas_call(
        _mla_attn_kernel,
        grid=(BATCH, pairs),
        in_specs=[
            pl.BlockSpec((SEQ, 2 * HEAD_SLOT), lambda b, p: (b, p)),
            pl.BlockSpec((SEQ, 2 * HEAD_SLOT), lambda b, p: (b, p)),
            pl.BlockSpec((SEQ, 2 * V_HEAD), lambda b, p: (b, p)),
        ],
        out_specs=pl.BlockSpec((SEQ, 2 * V_HEAD), lambda b, p: (b, p)),
        out_shape=jax.ShapeDtypeStruct((TOKENS, MLA_HEADS * V_HEAD), BF16),
        scratch_shapes=[pltpu.VMEM((SEQ, 2 * LANES), BF16)],
        compiler_params=_params("arbitrary", "arbitrary"),
        name="mla_attn",
    )(q, k, v)


def _out_proj_kernel(x_ref, a_ref, gate_ref, post_ref, w_ref, o_ref):
    y = _dot(a_ref[...], w_ref[...])
    o_ref[...] = x_ref[...] + gate_ref[...] * _rms(y, post_ref[...])


def _out_proj(x, a, mod, post, w, layer):
    tm = TM_OUT
    return pl.pallas_call(
        _out_proj_kernel,
        grid=(TOKENS // tm,),
        in_specs=[
            pl.BlockSpec((tm, D_MODEL), lambda t: (t, 0)),
            pl.BlockSpec((tm, a.shape[1]), lambda t: (t, 0)),
            _mod_spec(layer, 1, 2, tm),
            _row_spec(layer * N_SUB + 1),
            pl.BlockSpec(w.shape, lambda t: (0, 0)),
        ],
        out_specs=pl.BlockSpec((tm, D_MODEL), lambda t: (t, 0)),
        out_shape=jax.ShapeDtypeStruct((TOKENS, D_MODEL), F32),
        compiler_params=_params("arbitrary"),
        name="out_proj",
    )(x, a, mod, post, w)


def _t5_bucket(dist):
    max_exact = N_BUCKETS // 2
    d = jnp.maximum(dist, 1).astype(F32)
    large = max_exact + (jnp.log(d / max_exact) / math.log(MAX_DISTANCE / max_exact)
                         * (N_BUCKETS - max_exact)).astype(jnp.int32)
    large = jnp.minimum(large, N_BUCKETS - 1)
    return jnp.where(dist < max_exact, dist, large)


def _bias_selectors():
    j = jnp.arange(2 * DIL_BLOCK)
    rel = DIL_BLOCK - j
    sel = []
    for window, dilation in DIL_GROUPS:
        assert window // dilation == DIL_BLOCK
        bucket = _t5_bucket(jnp.maximum(rel, 0) * dilation)
        onehot = (bucket[None, :] == jnp.arange(N_BUCKETS)[:, None]) & (rel >= 0)[None, :]
        sel.append(onehot.astype(F32))
    return jnp.stack(sel)


def _bias_kernel(rbt_ref, sel_ref, o_ref):
    rbt = rbt_ref[...]
    sel = sel_ref[...]
    row0 = jnp.zeros((DIL_HEADS, 2 * DIL_BLOCK), F32)
    for b in range(N_BUCKETS):
        row0 = row0 + rbt[:, b:b + 1] * sel[b:b + 1, :]
    future = jnp.sum(sel, axis=0, keepdims=True) < 0.5
    row0 = jnp.where(future, MASK_VALUE, row0)
    sub = lax.broadcasted_iota(jnp.int32, (DIL_BLOCK, 2 * DIL_BLOCK), 0)
    for h in range(DIL_HEADS):
        t = jnp.broadcast_to(row0[h:h + 1, :], (DIL_BLOCK, 2 * DIL_BLOCK))
        for bit in range(DIL_BLOCK.bit_length() - 1):
            t = jnp.where(((sub >> bit) & 1) == 1, pltpu.roll(t, 1 << bit, 1), t)
        o_ref[h] = t


def _bias_tables(rel_bias):
    rbt = rel_bias.T.reshape(N_GROUPS, DIL_HEADS, N_BUCKETS)
    return pl.pallas_call(
        _bias_kernel,
        grid=(N_GROUPS,),
        in_specs=[
            pl.BlockSpec((None, DIL_HEADS, N_BUCKETS), lambda g: (g, 0, 0)),
            pl.BlockSpec((None, N_BUCKETS, 2 * DIL_BLOCK), lambda g: (g, 0, 0)),
        ],
        out_specs=pl.BlockSpec((DIL_HEADS, DIL_BLOCK, 2 * DIL_BLOCK), lambda g: (g, 0, 0)),
        out_shape=jax.ShapeDtypeStruct((N_GROUPS * DIL_HEADS, DIL_BLOCK, 2 * DIL_BLOCK), F32),
        compiler_params=_params("arbitrary"),
        name="dil_bias",
    )(rbt, _bias_selectors())


def _dil_proj_kernel(x_ref, shift_ref, scale_ref, pre_ref, w_ref, o_ref, hn_ref, slab_ref, *, dilation):
    j = pl.program_id(1)
    tm = x_ref.shape[0]
    sub = tm // dilation

    @pl.when(j == 0)
    def _():
        hn = _rms(x_ref[...], pre_ref[...]) * (1.0 + scale_ref[...]) + shift_ref[...]
        if dilation == 1:
            hn_ref[...] = hn.astype(BF16)
        else:
            for s in range(D_MODEL // LANES):
                slab_ref[s] = hn[:, s * LANES:(s + 1) * LANES]
            for r in range(dilation):
                for s in range(D_MODEL // LANES):
                    rows = slab_ref[s, pl.ds(r, sub, stride=dilation), :]
                    hn_ref[r * sub:(r + 1) * sub, s * LANES:(s + 1) * LANES] = rows.astype(BF16)

    y = _dot(hn_ref[...], w_ref[...])
    y = (y * jnp.where(j == 0, DIL_HEAD_DIM ** -0.5, 1.0)).astype(BF16)
    for r in range(dilation):
        o_ref[r] = y[r * sub:(r + 1) * sub, :]


def _dil_proj(x, mod, pre, w, layer, group):
    _, dilation = DIL_GROUPS[group]
    tm = TM_DIL_PROJ
    tiles_per_batch = SEQ // tm
    sub = tm // dilation
    return pl.pallas_call(
        functools.partial(_dil_proj_kernel, dilation=dilation),
        grid=(TOKENS // tm, 3),
        in_specs=[
            pl.BlockSpec((tm, D_MODEL), lambda t, j: (t, 0)),
            _mod_spec(layer, 1, 0, tm), _mod_spec(layer, 1, 1, tm),
            _row_spec(layer * N_SUB + 1),
            pl.BlockSpec((D_MODEL, DIL_WIDTH), lambda t, j: (0, group * 3 + j)),
        ],
        out_specs=pl.BlockSpec((None, dilation, sub, DIL_WIDTH),
                               lambda t, j: (t // tiles_per_batch, 0, t % tiles_per_batch, j)),
        out_shape=jax.ShapeDtypeStruct((BATCH, dilation, SEQ // dilation, 3 * DIL_WIDTH), BF16),
        scratch_shapes=[pltpu.VMEM((tm, D_MODEL), BF16),
                        pltpu.VMEM((D_MODEL // LANES, tm, LANES), F32)],
        compiler_params=_params("arbitrary", "arbitrary"),
        name=f"dil_proj_g{group}",
    )(x, mod, mod, pre, w)


DIL_HEADS_PER_STEP = 8
DIL_PAIRS_PER_STEP = DIL_HEADS_PER_STEP // 2


def _dil_attn_kernel(q_ref, k_ref, v_ref, bias_ref, o_ref, lse_ref, s_ref, *, dilation, n_blocks):
    blk = DIL_BLOCK
    lane = lax.broadcasted_iota(jnp.int32, (blk, LANES), 1)
    first_head = lane < DIL_HEAD_DIM
    ones = jnp.ones((2 * blk, LANES), BF16)

    def unit(r, n, first):
        kw = blk if first else 2 * blk
        q0 = 0 if first else pl.multiple_of(n * blk, blk)

        def keys(ref, cs):
            if first:
                return ref[r, 0:blk, cs]
            return ref[r, pl.ds(pl.multiple_of(n * blk - blk, blk), 2 * blk), cs]

        for hp in range(DIL_PAIRS_PER_STEP):
            cs = slice(hp * LANES, (hp + 1) * LANES)
            q2 = q_ref[r, pl.ds(q0, blk), cs]
            k2 = keys(k_ref, cs)
            for hh in range(2):
                h = 2 * hp + hh
                qh = jnp.where(first_head if hh == 0 else jnp.logical_not(first_head), q2, 0)
                bias = bias_ref[h, :, blk:] if first else bias_ref[h]
                s_ref[h, :, 0:kw] = _dot_nt(qh, k2) + bias

        t0 = n * (blk * dilation) + r
        rows = pl.ds(t0, blk) if dilation == 1 else pl.ds(t0, blk, stride=dilation)
        lse_tile = jnp.zeros((blk, LANES), F32)
        for hp in range(DIL_PAIRS_PER_STEP):
            cs = slice(hp * LANES, (hp + 1) * LANES)
            vaug = jnp.concatenate([keys(v_ref, cs), ones[:kw]], axis=1)
            pair = None
            for hh in range(2):
                h = 2 * hp + hh
                s = s_ref[h, :, 0:kw]
                m = jnp.max(s, axis=-1, keepdims=True)
                acc = _dot(jnp.exp(s - m).astype(BF16), vaug)
                l = acc[:, LANES:]
                o = acc[:, :LANES] * (1.0 / l)
                lse_tile = jnp.where(lane == h, m + jnp.log(l), lse_tile)
                pair = o if hh == 0 else jnp.where(first_head, pair, o)
            o_ref[hp, rows, :] = pair
        lse_ref[rows, :] = lse_tile

    def sub_sequence(r, carry):
        unit(r, 0, True)
        if n_blocks > 1:
            def body(n, c):
                unit(r, n, False)
                return c
            lax.fori_loop(1, n_blocks, body, 0)
        return carry

    if dilation == 1:
        sub_sequence(0, 0)
    else:
        lax.fori_loop(0, dilation, sub_sequence, 0)


def _dil_attn(proj, bias, group):
    _, dilation = DIL_GROUPS[group]
    length = SEQ // dilation
    halves = DIL_HEADS // DIL_HEADS_PER_STEP
    width = DIL_HEADS_PER_STEP * DIL_HEAD_DIM
    qkv_spec = lambda kind: pl.BlockSpec(
        (None, dilation, length, width), lambda b, u: (b, 0, 0, kind * halves + u))
    return pl.pallas_call(
        functools.partial(_dil_attn_kernel, dilation=dilation, n_blocks=length // DIL_BLOCK),
        grid=(BATCH, halves),
        in_specs=[
            qkv_spec(0), qkv_spec(1), qkv_spec(2),
            pl.BlockSpec((DIL_HEADS_PER_STEP, DIL_BLOCK, 2 * DIL_BLOCK),
                         lambda b, u: (group * halves + u, 0, 0)),
        ],
        out_specs=[
            pl.BlockSpec((None, DIL_PAIRS_PER_STEP, SEQ, LANES), lambda b, u: (b, u, 0, 0)),
            pl.BlockSpec((None, SEQ, LANES), lambda b, u: (b, 0, u)),
        ],
        out_shape=[
            jax.ShapeDtypeStruct((BATCH, DIL_HEADS // 2, SEQ, LANES), F32),
            jax.ShapeDtypeStruct((BATCH, SEQ, halves * LANES), F32),
        ],
        scratch_shapes=[pltpu.VMEM((DIL_HEADS_PER_STEP, DIL_BLOCK, 2 * DIL_BLOCK), F32)],
        compiler_params=_params("arbitrary", "arbitrary"),
        name=f"dil_attn_g{group}",
    )(proj, proj, proj, bias)


def _dil_out_kernel(x_ref, o0_ref, o1_ref, o2_ref, l0_ref, l1_ref, l2_ref, e_ref,
                    gate_ref, post_ref, w_ref, out_ref):
    lses = [l0_ref[...], l1_ref[...], l2_ref[...]]
    m = jnp.maximum(jnp.maximum(lses[0], lses[1]), lses[2])
    es = [jnp.exp(l - m) for l in lses]
    inv = 1.0 / (es[0] + es[1] + es[2])
    e = e_ref[...]
    mix = None
    for eg, o_ref in zip(es, (o0_ref, o1_ref, o2_ref)):
        alpha = eg * inv
        hi = alpha.astype(BF16)
        lo = (alpha - hi.astype(F32)).astype(BF16)
        wide = _dot(hi, e) + _dot(lo, e)
        o = jnp.concatenate([o_ref[p] for p in range(DIL_HEADS // 2)], axis=1)
        mix = wide * o if mix is None else mix + wide * o
    y = _dot(mix.astype(BF16), w_ref[...])
    out_ref[...] = x_ref[...] + gate_ref[...] * _rms(y, post_ref[...])


def _dil_out(x, outs, lses, mod, post, w, layer):
    tm = TM_OUT
    tiles_per_batch = SEQ // tm
    halves = DIL_HEADS // DIL_HEADS_PER_STEP
    head_of_col = jnp.arange(DIL_WIDTH) // DIL_HEAD_DIM
    lane = jnp.arange(halves * LANES)
    head_of_lane = jnp.where(lane % LANES < DIL_HEADS_PER_STEP,
                             (lane // LANES) * DIL_HEADS_PER_STEP + lane % LANES, -1)
    expand = (head_of_lane[:, None] == head_of_col[None, :]).astype(BF16)
    tok = lambda width: pl.BlockSpec((tm, width), lambda t: (t, 0))
    pairs = pl.BlockSpec((None, DIL_HEADS // 2, tm, LANES),
                         lambda t: (t // tiles_per_batch, 0, t % tiles_per_batch, 0))
    lse_spec = pl.BlockSpec((None, tm, halves * LANES),
                            lambda t: (t // tiles_per_batch, t % tiles_per_batch, 0))
    return pl.pallas_call(
        _dil_out_kernel,
        grid=(TOKENS // tm,),
        in_specs=[
            tok(D_MODEL), pairs, pairs, pairs, lse_spec, lse_spec, lse_spec,
            pl.BlockSpec(expand.shape, lambda t: (0, 0)),
            _mod_spec(layer, 1, 2, tm),
            _row_spec(layer * N_SUB + 1),
            pl.BlockSpec(w.shape, lambda t: (0, 0)),
        ],
        out_specs=tok(D_MODEL),
        out_shape=jax.ShapeDtypeStruct((TOKENS, D_MODEL), F32),
        compiler_params=_params("arbitrary"),
        name="dil_out",
    )(x, *outs, *lses, expand, mod, post, w)


def kernel(x, c, norm_pre, norm_post, w_mod, b_mod, ffn_w_gate, ffn_w_up, ffn_w_down,
           mla_w_in, mla_q_norm, mla_w_q_up, mla_kv_norm, mla_w_kv_up, mla_w_o,
           dil_w_in, dil_w_o, rel_bias):
    assert x.shape == (BATCH, SEQ, D_MODEL) and x.dtype == F32
    h = x.reshape(TOKENS, D_MODEL)
    mod = _modulation(c, w_mod, b_mod)
    pre = norm_pre.reshape(DEPTH * N_SUB, 1, D_MODEL)
    post = norm_post.reshape(DEPTH * N_SUB, 1, D_MODEL)
    wg, wu, wd = ffn_w_gate.astype(BF16), ffn_w_up.astype(BF16), ffn_w_down.astype(BF16)
    tables = _rope_tables()
    bias = _bias_tables(rel_bias)

    for layer in range(DEPTH):
        h = _ffn(h, mod, pre, post, wg, wu, wd, layer, 0)
        idx = layer // 2
        if layer % 2 == 0:
            win, wq, wk, wv = _mla_weights(mla_w_in[idx], mla_w_q_up[idx], mla_w_kv_up[idx])
            q, k, v = _mla_proj(h, mod, pre, win, mla_q_norm[idx][None, :], mla_kv_norm[idx][None, :],
                                wq, wk, wv, tables, layer)
            a = _mla_attn(q, k, v)
            h = _out_proj(h, a, mod, post, mla_w_o[idx].astype(BF16), layer)
        else:
            w_in = dil_w_in[idx].astype(BF16)
            outs, lses = zip(*[_dil_attn(_dil_proj(h, mod, pre, w_in, layer, g), bias, g)
                               for g in range(N_GROUPS)])
            h = _dil_out(h, outs, lses, mod, post, dil_w_o[idx].astype(BF16), layer)
        h = _ffn(h, mod, pre, post, wg, wu, wd, layer, 1)
    return h.reshape(BATCH, SEQ, D_MODEL)
```

```python
import functools
import math

import jax
import jax.numpy as jnp
from jax import lax
from jax.experimental import pallas as pl
from jax.experimental.pallas import tpu as pltpu

F32 = jnp.float32
BF16 = jnp.bfloat16

D_MODEL = 1024
BATCH = 8
SEQ = 2048
DEPTH = 2
N_SUB = 3
D_FF = 2816
FFN_RES = 0.5
EPS = 1e-6

MLA_HEADS = 16
Q_LORA = 384
KV_LORA = 256
QK_NOPE = 64
QK_ROPE = 32
V_HEAD = 64
ROPE_THETA = 10000.0

DIL_GROUPS = ((128, 1), (512, 4), (2048, 16))
N_GROUPS = 3
DIL_HEADS = 16
DIL_HEAD_DIM = 64
DIL_BLOCK = 128
DIL_WIDTH = DIL_HEADS * DIL_HEAD_DIM
N_BUCKETS = 32
MAX_DISTANCE = 2048

TOKENS = BATCH * SEQ
LANES = 128
HEAD_SLOT = 128
ROPE_HALF = QK_ROPE // 2
MASK_VALUE = -1e30
LOG2_E = math.log2(math.e)
VMEM_LIMIT = 56 * 1024 * 1024

TM_FFN = 512
TM_PROJ = 512
TM_DIL_PROJ = 1024
TN_DIL_PROJ = 1024
TM_OUT = 512
TQ_MLA = 256
TN_MOD = 1536
FF_CHUNK = 512


def _params(*sem):
    return pltpu.CompilerParams(dimension_semantics=sem, vmem_limit_bytes=VMEM_LIMIT)


def _rms(x, g):
    ms = jnp.mean(x * x, axis=-1, keepdims=True)
    return x * lax.rsqrt(ms + EPS) * g


def _dot(a, b):
    return jnp.dot(a, b, preferred_element_type=F32)


def _dot_nt(a, b):
    return lax.dot_general(a, b, (((1,), (1,)), ((), ())), preferred_element_type=F32)


def _silu(x):
    return x * jax.nn.sigmoid(x)


def _mod_spec(layer, sub, kind, tm):
    tiles_per_batch = SEQ // tm
    return pl.BlockSpec(
        (None, 1, D_MODEL),
        lambda t, *_: ((layer * BATCH + t // tiles_per_batch) * (N_SUB * 3) + sub * 3 + kind, 0, 0))


def _row_spec(index):
    return pl.BlockSpec((None, 1, D_MODEL), lambda *_: (index, 0, 0))


def _mod_kernel(c_ref, w_ref, b_ref, o_ref):
    h = _silu(c_ref[...])
    o_ref[...] = jnp.dot(h, w_ref[...], preferred_element_type=F32,
                         precision=lax.Precision.HIGHEST) + b_ref[...]


def _modulation(c, w_mod, b_mod):
    n = N_SUB * 3 * D_MODEL
    out = pl.pallas_call(
        _mod_kernel,
        grid=(DEPTH, n // TN_MOD),
        in_specs=[
            pl.BlockSpec((BATCH, D_MODEL), lambda i, j: (0, 0)),
            pl.BlockSpec((None, D_MODEL, TN_MOD), lambda i, j: (i, 0, j)),
            pl.BlockSpec((None, 1, TN_MOD), lambda i, j: (i, 0, j)),
        ],
        out_specs=pl.BlockSpec((None, BATCH, TN_MOD), lambda i, j: (i, 0, j)),
        out_shape=jax.ShapeDtypeStruct((DEPTH, BATCH, n), F32),
        compiler_params=_params("arbitrary", "arbitrary"),
        name="modulation",
    )(c, w_mod, b_mod.reshape(DEPTH, 1, n))
    return out.reshape(DEPTH * BATCH * N_SUB * 3, 1, D_MODEL)


def _ffn_kernel(x_ref, shift_ref, scale_ref, gate_ref, pre_ref, post_ref,
                wg_ref, wu_ref, wd_ref, o_ref, a_ref):
    x = x_ref[...]
    hn = (_rms(x, pre_ref[...]) * (1.0 + scale_ref[...]) + shift_ref[...]).astype(BF16)
    for c0 in range(0, D_FF, FF_CHUNK):
        c1 = min(c0 + FF_CHUNK, D_FF)
        g = _dot(hn, wg_ref[:, c0:c1])
        u = _dot(hn, wu_ref[:, c0:c1])
        a_ref[:, c0:c1] = (_silu(g) * u).astype(BF16)
    y = _dot(a_ref[...], wd_ref[...])
    o_ref[...] = x + FFN_RES * gate_ref[...] * _rms(y, post_ref[...])


def _ffn(x, mod, pre, post, wg, wu, wd, layer, which):
    sub = 0 if which == 0 else 2
    tm = TM_FFN
    w_in_spec = pl.BlockSpec((None, None, D_MODEL, D_FF), lambda t: (layer, which, 0, 0),
                             pipeline_mode=pl.Buffered(1))
    w_out_spec = pl.BlockSpec((None, None, D_FF, D_MODEL), lambda t: (layer, which, 0, 0),
                              pipeline_mode=pl.Buffered(1))
    return pl.pallas_call(
        _ffn_kernel,
        grid=(TOKENS // tm,),
        in_specs=[
            pl.BlockSpec((tm, D_MODEL), lambda t: (t, 0)),
            _mod_spec(layer, sub, 0, tm), _mod_spec(layer, sub, 1, tm), _mod_spec(layer, sub, 2, tm),
            _row_spec(layer * N_SUB + sub), _row_spec(layer * N_SUB + sub),
            w_in_spec, w_in_spec, w_out_spec,
        ],
        out_specs=pl.BlockSpec((tm, D_MODEL), lambda t: (t, 0)),
        out_shape=jax.ShapeDtypeStruct((TOKENS, D_MODEL), F32),
        scratch_shapes=[pltpu.VMEM((tm, D_FF), BF16)],
        compiler_params=_params("arbitrary"),
        name="ffn",
    )(x, mod, mod, mod, pre, post, wg, wu, wd)


def _rope(z, ta, tb, tc):
    return z * ta + pltpu.roll(z, ROPE_HALF, 1) * tb + pltpu.roll(z, HEAD_SLOT - ROPE_HALF, 1) * tc


def _mla_proj_kernel(x_ref, shift_ref, scale_ref, pre_ref, win_ref, qn_ref, kvn_ref,
                     wq_ref, wk_ref, wv_ref, ta_ref, tb_ref, tc_ref, q_ref, k_ref, v_ref):
    x = x_ref[...]
    hn = (_rms(x, pre_ref[...]) * (1.0 + scale_ref[...]) + shift_ref[...]).astype(BF16)
    lat = _dot(hn, win_ref[...])
    cq = _rms(lat[:, :Q_LORA], qn_ref[...]).astype(BF16)
    ckv = _rms(lat[:, Q_LORA:Q_LORA + KV_LORA], kvn_ref[...]).astype(BF16)
    ta, tb, tc = ta_ref[...], tb_ref[...], tc_ref[...]
    kr = _rope(lat[:, Q_LORA + KV_LORA:], ta, tb, tc)
    q = _dot(cq, wq_ref[...]) * ((QK_NOPE + QK_ROPE) ** -0.5)
    kn = _dot(ckv, wk_ref[...])
    v_ref[...] = _dot(ckv, wv_ref[...]).astype(BF16)
    for h in range(MLA_HEADS):
        sl = slice(h * HEAD_SLOT, (h + 1) * HEAD_SLOT)
        q_ref[:, sl] = _rope(q[:, sl], ta, tb, tc).astype(BF16)
        k_ref[:, sl] = (kn[:, sl] + kr).astype(BF16)


def _rope_tables():
    pos = jnp.arange(SEQ, dtype=F32)
    freqs = ROPE_THETA ** (-jnp.arange(ROPE_HALF, dtype=F32) / ROPE_HALF)
    ang = pos[:, None] * freqs[None, :]
    cos, sin = jnp.cos(ang), jnp.sin(ang)
    one = jnp.ones((SEQ, QK_NOPE), F32)
    zero = jnp.zeros((SEQ, QK_NOPE), F32)
    z16 = jnp.zeros((SEQ, ROPE_HALF), F32)
    pad = HEAD_SLOT - QK_NOPE - QK_ROPE
    ta = jnp.concatenate([one, cos, cos, one[:, :pad]], axis=1)
    tb = jnp.concatenate([zero, z16, sin, zero[:, :pad]], axis=1)
    tc = jnp.concatenate([zero, -sin, z16, zero[:, :pad]], axis=1)
    return ta, tb, tc


def _mla_weights(w_in, w_q_up, w_kv_up):
    pad = HEAD_SLOT - QK_NOPE - QK_ROPE
    kr_cols = jnp.pad(w_in[:, Q_LORA + KV_LORA:], ((0, 0), (QK_NOPE, pad)))
    win = jnp.concatenate([w_in[:, :Q_LORA + KV_LORA], kr_cols], axis=1).astype(BF16)
    wq = jnp.pad(w_q_up.reshape(Q_LORA, MLA_HEADS, QK_NOPE + QK_ROPE), ((0, 0), (0, 0), (0, pad)))
    wq = wq.reshape(Q_LORA, MLA_HEADS * HEAD_SLOT).astype(BF16)
    wkv = w_kv_up.reshape(KV_LORA, MLA_HEADS, QK_NOPE + V_HEAD)
    wk = jnp.pad(wkv[:, :, :QK_NOPE], ((0, 0), (0, 0), (0, HEAD_SLOT - QK_NOPE)))
    wk = wk.reshape(KV_LORA, MLA_HEADS * HEAD_SLOT).astype(BF16)
    wv = wkv[:, :, QK_NOPE:].reshape(KV_LORA, MLA_HEADS * V_HEAD).astype(BF16)
    return win, wq, wk, wv


def _mla_proj(x, mod, pre, win, qn, kvn, wq, wk, wv, tables, layer):
    tm = TM_PROJ
    tiles_per_batch = SEQ // tm
    full = lambda a: pl.BlockSpec(a.shape, lambda t: (0,) * a.ndim)
    tab_spec = pl.BlockSpec((tm, HEAD_SLOT), lambda t: (t % tiles_per_batch, 0))
    wide = MLA_HEADS * HEAD_SLOT
    return pl.pallas_call(
        _mla_proj_kernel,
        grid=(TOKENS // tm,),
        in_specs=[
            pl.BlockSpec((tm, D_MODEL), lambda t: (t, 0)),
            _mod_spec(layer, 1, 0, tm), _mod_spec(layer, 1, 1, tm),
            _row_spec(layer * N_SUB + 1),
            full(win), full(qn), full(kvn), full(wq), full(wk), full(wv),
            tab_spec, tab_spec, tab_spec,
        ],
        out_specs=[
            pl.BlockSpec((tm, wide), lambda t: (t, 0)),
            pl.BlockSpec((tm, wide), lambda t: (t, 0)),
            pl.BlockSpec((tm, MLA_HEADS * V_HEAD), lambda t: (t, 0)),
        ],
        out_shape=[
            jax.ShapeDtypeStruct((TOKENS, wide), BF16),
            jax.ShapeDtypeStruct((TOKENS, wide), BF16),
            jax.ShapeDtypeStruct((TOKENS, MLA_HEADS * V_HEAD), BF16),
        ],
        compiler_params=_params("arbitrary"),
        name="mla_proj",
    )(x, mod, mod, pre, win, qn, kvn, wq, wk, wv, *tables)


def _mla_attn_kernel(q_ref, k_ref, v_ref, o_ref, vaug_ref):
    tq = TQ_MLA
    row = lax.broadcasted_iota(jnp.int32, (tq, tq), 0)
    col = lax.broadcasted_iota(jnp.int32, (tq, tq), 1)
    causal = col <= row
    first_head = lax.broadcasted_iota(jnp.int32, (tq, 2 * V_HEAD), 1) < V_HEAD
    vaug_ref[:, :LANES] = v_ref[...]
    vaug_ref[:, LANES:] = jnp.ones((SEQ, LANES), BF16)

    def logits(qi, hh):
        r0 = qi * tq
        hs = slice(hh * HEAD_SLOT, (hh + 1) * HEAD_SLOT)
        q = q_ref[r0:r0 + tq, hs]
        s_d = jnp.where(causal, _dot_nt(q, k_ref[r0:r0 + tq, hs]), MASK_VALUE)
        s_o = _dot_nt(q, k_ref[0:r0, hs]) if qi > 0 else None
        return s_d, s_o

    def attend(qi, s_d, s_o):
        r0 = qi * tq
        m = jnp.max(s_d, axis=-1, keepdims=True)
        if s_o is not None:
            m = jnp.maximum(m, jnp.max(s_o, axis=-1, keepdims=True))
        acc = _dot(jnp.exp(s_d - m).astype(BF16), vaug_ref[r0:r0 + tq, :])
        if s_o is not None:
            acc = acc + _dot(jnp.exp(s_o - m).astype(BF16), vaug_ref[0:r0, :])
        return acc[:, :LANES] * (1.0 / acc[:, LANES:])

    units = [(qi, hh) for qi in range(SEQ // tq) for hh in range(2)]
    pending = logits(*units[0])
    for i, (qi, hh) in enumerate(units):
        ahead = logits(*units[i + 1]) if i + 1 < len(units) else None
        o = attend(qi, *pending)
        if hh == 0:
            o_first = o
        else:
            o_ref[qi * tq:(qi + 1) * tq, :] = jnp.where(first_head, o_first, o).astype(BF16)
        pending = ahead


def _mla_attn(q, k, v):
    pairs = MLA_HEADS // 2
    return pl.pallas_call(
        _mla_attn_kernel,
        grid=(BATCH, pairs),
        in_specs=[
            pl.BlockSpec((SEQ, 2 * HEAD_SLOT), lambda b, p: (b, p)),
            pl.BlockSpec((SEQ, 2 * HEAD_SLOT), lambda b, p: (b, p)),
            pl.BlockSpec((SEQ, 2 * V_HEAD), lambda b, p: (b, p)),
        ],
        out_specs=pl.BlockSpec((SEQ, 2 * V_HEAD), lambda b, p: (b, p)),
        out_shape=jax.ShapeDtypeStruct((TOKENS, MLA_HEADS * V_HEAD), BF16),
        scratch_shapes=[pltpu.VMEM((SEQ, 2 * LANES), BF16)],
        compiler_params=_params("arbitrary", "arbitrary"),
        name="mla_attn",
    )(q, k, v)


def _out_proj_kernel(x_ref, a_ref, gate_ref, post_ref, w_ref, o_ref):
    y = _dot(a_ref[...], w_ref[...])
    o_ref[...] = x_ref[...] + gate_ref[...] * _rms(y, post_ref[...])


def _out_proj(x, a, mod, post, w, layer):
    tm = TM_OUT
    return pl.pallas_call(
        _out_proj_kernel,
        grid=(TOKENS // tm,),
        in_specs=[
            pl.BlockSpec((tm, D_MODEL), lambda t: (t, 0)),
            pl.BlockSpec((tm, a.shape[1]), lambda t: (t, 0)),
            _mod_spec(layer, 1, 2, tm),
            _row_spec(layer * N_SUB + 1),
            pl.BlockSpec(w.shape, lambda t: (0, 0)),
        ],
        out_specs=pl.BlockSpec((tm, D_MODEL), lambda t: (t, 0)),
        out_shape=jax.ShapeDtypeStruct((TOKENS, D_MODEL), F32),
        compiler_params=_params("arbitrary"),
        name="out_proj",
    )(x, a, mod, post, w)


def _t5_bucket(dist):
    max_exact = N_BUCKETS // 2
    d = jnp.maximum(dist, 1).astype(F32)
    large = max_exact + (jnp.log(d / max_exact) / math.log(MAX_DISTANCE / max_exact)
                         * (N_BUCKETS - max_exact)).astype(jnp.int32)
    large = jnp.minimum(large, N_BUCKETS - 1)
    return jnp.where(dist < max_exact, dist, large)


def _bias_selectors():
    j = jnp.arange(2 * DIL_BLOCK)
    rel = DIL_BLOCK - j
    sel = []
    for window, dilation in DIL_GROUPS:
        assert window // dilation == DIL_BLOCK
        bucket = _t5_bucket(jnp.maximum(rel, 0) * dilation)
        onehot = (bucket[None, :] == jnp.arange(N_BUCKETS)[:, None]) & (rel >= 0)[None, :]
        sel.append(onehot.astype(F32))
    return jnp.stack(sel)


def _bias_kernel(rbt_ref, sel_ref, o_ref):
    rbt = rbt_ref[...]
    sel = sel_ref[...]
    row0 = jnp.zeros((DIL_HEADS, 2 * DIL_BLOCK), F32)
    for b in range(N_BUCKETS):
        row0 = row0 + rbt[:, b:b + 1] * sel[b:b + 1, :]
    future = jnp.sum(sel, axis=0, keepdims=True) < 0.5
    row0 = jnp.where(future, MASK_VALUE, row0 * LOG2_E)
    sub = lax.broadcasted_iota(jnp.int32, (DIL_BLOCK, 2 * DIL_BLOCK), 0)
    for h in range(DIL_HEADS):
        t = jnp.broadcast_to(row0[h:h + 1, :], (DIL_BLOCK, 2 * DIL_BLOCK))
        for bit in range(DIL_BLOCK.bit_length() - 1):
            t = jnp.where(((sub >> bit) & 1) == 1, pltpu.roll(t, 1 << bit, 1), t)
        o_ref[h] = t


def _bias_tables(rel_bias):
    rbt = rel_bias.T.reshape(N_GROUPS, DIL_HEADS, N_BUCKETS)
    return pl.pallas_call(
        _bias_kernel,
        grid=(N_GROUPS,),
        in_specs=[
            pl.BlockSpec((None, DIL_HEADS, N_BUCKETS), lambda g: (g, 0, 0)),
            pl.BlockSpec((None, N_BUCKETS, 2 * DIL_BLOCK), lambda g: (g, 0, 0)),
        ],
        out_specs=pl.BlockSpec((DIL_HEADS, DIL_BLOCK, 2 * DIL_BLOCK), lambda g: (g, 0, 0)),
        out_shape=jax.ShapeDtypeStruct((N_GROUPS * DIL_HEADS, DIL_BLOCK, 2 * DIL_BLOCK), F32),
        compiler_params=_params("arbitrary"),
        name="dil_bias",
    )(rbt, _bias_selectors())


def _dil_proj_kernel(x_ref, shift_ref, scale_ref, pre_ref, w_ref, o_ref, *scratch, dilation):
    tm = x_ref.shape[0]
    sub = tm // dilation
    hn = _rms(x_ref[...], pre_ref[...]) * (1.0 + scale_ref[...]) + shift_ref[...]
    if dilation == 1:
        hb = hn.astype(BF16)
    else:
        slab_ref, = scratch
        slabs = D_MODEL // LANES
        for s in range(slabs):
            slab_ref[s] = hn[:, s * LANES:(s + 1) * LANES]
        hb = jnp.concatenate(
            [jnp.concatenate([slab_ref[s, pl.ds(r, sub, stride=dilation), :] for s in range(slabs)],
                             axis=1).astype(BF16)
             for r in range(dilation)], axis=0)
    for j in range(3):
        cols = slice(j * DIL_WIDTH, (j + 1) * DIL_WIDTH)
        y = _dot(hb, w_ref[:, cols])
        if j == 0:
            y = y * (DIL_HEAD_DIM ** -0.5 * LOG2_E)
        y = y.astype(BF16)
        for r in range(dilation):
            o_ref[r, :, cols] = y[r * sub:(r + 1) * sub, :]


def _dil_proj(x, mod, pre, w, layer, group):
    _, dilation = DIL_GROUPS[group]
    tm = TM_DIL_PROJ
    tiles_per_batch = SEQ // tm
    sub = tm // dilation
    scratch = [] if dilation == 1 else [pltpu.VMEM((D_MODEL // LANES, tm, LANES), F32)]
    return pl.pallas_call(
        functools.partial(_dil_proj_kernel, dilation=dilation),
        grid=(TOKENS // tm,),
        in_specs=[
            pl.BlockSpec((tm, D_MODEL), lambda t: (t, 0)),
            _mod_spec(layer, 1, 0, tm), _mod_spec(layer, 1, 1, tm),
            _row_spec(layer * N_SUB + 1),
            pl.BlockSpec((D_MODEL, 3 * DIL_WIDTH), lambda t: (0, group), pipeline_mode=pl.Buffered(1)),
        ],
        out_specs=pl.BlockSpec((None, dilation, sub, 3 * DIL_WIDTH),
                               lambda t: (t // tiles_per_batch, 0, t % tiles_per_batch, 0)),
        out_shape=jax.ShapeDtypeStruct((BATCH, dilation, SEQ // dilation, 3 * DIL_WIDTH), BF16),
        scratch_shapes=scratch,
        compiler_params=_params("arbitrary"),
        name=f"dil_proj_g{group}",
    )(x, mod, mod, pre, w)


DIL_HEADS_PER_STEP = 8
DIL_PAIRS_PER_STEP = DIL_HEADS_PER_STEP // 2


def _stat_lane(head):
    return head if head % 2 else DIL_HEAD_DIM + head


def _dil_attn_kernel(q_ref, k_ref, v_ref, bias_ref, o_ref, m_ref, l_ref, s_ref, *, dilation, n_blocks):
    blk = DIL_BLOCK
    lane = lax.broadcasted_iota(jnp.int32, (blk, LANES), 1)
    first_head = lane < DIL_HEAD_DIM
    head0 = pl.program_id(1) * DIL_HEADS_PER_STEP

    def where_unit(i, first):
        if first:
            return i, 0
        later = n_blocks - 1
        if dilation == 1:
            return 0, i + 1
        i = jnp.asarray(i, jnp.int32)
        return lax.div(i, jnp.int32(later)), lax.rem(i, jnp.int32(later)) + 1

    def window(ref, r, n, first, cs):
        if first:
            return ref[r, 0:blk, cs]
        return ref[r, pl.ds(pl.multiple_of(n * blk - blk, blk), 2 * blk), cs]

    def logits(i, slot, first):
        r, n = where_unit(i, first)
        kw = blk if first else 2 * blk
        q_rows = slice(0, blk) if first else pl.ds(pl.multiple_of(n * blk, blk), blk)
        for hp in range(DIL_PAIRS_PER_STEP):
            cs = slice(hp * LANES, (hp + 1) * LANES)
            q2 = q_ref[r, q_rows, cs]
            k2 = window(k_ref, r, n, first, cs)
            for hh in range(2):
                h = 2 * hp + hh
                qh = jnp.where(first_head if hh == 0 else jnp.logical_not(first_head), q2, 0)
                bias = bias_ref[h, :, blk:] if first else bias_ref[h]
                s_ref[slot, h, :, 0:kw] = _dot_nt(qh, k2) + bias

    def attend(i, slot, first):
        r, n = where_unit(i, first)
        kw = blk if first else 2 * blk
        t0 = n * (blk * dilation) + r
        rows = pl.ds(t0, blk) if dilation == 1 else pl.ds(t0, blk, stride=dilation)
        m_tile = jnp.zeros((blk, LANES), F32)
        l_tile = jnp.ones((blk, LANES), F32)
        for hp in range(DIL_PAIRS_PER_STEP):
            cs = slice(hp * LANES, (hp + 1) * LANES)
            v2 = window(v_ref, r, n, first, cs)
            accs = []
            for hh in range(2):
                s = s_ref[slot, 2 * hp + hh, :, 0:kw]
                m = jnp.max(s, axis=-1, keepdims=True)
                key_lane = lax.broadcasted_iota(jnp.int32, (kw, LANES), 1)
                own = key_lane < DIL_HEAD_DIM if hh == 0 else key_lane >= DIL_HEAD_DIM
                acc = _dot(jnp.exp2(s - m).astype(BF16), jnp.where(own, v2, 1))
                stat_lane = lane == head0 + _stat_lane(2 * hp + hh)
                m_tile = jnp.where(stat_lane, m, m_tile)
                l_tile = jnp.where(stat_lane, acc, l_tile)
                accs.append(acc)
            o_ref[hp, rows, :] = jnp.where(first_head, accs[0], accs[1])
        m_ref[rows, :] = m_tile
        l_ref[rows, :] = l_tile

    def run(count, first):
        logits(0, 0, first)

        def body(j, carry):
            i = 2 * j
            logits(i + 1, 1, first)
            attend(i, 0, first)
            logits(jnp.minimum(i + 2, count - 1), 0, first)
            attend(i + 1, 1, first)
            return carry
        if count >= 2:
            lax.fori_loop(0, count // 2, body, 0)
        if count % 2:
            attend(count - 1, 0, first)

    run(dilation, True)
    if n_blocks > 1:
        run(dilation * (n_blocks - 1), False)


def _dil_attn(proj, bias, group):
    _, dilation = DIL_GROUPS[group]
    length = SEQ // dilation
    halves = DIL_HEADS // DIL_HEADS_PER_STEP
    width = DIL_HEADS_PER_STEP * DIL_HEAD_DIM
    qkv_spec = lambda kind: pl.BlockSpec(
        (None, dilation, length, width), lambda b, u: (b, 0, 0, kind * halves + u))
    return pl.pallas_call(
        functools.partial(_dil_attn_kernel, dilation=dilation, n_blocks=length // DIL_BLOCK),
        grid=(BATCH, halves),
        in_specs=[
            qkv_spec(0), qkv_spec(1), qkv_spec(2),
            pl.BlockSpec((DIL_HEADS_PER_STEP, DIL_BLOCK, 2 * DIL_BLOCK),
                         lambda b, u: (group * halves + u, 0, 0)),
        ],
        out_specs=[
            pl.BlockSpec((None, DIL_PAIRS_PER_STEP, SEQ, LANES), lambda b, u: (b, u, 0, 0)),
            pl.BlockSpec((None, SEQ, LANES), lambda b, u: (b, 0, u)),
            pl.BlockSpec((None, SEQ, LANES), lambda b, u: (b, 0, u)),
        ],
        out_shape=[
            jax.ShapeDtypeStruct((BATCH, DIL_HEADS // 2, SEQ, LANES), F32),
            jax.ShapeDtypeStruct((BATCH, SEQ, halves * LANES), F32),
            jax.ShapeDtypeStruct((BATCH, SEQ, halves * LANES), F32),
        ],
        scratch_shapes=[pltpu.VMEM((2, DIL_HEADS_PER_STEP, DIL_BLOCK, 2 * DIL_BLOCK), F32)],
        compiler_params=_params("arbitrary", "arbitrary"),
        name=f"dil_attn_g{group}",
    )(proj, proj, proj, bias)


def _dil_out_kernel(x_ref, o0_ref, o1_ref, o2_ref, m0_ref, m1_ref, m2_ref, l0_ref, l1_ref, l2_ref,
                    e_ref, gate_ref, post_ref, w_ref, out_ref):
    lane = lax.broadcasted_iota(jnp.int32, (1, LANES), 1)
    in_first_block = functools.reduce(
        jnp.logical_or, [lane == _stat_lane(h) for h in range(DIL_HEADS_PER_STEP)])
    merge = lambda ref: jnp.where(in_first_block, ref[:, :LANES], ref[:, LANES:])
    ms = [merge(r) for r in (m0_ref, m1_ref, m2_ref)]
    ls = [merge(r) for r in (l0_ref, l1_ref, l2_ref)]
    m = jnp.maximum(jnp.maximum(ms[0], ms[1]), ms[2])
    es = [jnp.exp2(mg - m) for mg in ms]
    inv = 1.0 / (ls[0] * es[0] + ls[1] * es[1] + ls[2] * es[2])
    e = e_ref[...]
    mix = None
    for eg, o_ref in zip(es, (o0_ref, o1_ref, o2_ref)):
        alpha = eg * inv
        hi = alpha.astype(BF16)
        lo = (alpha - hi.astype(F32)).astype(BF16)
        wide = _dot(jnp.concatenate([hi, lo], axis=1), e)
        o = jnp.concatenate([o_ref[p] for p in range(DIL_HEADS // 2)], axis=1)
        mix = wide * o if mix is None else mix + wide * o
    y = _dot(mix.astype(BF16), w_ref[...])
    out_ref[...] = x_ref[...] + gate_ref[...] * _rms(y, post_ref[...])


def _dil_out(x, outs, maxes, sums, mod, post, w, layer):
    tm = TM_OUT
    tiles_per_batch = SEQ // tm
    halves = DIL_HEADS // DIL_HEADS_PER_STEP
    stat_lane_of_col = jnp.array([_stat_lane(c // DIL_HEAD_DIM) for c in range(DIL_WIDTH)], jnp.int32)
    lane_of_row = jnp.arange(2 * LANES) % LANES
    expand = (lane_of_row[:, None] == stat_lane_of_col[None, :]).astype(BF16)
    tok = lambda width: pl.BlockSpec((tm, width), lambda t: (t, 0))
    pairs = pl.BlockSpec((None, DIL_HEADS // 2, tm, LANES),
                         lambda t: (t // tiles_per_batch, 0, t % tiles_per_batch, 0))
    lse_spec = pl.BlockSpec((None, tm, halves * LANES),
                            lambda t: (t // tiles_per_batch, t % tiles_per_batch, 0))
    return pl.pallas_call(
        _dil_out_kernel,
        grid=(TOKENS // tm,),
        in_specs=[
            tok(D_MODEL), pairs, pairs, pairs, *([lse_spec] * 6),
            pl.BlockSpec(expand.shape, lambda t: (0, 0)),
            _mod_spec(layer, 1, 2, tm),
            _row_spec(layer * N_SUB + 1),
            pl.BlockSpec(w.shape, lambda t: (0, 0)),
        ],
        out_specs=tok(D_MODEL),
        out_shape=jax.ShapeDtypeStruct((TOKENS, D_MODEL), F32),
        compiler_params=_params("arbitrary"),
        name="dil_out",
    )(x, *outs, *maxes, *sums, expand, mod, post, w)


def kernel(x, c, norm_pre, norm_post, w_mod, b_mod, ffn_w_gate, ffn_w_up, ffn_w_down,
           mla_w_in, mla_q_norm, mla_w_q_up, mla_kv_norm, mla_w_kv_up, mla_w_o,
           dil_w_in, dil_w_o, rel_bias):
    assert x.shape == (BATCH, SEQ, D_MODEL) and x.dtype == F32
    h = x.reshape(TOKENS, D_MODEL)
    mod = _modulation(c, w_mod, b_mod)
    pre = norm_pre.reshape(DEPTH * N_SUB, 1, D_MODEL)
    post = norm_post.reshape(DEPTH * N_SUB, 1, D_MODEL)
    wg, wu, wd = ffn_w_gate.astype(BF16), ffn_w_up.astype(BF16), ffn_w_down.astype(BF16)
    tables = _rope_tables()
    bias = _bias_tables(rel_bias)

    for layer in range(DEPTH):
        h = _ffn(h, mod, pre, post, wg, wu, wd, layer, 0)
        idx = layer // 2
        if layer % 2 == 0:
            win, wq, wk, wv = _mla_weights(mla_w_in[idx], mla_w_q_up[idx], mla_w_kv_up[idx])
            q, k, v = _mla_proj(h, mod, pre, win, mla_q_norm[idx][None, :], mla_kv_norm[idx][None, :],
                                wq, wk, wv, tables, layer)
            a = _mla_attn(q, k, v)
            h = _out_proj(h, a, mod, post, mla_w_o[idx].astype(BF16), layer)
        else:
            w_in = dil_w_in[idx].astype(BF16)
            outs, maxes, sums = zip(*[_dil_attn(_dil_proj(h, mod, pre, w_in, layer, g), bias, g)
                                      for g in range(N_GROUPS)])
            h = _dil_out(h, outs, maxes, sums, mod, post, dil_w_o[idx].astype(BF16), layer)
        h = _ffn(h, mod, pre, post, wg, wu, wd, layer, 1)
    return h.reshape(BATCH, SEQ, D_MODEL)
```

```python
import functools
import math

import jax
import jax.numpy as jnp
from jax import lax
from jax.experimental import pallas as pl
from jax.experimental.pallas import tpu as pltpu

F32 = jnp.float32
BF16 = jnp.bfloat16

D_MODEL = 1024
BATCH = 8
SEQ = 2048
DEPTH = 2
N_SUB = 3
D_FF = 2816
FFN_RES = 0.5
EPS = 1e-6

MLA_HEADS = 16
Q_LORA = 384
KV_LORA = 256
QK_NOPE = 64
QK_ROPE = 32
V_HEAD = 64
ROPE_THETA = 10000.0

DIL_GROUPS = ((128, 1), (512, 4), (2048, 16))
N_GROUPS = 3
DIL_HEADS = 16
DIL_HEAD_DIM = 64
DIL_BLOCK = 128
DIL_WIDTH = DIL_HEADS * DIL_HEAD_DIM
N_BUCKETS = 32
MAX_DISTANCE = 2048

TOKENS = BATCH * SEQ
LANES = 128
HEAD_SLOT = 128
ROPE_HALF = QK_ROPE // 2
MASK_VALUE = -1e30
LOG2_E = math.log2(math.e)
VMEM_LIMIT = 56 * 1024 * 1024

TM_FFN = 1024
FFN_SUB = 512
TM_PROJ = 1024
PROJ_SUB = 256
TM_DIL_PROJ = 1024
TN_DIL_PROJ = 1024
TM_OUT = 512
TQ_MLA = 256
TN_MOD = 1536
FF_CHUNK = 1536


def _params(*sem):
    return pltpu.CompilerParams(dimension_semantics=sem, vmem_limit_bytes=VMEM_LIMIT)


def _rms(x, g):
    ms = jnp.mean(x * x, axis=-1, keepdims=True)
    return x * lax.rsqrt(ms + EPS) * g


def _dot(a, b):
    return jnp.dot(a, b, preferred_element_type=F32)


def _dot_nt(a, b):
    return lax.dot_general(a, b, (((1,), (1,)), ((), ())), preferred_element_type=F32)


def _silu(x):
    return x * jax.nn.sigmoid(x)


def _mod_spec(layer, sub, kind, tm):
    tiles_per_batch = SEQ // tm
    return pl.BlockSpec(
        (None, 1, D_MODEL),
        lambda t, *_: ((layer * BATCH + t // tiles_per_batch) * (N_SUB * 3) + sub * 3 + kind, 0, 0))


def _row_spec(index):
    return pl.BlockSpec((None, 1, D_MODEL), lambda *_: (index, 0, 0))


def _mod_kernel(c_ref, w_ref, b_ref, o_ref):
    h = _silu(c_ref[...])
    o_ref[...] = jnp.dot(h, w_ref[...], preferred_element_type=F32,
                         precision=lax.Precision.HIGHEST) + b_ref[...]


def _modulation(c, w_mod, b_mod):
    n = N_SUB * 3 * D_MODEL
    out = pl.pallas_call(
        _mod_kernel,
        grid=(DEPTH, n // TN_MOD),
        in_specs=[
            pl.BlockSpec((BATCH, D_MODEL), lambda i, j: (0, 0)),
            pl.BlockSpec((None, D_MODEL, TN_MOD), lambda i, j: (i, 0, j)),
            pl.BlockSpec((None, 1, TN_MOD), lambda i, j: (i, 0, j)),
        ],
        out_specs=pl.BlockSpec((None, BATCH, TN_MOD), lambda i, j: (i, 0, j)),
        out_shape=jax.ShapeDtypeStruct((DEPTH, BATCH, n), F32),
        compiler_params=_params("arbitrary", "arbitrary"),
        name="modulation",
    )(c, w_mod, b_mod.reshape(DEPTH, 1, n))
    return out.reshape(DEPTH * BATCH * N_SUB * 3, 1, D_MODEL)


def _mla_mix(rows, att_ref, wo_ref):
    return _dot(att_ref[rows, :], wo_ref[...])


def _dil_mix(rows, o0_ref, o1_ref, o2_ref, m0_ref, m1_ref, m2_ref, l0_ref, l1_ref, l2_ref, e_ref, wo_ref):
    lane = lax.broadcasted_iota(jnp.int32, (1, LANES), 1)
    in_first_block = functools.reduce(
        jnp.logical_or, [lane == _stat_lane(h) for h in range(DIL_HEADS_PER_STEP)])
    merge = lambda ref: jnp.where(in_first_block, ref[rows, :LANES], ref[rows, LANES:])
    ms = [merge(r) for r in (m0_ref, m1_ref, m2_ref)]
    ls = [merge(r) for r in (l0_ref, l1_ref, l2_ref)]
    m = jnp.maximum(jnp.maximum(ms[0], ms[1]), ms[2])
    es = [jnp.exp2(mg - m) for mg in ms]
    inv = 1.0 / (ls[0] * es[0] + ls[1] * es[1] + ls[2] * es[2])
    e = e_ref[...]
    mix = None
    for eg, o_ref in zip(es, (o0_ref, o1_ref, o2_ref)):
        alpha = eg * inv
        hi = alpha.astype(BF16)
        lo = (alpha - hi.astype(F32)).astype(BF16)
        wide = _dot(jnp.concatenate([hi, lo], axis=1), e)
        o = jnp.concatenate([o_ref[p, rows, :] for p in range(DIL_HEADS // 2)], axis=1)
        mix = wide * o if mix is None else mix + wide * o
    return _dot(mix.astype(BF16), wo_ref[...])


def _ffn_kernel(x_ref, *refs, after_mla):
    if after_mla:
        att_ref, wo_ref, mgate_ref, mpost_ref, *refs = refs
    shift_ref, scale_ref, gate_ref, pre_ref, post_ref, wg_ref, wu_ref, wd_ref, o_ref, a_ref = refs
    tm = x_ref.shape[0]
    subs = [slice(r0, r0 + FFN_SUB) for r0 in range(0, tm, FFN_SUB)]
    hns = []
    for rows in subs:
        x = x_ref[rows, :]
        if after_mla:
            x = x + mgate_ref[...] * _rms(_mla_mix(rows, att_ref, wo_ref), mpost_ref[...])
            o_ref[rows, :] = x
        hns.append((_rms(x, pre_ref[...]) * (1.0 + scale_ref[...]) + shift_ref[...]).astype(BF16))
    for rows, hn in zip(subs, hns):
        for c0 in range(0, D_FF, FF_CHUNK):
            c1 = min(c0 + FF_CHUNK, D_FF)
            g = _dot(hn, wg_ref[:, c0:c1])
            u = _dot(hn, wu_ref[:, c0:c1])
            a_ref[rows, c0:c1] = (_silu(g) * u).astype(BF16)
    for rows in subs:
        y = _dot(a_ref[rows, :], wd_ref[...])
        x = o_ref[rows, :] if after_mla else x_ref[rows, :]
        o_ref[rows, :] = x + FFN_RES * gate_ref[...] * _rms(y, post_ref[...])


def _ffn(x, mod, pre, post, wg, wu, wd, layer, which, mla=None):
    sub = 0 if which == 0 else 2
    tm = TM_FFN
    w_in_spec = pl.BlockSpec((None, None, D_MODEL, D_FF), lambda t: (layer, which, 0, 0),
                             pipeline_mode=pl.Buffered(1))
    w_out_spec = pl.BlockSpec((None, None, D_FF, D_MODEL), lambda t: (layer, which, 0, 0),
                              pipeline_mode=pl.Buffered(1))
    mla_args, mla_specs = [], []
    if mla is not None:
        att, w_o = mla
        mla_args = [att, w_o, mod, post]
        mla_specs = [pl.BlockSpec((tm, att.shape[1]), lambda t: (t, 0)),
                     pl.BlockSpec(w_o.shape, lambda t: (0, 0)),
                     _mod_spec(layer, 1, 2, tm), _row_spec(layer * N_SUB + 1)]
    return pl.pallas_call(
        functools.partial(_ffn_kernel, after_mla=mla is not None),
        grid=(TOKENS // tm,),
        in_specs=[
            pl.BlockSpec((tm, D_MODEL), lambda t: (t, 0)),
            *mla_specs,
            _mod_spec(layer, sub, 0, tm), _mod_spec(layer, sub, 1, tm), _mod_spec(layer, sub, 2, tm),
            _row_spec(layer * N_SUB + sub), _row_spec(layer * N_SUB + sub),
            w_in_spec, w_in_spec, w_out_spec,
        ],
        out_specs=pl.BlockSpec((tm, D_MODEL), lambda t: (t, 0)),
        out_shape=jax.ShapeDtypeStruct((TOKENS, D_MODEL), F32),
        scratch_shapes=[pltpu.VMEM((tm, D_FF), BF16)],
        compiler_params=_params("arbitrary"),
        name="ffn" if mla is None else "mla_out_ffn",
    )(x, *mla_args, mod, mod, mod, pre, post, wg, wu, wd)


def _rope(z, ta, tb):
    return z * ta + pltpu.roll(z, HEAD_SLOT // 2, 1) * tb


def _mla_proj_kernel(x_ref, shift_ref, scale_ref, pre_ref, win_ref, qn_ref, kvn_ref,
                     wq_ref, wk_ref, wv_ref, ta_ref, tb_ref, q_ref, k_ref, v_ref):
    tm = x_ref.shape[0]
    subs = [slice(r0, r0 + PROJ_SUB) for r0 in range(0, tm, PROJ_SUB)]
    lats = []
    for rows in subs:
        hn = (_rms(x_ref[rows, :], pre_ref[...]) * (1.0 + scale_ref[...]) + shift_ref[...]).astype(BF16)
        lats.append(_dot(hn, win_ref[...]))
    for rows, lat in zip(subs, lats):
        cq = _rms(lat[:, :Q_LORA], qn_ref[...]).astype(BF16)
        ckv = _rms(lat[:, Q_LORA:Q_LORA + KV_LORA], kvn_ref[...]).astype(BF16)
        ta, tb = ta_ref[rows, :], tb_ref[rows, :]
        kr = _rope(lat[:, Q_LORA + KV_LORA:], ta, tb)
        q = _dot(cq, wq_ref[...]) * ((QK_NOPE + QK_ROPE) ** -0.5)
        kn = _dot(ckv, wk_ref[...])
        v_ref[rows, :] = _dot(ckv, wv_ref[...]).astype(BF16)
        for h in range(MLA_HEADS):
            sl = slice(h * HEAD_SLOT, (h + 1) * HEAD_SLOT)
            q_ref[rows, sl] = _rope(q[:, sl], ta, tb).astype(BF16)
            k_ref[rows, sl] = (kn[:, sl] + kr).astype(BF16)


def _head_slot(nope, rope, like):
    split = HEAD_SLOT // 2 - ROPE_HALF
    zeros = lambda n: jnp.zeros(like.shape[:-1] + (n,), like.dtype)
    nope = zeros(QK_NOPE) if nope is None else nope
    rope = zeros(QK_ROPE) if rope is None else rope
    return jnp.concatenate([rope[..., :ROPE_HALF], nope[..., :split], rope[..., ROPE_HALF:], nope[..., split:],
                            zeros(HEAD_SLOT - QK_NOPE - QK_ROPE)], axis=-1)


def _rope_tables():
    pos = jnp.arange(SEQ, dtype=F32)
    freqs = ROPE_THETA ** (-jnp.arange(ROPE_HALF, dtype=F32) / ROPE_HALF)
    ang = pos[:, None] * freqs[None, :]
    cos, sin = jnp.cos(ang), jnp.sin(ang)
    ones = jnp.ones((SEQ, QK_NOPE), F32)
    ta = _head_slot(ones, jnp.concatenate([cos, cos], axis=1), cos)
    tb = _head_slot(None, jnp.concatenate([-sin, sin], axis=1), sin)
    return ta, tb


def _mla_weights(w_in, w_q_up, w_kv_up):
    win = jnp.concatenate([w_in[:, :Q_LORA + KV_LORA],
                           _head_slot(None, w_in[:, Q_LORA + KV_LORA:], w_in)], axis=1).astype(BF16)
    wq = w_q_up.reshape(Q_LORA, MLA_HEADS, QK_NOPE + QK_ROPE)
    wq = _head_slot(wq[..., :QK_NOPE], wq[..., QK_NOPE:], wq)
    wq = wq.reshape(Q_LORA, MLA_HEADS * HEAD_SLOT).astype(BF16)
    wkv = w_kv_up.reshape(KV_LORA, MLA_HEADS, QK_NOPE + V_HEAD)
    wk = _head_slot(wkv[..., :QK_NOPE], None, wkv).reshape(KV_LORA, MLA_HEADS * HEAD_SLOT).astype(BF16)
    wv = wkv[..., QK_NOPE:].reshape(KV_LORA, MLA_HEADS * V_HEAD).astype(BF16)
    return win, wq, wk, wv


def _mla_proj(x, mod, pre, win, qn, kvn, wq, wk, wv, tables, layer):
    tm = TM_PROJ
    tiles_per_batch = SEQ // tm
    full = lambda a: pl.BlockSpec(a.shape, lambda t: (0,) * a.ndim)
    tab_spec = pl.BlockSpec((tm, HEAD_SLOT), lambda t: (t % tiles_per_batch, 0))
    wide = MLA_HEADS * HEAD_SLOT
    return pl.pallas_call(
        _mla_proj_kernel,
        grid=(TOKENS // tm,),
        in_specs=[
            pl.BlockSpec((tm, D_MODEL), lambda t: (t, 0)),
            _mod_spec(layer, 1, 0, tm), _mod_spec(layer, 1, 1, tm),
            _row_spec(layer * N_SUB + 1),
            full(win), full(qn), full(kvn), full(wq), full(wk), full(wv),
            tab_spec, tab_spec,
        ],
        out_specs=[
            pl.BlockSpec((tm, wide), lambda t: (t, 0)),
            pl.BlockSpec((tm, wide), lambda t: (t, 0)),
            pl.BlockSpec((tm, MLA_HEADS * V_HEAD), lambda t: (t, 0)),
        ],
        out_shape=[
            jax.ShapeDtypeStruct((TOKENS, wide), BF16),
            jax.ShapeDtypeStruct((TOKENS, wide), BF16),
            jax.ShapeDtypeStruct((TOKENS, MLA_HEADS * V_HEAD), BF16),
        ],
        compiler_params=_params("arbitrary"),
        name="mla_proj",
    )(x, mod, mod, pre, win, qn, kvn, wq, wk, wv, *tables)


def _mla_attn_kernel(q_ref, k_ref, v_ref, o_ref, vaug_ref):
    tq = TQ_MLA
    row = lax.broadcasted_iota(jnp.int32, (tq, tq), 0)
    col = lax.broadcasted_iota(jnp.int32, (tq, tq), 1)
    causal = col <= row
    first_head = lax.broadcasted_iota(jnp.int32, (tq, 2 * V_HEAD), 1) < V_HEAD
    vaug_ref[:, :LANES] = v_ref[...]
    vaug_ref[:, LANES:] = jnp.ones((SEQ, LANES), BF16)

    def logits(qi, hh):
        r0 = qi * tq
        hs = slice(hh * HEAD_SLOT, (hh + 1) * HEAD_SLOT)
        q = q_ref[r0:r0 + tq, hs]
        s_d = jnp.where(causal, _dot_nt(q, k_ref[r0:r0 + tq, hs]), MASK_VALUE)
        s_o = _dot_nt(q, k_ref[0:r0, hs]) if qi > 0 else None
        return s_d, s_o

    def attend(qi, s_d, s_o):
        r0 = qi * tq
        m = jnp.max(s_d, axis=-1, keepdims=True)
        if s_o is not None:
            m = jnp.maximum(m, jnp.max(s_o, axis=-1, keepdims=True))
        acc = _dot(jnp.exp(s_d - m).astype(BF16), vaug_ref[r0:r0 + tq, :])
        if s_o is not None:
            acc = acc + _dot(jnp.exp(s_o - m).astype(BF16), vaug_ref[0:r0, :])
        return acc[:, :LANES] * (1.0 / acc[:, LANES:])

    units = [(qi, hh) for qi in range(SEQ // tq) for hh in range(2)]
    pending = logits(*units[0])
    for i, (qi, hh) in enumerate(units):
        ahead = logits(*units[i + 1]) if i + 1 < len(units) else None
        o = attend(qi, *pending)
        if hh == 0:
            o_first = o
        else:
            o_ref[qi * tq:(qi + 1) * tq, :] = jnp.where(first_head, o_first, o).astype(BF16)
        pending = ahead


def _mla_attn(q, k, v):
    pairs = MLA_HEADS // 2
    return pl.pallas_call(
        _mla_attn_kernel,
        grid=(BATCH, pairs),
        in_specs=[
            pl.BlockSpec((SEQ, 2 * HEAD_SLOT), lambda b, p: (b, p)),
            pl.BlockSpec((SEQ, 2 * HEAD_SLOT), lambda b, p: (b, p)),
            pl.BlockSpec((SEQ, 2 * V_HEAD), lambda b, p: (b, p)),
        ],
        out_specs=pl.BlockSpec((SEQ, 2 * V_HEAD), lambda b, p: (b, p)),
        out_shape=jax.ShapeDtypeStruct((TOKENS, MLA_HEADS * V_HEAD), BF16),
        scratch_shapes=[pltpu.VMEM((SEQ, 2 * LANES), BF16)],
        compiler_params=_params("arbitrary", "arbitrary"),
        name="mla_attn",
    )(q, k, v)


def _t5_bucket(dist):
    max_exact = N_BUCKETS // 2
    d = jnp.maximum(dist, 1).astype(F32)
    large = max_exact + (jnp.log(d / max_exact) / math.log(MAX_DISTANCE / max_exact)
                         * (N_BUCKETS - max_exact)).astype(jnp.int32)
    large = jnp.minimum(large, N_BUCKETS - 1)
    return jnp.where(dist < max_exact, dist, large)


def _bias_selectors():
    j = jnp.arange(2 * DIL_BLOCK)
    rel = DIL_BLOCK - j
    sel = []
    for window, dilation in DIL_GROUPS:
        assert window // dilation == DIL_BLOCK
        bucket = _t5_bucket(jnp.maximum(rel, 0) * dilation)
        onehot = (bucket[None, :] == jnp.arange(N_BUCKETS)[:, None]) & (rel >= 0)[None, :]
        sel.append(onehot.astype(F32))
    return jnp.stack(sel)


def _bias_kernel(rbt_ref, sel_ref, o_ref):
    rbt = rbt_ref[...]
    sel = sel_ref[...]
    row0 = jnp.zeros((DIL_HEADS, 2 * DIL_BLOCK), F32)
    for b in range(N_BUCKETS):
        row0 = row0 + rbt[:, b:b + 1] * sel[b:b + 1, :]
    future = jnp.sum(sel, axis=0, keepdims=True) < 0.5
    row0 = jnp.where(future, MASK_VALUE, row0 * LOG2_E)
    sub = lax.broadcasted_iota(jnp.int32, (DIL_BLOCK, 2 * DIL_BLOCK), 0)
    for h in range(DIL_HEADS):
        t = jnp.broadcast_to(row0[h:h + 1, :], (DIL_BLOCK, 2 * DIL_BLOCK))
        for bit in range(DIL_BLOCK.bit_length() - 1):
            t = jnp.where(((sub >> bit) & 1) == 1, pltpu.roll(t, 1 << bit, 1), t)
        o_ref[h] = t


def _bias_tables(rel_bias):
    rbt = rel_bias.T.reshape(N_GROUPS, DIL_HEADS, N_BUCKETS)
    return pl.pallas_call(
        _bias_kernel,
        grid=(N_GROUPS,),
        in_specs=[
            pl.BlockSpec((None, DIL_HEADS, N_BUCKETS), lambda g: (g, 0, 0)),
            pl.BlockSpec((None, N_BUCKETS, 2 * DIL_BLOCK), lambda g: (g, 0, 0)),
        ],
        out_specs=pl.BlockSpec((DIL_HEADS, DIL_BLOCK, 2 * DIL_BLOCK), lambda g: (g, 0, 0)),
        out_shape=jax.ShapeDtypeStruct((N_GROUPS * DIL_HEADS, DIL_BLOCK, 2 * DIL_BLOCK), F32),
        compiler_params=_params("arbitrary"),
        name="dil_bias",
    )(rbt, _bias_selectors())


def _dil_proj_kernel(x_ref, shift_ref, scale_ref, pre_ref, w_ref, o_ref, *scratch, dilation):
    tm = x_ref.shape[0]
    sub = tm // dilation
    hn = _rms(x_ref[...], pre_ref[...]) * (1.0 + scale_ref[...]) + shift_ref[...]
    if dilation == 1:
        hb = hn.astype(BF16)
    else:
        slab_ref, = scratch
        slabs = D_MODEL // LANES
        for s in range(slabs):
            slab_ref[s] = hn[:, s * LANES:(s + 1) * LANES]
        hb = jnp.concatenate(
            [jnp.concatenate([slab_ref[s, pl.ds(r, sub, stride=dilation), :] for s in range(slabs)],
                             axis=1).astype(BF16)
             for r in range(dilation)], axis=0)
    for j in range(3):
        cols = slice(j * DIL_WIDTH, (j + 1) * DIL_WIDTH)
        y = _dot(hb, w_ref[:, cols])
        if j == 0:
            y = y * (DIL_HEAD_DIM ** -0.5 * LOG2_E)
        y = y.astype(BF16)
        for r in range(dilation):
            o_ref[r, :, cols] = y[r * sub:(r + 1) * sub, :]


def _dil_proj(x, mod, pre, w, layer, group):
    _, dilation = DIL_GROUPS[group]
    tm = TM_DIL_PROJ
    tiles_per_batch = SEQ // tm
    sub = tm // dilation
    scratch = [] if dilation == 1 else [pltpu.VMEM((D_MODEL // LANES, tm, LANES), F32)]
    return pl.pallas_call(
        functools.partial(_dil_proj_kernel, dilation=dilation),
        grid=(TOKENS // tm,),
        in_specs=[
            pl.BlockSpec((tm, D_MODEL), lambda t: (t, 0)),
            _mod_spec(layer, 1, 0, tm), _mod_spec(layer, 1, 1, tm),
            _row_spec(layer * N_SUB + 1),
            pl.BlockSpec((D_MODEL, 3 * DIL_WIDTH), lambda t: (0, group), pipeline_mode=pl.Buffered(1)),
        ],
        out_specs=pl.BlockSpec((None, dilation, sub, 3 * DIL_WIDTH),
                               lambda t: (t // tiles_per_batch, 0, t % tiles_per_batch, 0)),
        out_shape=jax.ShapeDtypeStruct((BATCH, dilation, SEQ // dilation, 3 * DIL_WIDTH), BF16),
        scratch_shapes=scratch,
        compiler_params=_params("arbitrary"),
        name=f"dil_proj_g{group}",
    )(x, mod, mod, pre, w)


DIL_HEADS_PER_STEP = 8
DIL_PAIRS_PER_STEP = DIL_HEADS_PER_STEP // 2


def _stat_lane(head):
    return head if head % 2 else DIL_HEAD_DIM + head


def _dil_attn_kernel(q_ref, k_ref, v_ref, bias_ref, o_ref, m_ref, l_ref, s_ref, *, dilation, n_blocks):
    blk = DIL_BLOCK
    lane = lax.broadcasted_iota(jnp.int32, (blk, LANES), 1)
    first_head = lane < DIL_HEAD_DIM
    head0 = pl.program_id(1) * DIL_HEADS_PER_STEP

    def where_unit(i, first):
        if first:
            return i, 0
        later = n_blocks - 1
        if dilation == 1:
            return 0, i + 1
        i = jnp.asarray(i, jnp.int32)
        return lax.div(i, jnp.int32(later)), lax.rem(i, jnp.int32(later)) + 1

    def window(ref, r, n, first, cs):
        if first:
            return ref[r, 0:blk, cs]
        return ref[r, pl.ds(pl.multiple_of(n * blk - blk, blk), 2 * blk), cs]

    def logits(i, slot, first):
        r, n = where_unit(i, first)
        kw = blk if first else 2 * blk
        q_rows = slice(0, blk) if first else pl.ds(pl.multiple_of(n * blk, blk), blk)
        for hp in range(DIL_PAIRS_PER_STEP):
            cs = slice(hp * LANES, (hp + 1) * LANES)
            q2 = q_ref[r, q_rows, cs]
            k2 = window(k_ref, r, n, first, cs)
            for hh in range(2):
                h = 2 * hp + hh
                qh = jnp.where(first_head if hh == 0 else jnp.logical_not(first_head), q2, 0)
                bias = bias_ref[h, :, blk:] if first else bias_ref[h]
                s_ref[slot, h, :, 0:kw] = _dot_nt(qh, k2) + bias

    def attend(i, slot, first):
        r, n = where_unit(i, first)
        kw = blk if first else 2 * blk
        t0 = n * (blk * dilation) + r
        rows = pl.ds(t0, blk) if dilation == 1 else pl.ds(t0, blk, stride=dilation)
        m_tile = jnp.zeros((blk, LANES), F32)
        l_tile = jnp.ones((blk, LANES), F32)
        for hp in range(DIL_PAIRS_PER_STEP):
            cs = slice(hp * LANES, (hp + 1) * LANES)
            v2 = window(v_ref, r, n, first, cs)
            accs = []
            for hh in range(2):
                s = s_ref[slot, 2 * hp + hh, :, 0:kw]
                m = jnp.max(s, axis=-1, keepdims=True)
                key_lane = lax.broadcasted_iota(jnp.int32, (kw, LANES), 1)
                own = key_lane < DIL_HEAD_DIM if hh == 0 else key_lane >= DIL_HEAD_DIM
                acc = _dot(jnp.exp2(s - m).astype(BF16), jnp.where(own, v2, 1))
                stat_lane = lane == head0 + _stat_lane(2 * hp + hh)
                m_tile = jnp.where(stat_lane, m, m_tile)
                l_tile = jnp.where(stat_lane, acc, l_tile)
                accs.append(acc)
            o_ref[hp, rows, :] = jnp.where(first_head, accs[0], accs[1])
        m_ref[rows, :] = m_tile
        l_ref[rows, :] = l_tile

    def run(count, first):
        logits(0, 0, first)

        def body(j, carry):
            i = 2 * j
            logits(i + 1, 1, first)
            attend(i, 0, first)
            logits(jnp.minimum(i + 2, count - 1), 0, first)
            attend(i + 1, 1, first)
            return carry
        if count >= 2:
            lax.fori_loop(0, count // 2, body, 0)
        if count % 2:
            attend(count - 1, 0, first)

    run(dilation, True)
    if n_blocks > 1:
        run(dilation * (n_blocks - 1), False)


def _dil_attn(proj, bias, group):
    _, dilation = DIL_GROUPS[group]
    length = SEQ // dilation
    halves = DIL_HEADS // DIL_HEADS_PER_STEP
    width = DIL_HEADS_PER_STEP * DIL_HEAD_DIM
    qkv_spec = lambda kind: pl.BlockSpec(
        (None, dilation, length, width), lambda b, u: (b, 0, 0, kind * halves + u))
    return pl.pallas_call(
        functools.partial(_dil_attn_kernel, dilation=dilation, n_blocks=length // DIL_BLOCK),
        grid=(BATCH, halves),
        in_specs=[
            qkv_spec(0), qkv_spec(1), qkv_spec(2),
            pl.BlockSpec((DIL_HEADS_PER_STEP, DIL_BLOCK, 2 * DIL_BLOCK),
                         lambda b, u: (group * halves + u, 0, 0)),
        ],
        out_specs=[
            pl.BlockSpec((None, DIL_PAIRS_PER_STEP, SEQ, LANES), lambda b, u: (b, u, 0, 0)),
            pl.BlockSpec((None, SEQ, LANES), lambda b, u: (b, 0, u)),
            pl.BlockSpec((None, SEQ, LANES), lambda b, u: (b, 0, u)),
        ],
        out_shape=[
            jax.ShapeDtypeStruct((BATCH, DIL_HEADS // 2, SEQ, LANES), F32),
            jax.ShapeDtypeStruct((BATCH, SEQ, halves * LANES), F32),
            jax.ShapeDtypeStruct((BATCH, SEQ, halves * LANES), F32),
        ],
        scratch_shapes=[pltpu.VMEM((2, DIL_HEADS_PER_STEP, DIL_BLOCK, 2 * DIL_BLOCK), F32)],
        compiler_params=_params("arbitrary", "arbitrary"),
        name=f"dil_attn_g{group}",
    )(proj, proj, proj, bias)


def _dil_out_kernel(x_ref, *refs):
    *mix_refs, gate_ref, post_ref, out_ref = refs
    y = _dil_mix(slice(0, x_ref.shape[0]), *mix_refs)
    out_ref[...] = x_ref[...] + gate_ref[...] * _rms(y, post_ref[...])


def _dil_out(x, outs, maxes, sums, mod, post, w_o, layer):
    tm = TM_OUT
    tiles_per_batch = SEQ // tm
    halves = DIL_HEADS // DIL_HEADS_PER_STEP
    stat_lane_of_col = jnp.array([_stat_lane(c // DIL_HEAD_DIM) for c in range(DIL_WIDTH)], jnp.int32)
    lane_of_row = jnp.arange(2 * LANES) % LANES
    expand = (lane_of_row[:, None] == stat_lane_of_col[None, :]).astype(BF16)
    pairs = pl.BlockSpec((None, DIL_HEADS // 2, tm, LANES),
                         lambda t: (t // tiles_per_batch, 0, t % tiles_per_batch, 0))
    stats = pl.BlockSpec((None, tm, halves * LANES),
                         lambda t: (t // tiles_per_batch, t % tiles_per_batch, 0))
    tok = pl.BlockSpec((tm, D_MODEL), lambda t: (t, 0))
    return pl.pallas_call(
        _dil_out_kernel,
        grid=(TOKENS // tm,),
        in_specs=[
            tok, pairs, pairs, pairs, *([stats] * 6),
            pl.BlockSpec(expand.shape, lambda t: (0, 0)), pl.BlockSpec(w_o.shape, lambda t: (0, 0)),
            _mod_spec(layer, 1, 2, tm), _row_spec(layer * N_SUB + 1),
        ],
        out_specs=tok,
        out_shape=jax.ShapeDtypeStruct((TOKENS, D_MODEL), F32),
        compiler_params=_params("arbitrary"),
        name="dil_out",
    )(x, *outs, *maxes, *sums, expand, w_o, mod, post)


def kernel(x, c, norm_pre, norm_post, w_mod, b_mod, ffn_w_gate, ffn_w_up, ffn_w_down,
           mla_w_in, mla_q_norm, mla_w_q_up, mla_kv_norm, mla_w_kv_up, mla_w_o,
           dil_w_in, dil_w_o, rel_bias):
    assert x.shape == (BATCH, SEQ, D_MODEL) and x.dtype == F32
    h = x.reshape(TOKENS, D_MODEL)
    mod = _modulation(c, w_mod, b_mod)
    pre = norm_pre.reshape(DEPTH * N_SUB, 1, D_MODEL)
    post = norm_post.reshape(DEPTH * N_SUB, 1, D_MODEL)
    wg, wu, wd = ffn_w_gate.astype(BF16), ffn_w_up.astype(BF16), ffn_w_down.astype(BF16)
    tables = _rope_tables()
    bias = _bias_tables(rel_bias)

    for layer in range(DEPTH):
        h = _ffn(h, mod, pre, post, wg, wu, wd, layer, 0)
        idx = layer // 2
        if layer % 2 == 0:
            win, wq, wk, wv = _mla_weights(mla_w_in[idx], mla_w_q_up[idx], mla_w_kv_up[idx])
            q, k, v = _mla_proj(h, mod, pre, win, mla_q_norm[idx][None, :], mla_kv_norm[idx][None, :],
                                wq, wk, wv, tables, layer)
            mla = (_mla_attn(q, k, v), mla_w_o[idx].astype(BF16))
        else:
            w_in = dil_w_in[idx].astype(BF16)
            outs, maxes, sums = zip(*[_dil_attn(_dil_proj(h, mod, pre, w_in, layer, g), bias, g)
                                      for g in range(N_GROUPS)])
            h = _dil_out(h, outs, maxes, sums, mod, post, dil_w_o[idx].astype(BF16), layer)
            mla = None
        h = _ffn(h, mod, pre, post, wg, wu, wd, layer, 1, mla=mla)
    return h.reshape(BATCH, SEQ, D_MODEL)
```

```python
import functools
import math

import jax
import jax.numpy as jnp
from jax import lax
from jax.experimental import pallas as pl
from jax.experimental.pallas import tpu as pltpu

F32 = jnp.float32
BF16 = jnp.bfloat16

D_MODEL = 1024
BATCH = 8
SEQ = 2048
DEPTH = 2
N_SUB = 3
D_FF = 2816
FFN_RES = 0.5
EPS = 1e-6

MLA_HEADS = 16
Q_LORA = 384
KV_LORA = 256
QK_NOPE = 64
QK_ROPE = 32
V_HEAD = 64
ROPE_THETA = 10000.0

DIL_GROUPS = ((128, 1), (512, 4), (2048, 16))
N_GROUPS = 3
DIL_HEADS = 16
DIL_HEAD_DIM = 64
DIL_BLOCK = 128
DIL_WIDTH = DIL_HEADS * DIL_HEAD_DIM
N_BUCKETS = 32
MAX_DISTANCE = 2048

TOKENS = BATCH * SEQ
LANES = 128
HEAD_SLOT = 128
ROPE_HALF = QK_ROPE // 2
MASK_VALUE = -1e30
LOG2_E = math.log2(math.e)
VMEM_LIMIT = 56 * 1024 * 1024

TM_FFN = 1024
FFN_SUB = 512
TM_PROJ = 1024
PROJ_SUB = 256
TM_DIL_PROJ = 1024
TN_DIL_PROJ = 1024
TM_OUT = 512
TQ_MLA = 256
TN_MOD = 1536
FF_CHUNK = 1536


def _params(*sem):
    return pltpu.CompilerParams(dimension_semantics=sem, vmem_limit_bytes=VMEM_LIMIT)


def _rms(x, g):
    ms = jnp.mean(x * x, axis=-1, keepdims=True)
    return x * lax.rsqrt(ms + EPS) * g


def _dot(a, b):
    return jnp.dot(a, b, preferred_element_type=F32)


def _dot_nt(a, b):
    return lax.dot_general(a, b, (((1,), (1,)), ((), ())), preferred_element_type=F32)


def _silu(x):
    return x * jax.nn.sigmoid(x)


def _mod_spec(layer, sub, kind, tm):
    tiles_per_batch = SEQ // tm
    return pl.BlockSpec(
        (None, 1, D_MODEL),
        lambda t, *_: ((layer * BATCH + t // tiles_per_batch) * (N_SUB * 3) + sub * 3 + kind, 0, 0))


def _row_spec(index):
    return pl.BlockSpec((None, 1, D_MODEL), lambda *_: (index, 0, 0))


def _mod_kernel(c_ref, w_ref, b_ref, o_ref):
    h = _silu(c_ref[...])
    o_ref[...] = jnp.dot(h, w_ref[...], preferred_element_type=F32,
                         precision=lax.Precision.HIGHEST) + b_ref[...]


def _modulation(c, w_mod, b_mod):
    n = N_SUB * 3 * D_MODEL
    out = pl.pallas_call(
        _mod_kernel,
        grid=(DEPTH, n // TN_MOD),
        in_specs=[
            pl.BlockSpec((BATCH, D_MODEL), lambda i, j: (0, 0)),
            pl.BlockSpec((None, D_MODEL, TN_MOD), lambda i, j: (i, 0, j)),
            pl.BlockSpec((None, 1, TN_MOD), lambda i, j: (i, 0, j)),
        ],
        out_specs=pl.BlockSpec((None, BATCH, TN_MOD), lambda i, j: (i, 0, j)),
        out_shape=jax.ShapeDtypeStruct((DEPTH, BATCH, n), F32),
        compiler_params=_params("arbitrary", "arbitrary"),
        name="modulation",
    )(c, w_mod, b_mod.reshape(DEPTH, 1, n))
    return out.reshape(DEPTH * BATCH * N_SUB * 3, 1, D_MODEL)


def _mla_mix(rows, att_ref, wo_ref):
    return _dot(att_ref[rows, :], wo_ref[...])


def _ffn_kernel(x_ref, *refs, after_mla):
    if after_mla:
        att_ref, wo_ref, mgate_ref, mpost_ref, *refs = refs
    shift_ref, scale_ref, gate_ref, pre_ref, post_ref, wg_ref, wu_ref, wd_ref, o_ref, a_ref = refs
    tm = x_ref.shape[0]
    subs = [slice(r0, r0 + FFN_SUB) for r0 in range(0, tm, FFN_SUB)]
    hns = []
    for rows in subs:
        x = x_ref[rows, :]
        if after_mla:
            x = x + mgate_ref[...] * _rms(_mla_mix(rows, att_ref, wo_ref), mpost_ref[...])
            o_ref[rows, :] = x
        hns.append((_rms(x, pre_ref[...]) * (1.0 + scale_ref[...]) + shift_ref[...]).astype(BF16))
    for rows, hn in zip(subs, hns):
        for c0 in range(0, D_FF, FF_CHUNK):
            c1 = min(c0 + FF_CHUNK, D_FF)
            g = _dot(hn, wg_ref[:, c0:c1])
            u = _dot(hn, wu_ref[:, c0:c1])
            a_ref[rows, c0:c1] = (_silu(g) * u).astype(BF16)
    for rows in subs:
        y = _dot(a_ref[rows, :], wd_ref[...])
        x = o_ref[rows, :] if after_mla else x_ref[rows, :]
        o_ref[rows, :] = x + FFN_RES * gate_ref[...] * _rms(y, post_ref[...])


def _ffn(x, mod, pre, post, wg, wu, wd, layer, which, mla=None):
    sub = 0 if which == 0 else 2
    tm = TM_FFN
    w_in_spec = pl.BlockSpec((None, None, D_MODEL, D_FF), lambda t: (layer, which, 0, 0),
                             pipeline_mode=pl.Buffered(1))
    w_out_spec = pl.BlockSpec((None, None, D_FF, D_MODEL), lambda t: (layer, which, 0, 0),
                              pipeline_mode=pl.Buffered(1))
    mla_args, mla_specs = [], []
    if mla is not None:
        att, w_o = mla
        mla_args = [att, w_o, mod, post]
        mla_specs = [pl.BlockSpec((tm, att.shape[1]), lambda t: (t, 0)),
                     pl.BlockSpec(w_o.shape, lambda t: (0, 0)),
                     _mod_spec(layer, 1, 2, tm), _row_spec(layer * N_SUB + 1)]
    return pl.pallas_call(
        functools.partial(_ffn_kernel, after_mla=mla is not None),
        grid=(TOKENS // tm,),
        in_specs=[
            pl.BlockSpec((tm, D_MODEL), lambda t: (t, 0)),
            *mla_specs,
            _mod_spec(layer, sub, 0, tm), _mod_spec(layer, sub, 1, tm), _mod_spec(layer, sub, 2, tm),
            _row_spec(layer * N_SUB + sub), _row_spec(layer * N_SUB + sub),
            w_in_spec, w_in_spec, w_out_spec,
        ],
        out_specs=pl.BlockSpec((tm, D_MODEL), lambda t: (t, 0)),
        out_shape=jax.ShapeDtypeStruct((TOKENS, D_MODEL), F32),
        scratch_shapes=[pltpu.VMEM((tm, D_FF), BF16)],
        compiler_params=_params("arbitrary"),
        name="ffn" if mla is None else "mla_out_ffn",
    )(x, *mla_args, mod, mod, mod, pre, post, wg, wu, wd)


def _rope(z, ta, tb):
    return z * ta + pltpu.roll(z, HEAD_SLOT // 2, 1) * tb


def _mla_proj_kernel(x_ref, shift_ref, scale_ref, pre_ref, win_ref, qn_ref, kvn_ref,
                     wq_ref, wk_ref, wv_ref, ta_ref, tb_ref, q_ref, k_ref, v_ref):
    tm = x_ref.shape[0]
    subs = [slice(r0, r0 + PROJ_SUB) for r0 in range(0, tm, PROJ_SUB)]
    lats = []
    for rows in subs:
        hn = (_rms(x_ref[rows, :], pre_ref[...]) * (1.0 + scale_ref[...]) + shift_ref[...]).astype(BF16)
        lats.append(_dot(hn, win_ref[...]))
    for rows, lat in zip(subs, lats):
        cq = _rms(lat[:, :Q_LORA], qn_ref[...]).astype(BF16)
        ckv = _rms(lat[:, Q_LORA:Q_LORA + KV_LORA], kvn_ref[...]).astype(BF16)
        ta, tb = ta_ref[rows, :], tb_ref[rows, :]
        kr = _rope(lat[:, Q_LORA + KV_LORA:], ta, tb)
        q = _dot(cq, wq_ref[...]) * ((QK_NOPE + QK_ROPE) ** -0.5)
        kn = _dot(ckv, wk_ref[...])
        v_ref[rows, :] = _dot(ckv, wv_ref[...]).astype(BF16)
        for h in range(MLA_HEADS):
            sl = slice(h * HEAD_SLOT, (h + 1) * HEAD_SLOT)
            q_ref[rows, sl] = _rope(q[:, sl], ta, tb).astype(BF16)
            k_ref[rows, sl] = (kn[:, sl] + kr).astype(BF16)


def _head_slot(nope, rope, like):
    split = HEAD_SLOT // 2 - ROPE_HALF
    zeros = lambda n: jnp.zeros(like.shape[:-1] + (n,), like.dtype)
    nope = zeros(QK_NOPE) if nope is None else nope
    rope = zeros(QK_ROPE) if rope is None else rope
    return jnp.concatenate([rope[..., :ROPE_HALF], nope[..., :split], rope[..., ROPE_HALF:], nope[..., split:],
                            zeros(HEAD_SLOT - QK_NOPE - QK_ROPE)], axis=-1)


def _rope_tables():
    pos = jnp.arange(SEQ, dtype=F32)
    freqs = ROPE_THETA ** (-jnp.arange(ROPE_HALF, dtype=F32) / ROPE_HALF)
    ang = pos[:, None] * freqs[None, :]
    cos, sin = jnp.cos(ang), jnp.sin(ang)
    ones = jnp.ones((SEQ, QK_NOPE), F32)
    ta = _head_slot(ones, jnp.concatenate([cos, cos], axis=1), cos)
    tb = _head_slot(None, jnp.concatenate([-sin, sin], axis=1), sin)
    return ta, tb


def _mla_weights(w_in, w_q_up, w_kv_up):
    win = jnp.concatenate([w_in[:, :Q_LORA + KV_LORA],
                           _head_slot(None, w_in[:, Q_LORA + KV_LORA:], w_in)], axis=1).astype(BF16)
    wq = w_q_up.reshape(Q_LORA, MLA_HEADS, QK_NOPE + QK_ROPE)
    wq = _head_slot(wq[..., :QK_NOPE], wq[..., QK_NOPE:], wq)
    wq = wq.reshape(Q_LORA, MLA_HEADS * HEAD_SLOT).astype(BF16)
    wkv = w_kv_up.reshape(KV_LORA, MLA_HEADS, QK_NOPE + V_HEAD)
    wk = _head_slot(wkv[..., :QK_NOPE], None, wkv).reshape(KV_LORA, MLA_HEADS * HEAD_SLOT).astype(BF16)
    wv = wkv[..., QK_NOPE:].reshape(KV_LORA, MLA_HEADS * V_HEAD).astype(BF16)
    return win, wq, wk, wv


def _mla_proj(x, mod, pre, win, qn, kvn, wq, wk, wv, tables, layer):
    tm = TM_PROJ
    tiles_per_batch = SEQ // tm
    full = lambda a: pl.BlockSpec(a.shape, lambda t: (0,) * a.ndim)
    tab_spec = pl.BlockSpec((tm, HEAD_SLOT), lambda t: (t % tiles_per_batch, 0))
    wide = MLA_HEADS * HEAD_SLOT
    return pl.pallas_call(
        _mla_proj_kernel,
        grid=(TOKENS // tm,),
        in_specs=[
            pl.BlockSpec((tm, D_MODEL), lambda t: (t, 0)),
            _mod_spec(layer, 1, 0, tm), _mod_spec(layer, 1, 1, tm),
            _row_spec(layer * N_SUB + 1),
            full(win), full(qn), full(kvn), full(wq), full(wk), full(wv),
            tab_spec, tab_spec,
        ],
        out_specs=[
            pl.BlockSpec((tm, wide), lambda t: (t, 0)),
            pl.BlockSpec((tm, wide), lambda t: (t, 0)),
            pl.BlockSpec((tm, MLA_HEADS * V_HEAD), lambda t: (t, 0)),
        ],
        out_shape=[
            jax.ShapeDtypeStruct((TOKENS, wide), BF16),
            jax.ShapeDtypeStruct((TOKENS, wide), BF16),
            jax.ShapeDtypeStruct((TOKENS, MLA_HEADS * V_HEAD), BF16),
        ],
        compiler_params=_params("arbitrary"),
        name="mla_proj",
    )(x, mod, mod, pre, win, qn, kvn, wq, wk, wv, *tables)


def _mla_attn_kernel(q_ref, k_ref, v_ref, o_ref, vaug_ref):
    tq = TQ_MLA
    row = lax.broadcasted_iota(jnp.int32, (tq, tq), 0)
    col = lax.broadcasted_iota(jnp.int32, (tq, tq), 1)
    causal = col <= row
    first_head = lax.broadcasted_iota(jnp.int32, (tq, 2 * V_HEAD), 1) < V_HEAD
    vaug_ref[:, :LANES] = v_ref[...]
    vaug_ref[:, LANES:] = jnp.ones((SEQ, LANES), BF16)

    def logits(qi, hh):
        r0 = qi * tq
        hs = slice(hh * HEAD_SLOT, (hh + 1) * HEAD_SLOT)
        q = q_ref[r0:r0 + tq, hs]
        s_d = jnp.where(causal, _dot_nt(q, k_ref[r0:r0 + tq, hs]), MASK_VALUE)
        s_o = _dot_nt(q, k_ref[0:r0, hs]) if qi > 0 else None
        return s_d, s_o

    def attend(qi, s_d, s_o):
        r0 = qi * tq
        m = jnp.max(s_d, axis=-1, keepdims=True)
        if s_o is not None:
            m = jnp.maximum(m, jnp.max(s_o, axis=-1, keepdims=True))
        acc = _dot(jnp.exp(s_d - m).astype(BF16), vaug_ref[r0:r0 + tq, :])
        if s_o is not None:
            acc = acc + _dot(jnp.exp(s_o - m).astype(BF16), vaug_ref[0:r0, :])
        return acc[:, :LANES] * (1.0 / acc[:, LANES:])

    units = [(qi, hh) for qi in range(SEQ // tq) for hh in range(2)]
    pending = logits(*units[0])
    for i, (qi, hh) in enumerate(units):
        ahead = logits(*units[i + 1]) if i + 1 < len(units) else None
        o = attend(qi, *pending)
        if hh == 0:
            o_first = o
        else:
            o_ref[qi * tq:(qi + 1) * tq, :] = jnp.where(first_head, o_first, o).astype(BF16)
        pending = ahead


def _mla_attn(q, k, v):
    pairs = MLA_HEADS // 2
    return pl.pallas_call(
        _mla_attn_kernel,
        grid=(BATCH, pairs),
        in_specs=[
            pl.BlockSpec((SEQ, 2 * HEAD_SLOT), lambda b, p: (b, p)),
            pl.BlockSpec((SEQ, 2 * HEAD_SLOT), lambda b, p: (b, p)),
            pl.BlockSpec((SEQ, 2 * V_HEAD), lambda b, p: (b, p)),
        ],
        out_specs=pl.BlockSpec((SEQ, 2 * V_HEAD), lambda b, p: (b, p)),
        out_shape=jax.ShapeDtypeStruct((TOKENS, MLA_HEADS * V_HEAD), BF16),
        scratch_shapes=[pltpu.VMEM((SEQ, 2 * LANES), BF16)],
        compiler_params=_params("arbitrary", "arbitrary"),
        name="mla_attn",
    )(q, k, v)


def _t5_bucket(dist):
    max_exact = N_BUCKETS // 2
    d = jnp.maximum(dist, 1).astype(F32)
    large = max_exact + (jnp.log(d / max_exact) / math.log(MAX_DISTANCE / max_exact)
                         * (N_BUCKETS - max_exact)).astype(jnp.int32)
    large = jnp.minimum(large, N_BUCKETS - 1)
    return jnp.where(dist < max_exact, dist, large)


def _bias_selectors():
    j = jnp.arange(2 * DIL_BLOCK)
    rel = DIL_BLOCK - j
    sel = []
    for window, dilation in DIL_GROUPS:
        assert window // dilation == DIL_BLOCK
        bucket = _t5_bucket(jnp.maximum(rel, 0) * dilation)
        onehot = (bucket[None, :] == jnp.arange(N_BUCKETS)[:, None]) & (rel >= 0)[None, :]
        sel.append(onehot.astype(F32))
    return jnp.stack(sel)


def _bias_kernel(rbt_ref, sel_ref, o_ref):
    rbt = rbt_ref[...]
    sel = sel_ref[...]
    row0 = jnp.zeros((DIL_HEADS, 2 * DIL_BLOCK), F32)
    for b in range(N_BUCKETS):
        row0 = row0 + rbt[:, b:b + 1] * sel[b:b + 1, :]
    future = jnp.sum(sel, axis=0, keepdims=True) < 0.5
    row0 = jnp.where(future, MASK_VALUE, row0 * LOG2_E)
    for h in range(DIL_HEADS):
        t = jnp.broadcast_to(row0[h:h + 1, :], (DIL_BLOCK, 2 * DIL_BLOCK))
        o_ref[h] = pltpu.roll(t, 0, 1, stride=1, stride_axis=0)


def _bias_tables(rel_bias):
    rbt = rel_bias.T.reshape(N_GROUPS, DIL_HEADS, N_BUCKETS)
    return pl.pallas_call(
        _bias_kernel,
        grid=(N_GROUPS,),
        in_specs=[
            pl.BlockSpec((None, DIL_HEADS, N_BUCKETS), lambda g: (g, 0, 0)),
            pl.BlockSpec((None, N_BUCKETS, 2 * DIL_BLOCK), lambda g: (g, 0, 0)),
        ],
        out_specs=pl.BlockSpec((DIL_HEADS, DIL_BLOCK, 2 * DIL_BLOCK), lambda g: (g, 0, 0)),
        out_shape=jax.ShapeDtypeStruct((N_GROUPS * DIL_HEADS, DIL_BLOCK, 2 * DIL_BLOCK), F32),
        compiler_params=_params("arbitrary"),
        name="dil_bias",
    )(rbt, _bias_selectors())


def _dil_proj_kernel(x_ref, shift_ref, scale_ref, pre_ref, w_ref, o_ref, *scratch, dilation):
    tm = x_ref.shape[0]
    per = PROJ_SUB // dilation
    slabs = D_MODEL // LANES
    hbs = []
    for i, r0 in enumerate(range(0, tm, PROJ_SUB)):
        hn = _rms(x_ref[r0:r0 + PROJ_SUB, :], pre_ref[...]) * (1.0 + scale_ref[...]) + shift_ref[...]
        if dilation == 1:
            hbs.append(hn.astype(BF16))
            continue
        slab_ref, = scratch
        for s in range(slabs):
            slab_ref[i, s] = hn[:, s * LANES:(s + 1) * LANES]
        hbs.append(jnp.concatenate(
            [jnp.concatenate([slab_ref[i, s, pl.ds(r, per, stride=dilation), :] for s in range(slabs)],
                             axis=1).astype(BF16)
             for r in range(dilation)], axis=0))
    for i, hb in enumerate(hbs):
        for j in range(3):
            cols = slice(j * DIL_WIDTH, (j + 1) * DIL_WIDTH)
            y = _dot(hb, w_ref[:, cols])
            if j == 0:
                y = y * (DIL_HEAD_DIM ** -0.5 * LOG2_E)
            y = y.astype(BF16)
            for r in range(dilation):
                o_ref[r, i * per:(i + 1) * per, cols] = y[r * per:(r + 1) * per, :]


def _dil_proj(x, mod, pre, w, layer, group):
    _, dilation = DIL_GROUPS[group]
    tm = TM_DIL_PROJ
    tiles_per_batch = SEQ // tm
    sub = tm // dilation
    scratch = [] if dilation == 1 else [
        pltpu.VMEM((tm // PROJ_SUB, D_MODEL // LANES, PROJ_SUB, LANES), F32)]
    return pl.pallas_call(
        functools.partial(_dil_proj_kernel, dilation=dilation),
        grid=(TOKENS // tm,),
        in_specs=[
            pl.BlockSpec((tm, D_MODEL), lambda t: (t, 0)),
            _mod_spec(layer, 1, 0, tm), _mod_spec(layer, 1, 1, tm),
            _row_spec(layer * N_SUB + 1),
            pl.BlockSpec((D_MODEL, 3 * DIL_WIDTH), lambda t: (0, group), pipeline_mode=pl.Buffered(1)),
        ],
        out_specs=pl.BlockSpec((None, dilation, sub, 3 * DIL_WIDTH),
                               lambda t: (t // tiles_per_batch, 0, t % tiles_per_batch, 0)),
        out_shape=jax.ShapeDtypeStruct((BATCH, dilation, SEQ // dilation, 3 * DIL_WIDTH), BF16),
        scratch_shapes=scratch,
        compiler_params=_params("arbitrary"),
        name=f"dil_proj_g{group}",
    )(x, mod, mod, pre, w)


DIL_HEADS_PER_STEP = 8
DIL_PAIRS_PER_STEP = DIL_HEADS_PER_STEP // 2


STAT_MAX_SHIFT = 32


def _stat_lane(head):
    return head if head % 2 else DIL_HEAD_DIM + head


def _dil_attn_kernel(q_ref, k_ref, v_ref, bias_ref, o_ref, stat_ref, s_ref, acc_ref, *, dilation, n_blocks):
    blk = DIL_BLOCK
    lane = lax.broadcasted_iota(jnp.int32, (blk, LANES), 1)
    first_head = lane < DIL_HEAD_DIM
    head0 = pl.program_id(1) * DIL_HEADS_PER_STEP

    def where_unit(i, first):
        if first:
            return i, 0
        later = n_blocks - 1
        if dilation == 1:
            return 0, i + 1
        i = jnp.asarray(i, jnp.int32)
        return lax.div(i, jnp.int32(later)), lax.rem(i, jnp.int32(later)) + 1

    def window(ref, r, n, first, cs):
        if first:
            return ref[r, 0:blk, cs]
        return ref[r, pl.ds(pl.multiple_of(n * blk - blk, blk), 2 * blk), cs]

    def logits(i, slot, first):
        r, n = where_unit(i, first)
        kw = blk if first else 2 * blk
        q_rows = slice(0, blk) if first else pl.ds(pl.multiple_of(n * blk, blk), blk)
        for hp in range(DIL_PAIRS_PER_STEP):
            cs = slice(hp * LANES, (hp + 1) * LANES)
            q2 = q_ref[r, q_rows, cs]
            k2 = window(k_ref, r, n, first, cs)
            for hh in range(2):
                h = 2 * hp + hh
                qh = jnp.where(first_head if hh == 0 else jnp.logical_not(first_head), q2, 0)
                bias = bias_ref[h, :, blk:] if first else bias_ref[h]
                s_ref[slot, h, :, 0:kw] = _dot_nt(qh, k2) + bias

    def attend(i, slot, first):
        r, n = where_unit(i, first)
        kw = blk if first else 2 * blk
        t0 = n * (blk * dilation) + r
        rows = pl.ds(t0, blk) if dilation == 1 else pl.ds(t0, blk, stride=dilation)
        stat_tile = jnp.ones((blk, LANES), F32)
        for hp in range(DIL_PAIRS_PER_STEP):
            cs = slice(hp * LANES, (hp + 1) * LANES)
            v2 = window(v_ref, r, n, first, cs)
            accs = []
            for hh in range(2):
                s = s_ref[slot, 2 * hp + hh, :, 0:kw]
                m = jnp.max(s, axis=-1, keepdims=True)
                key_lane = lax.broadcasted_iota(jnp.int32, (kw, LANES), 1)
                own = key_lane < DIL_HEAD_DIM if hh == 0 else key_lane >= DIL_HEAD_DIM
                acc = _dot(jnp.exp2(s - m).astype(BF16), jnp.where(own, v2, 1))
                sum_lane = head0 + _stat_lane(2 * hp + hh)
                stat_tile = jnp.where(lane == sum_lane, acc, stat_tile)
                stat_tile = jnp.where(lane == sum_lane + STAT_MAX_SHIFT, m, stat_tile)
                accs.append(acc)
            acc_ref[hp, rows, :] = jnp.where(first_head, accs[0], accs[1])
        stat_ref[rows, :] = stat_tile

    def run(count, first):
        logits(0, 0, first)

        def body(j, carry):
            i = 2 * j
            logits(i + 1, 1, first)
            attend(i, 0, first)
            logits(jnp.minimum(i + 2, count - 1), 0, first)
            attend(i + 1, 1, first)
            return carry
        if count >= 2:
            lax.fori_loop(0, count // 2, body, 0)
        if count % 2:
            attend(count - 1, 0, first)

    run(dilation, True)
    if n_blocks > 1:
        run(dilation * (n_blocks - 1), False)
    for hp in range(DIL_PAIRS_PER_STEP):
        o_ref[:, hp * LANES:(hp + 1) * LANES] = acc_ref[hp].astype(BF16)


def _dil_attn(proj, bias, group):
    _, dilation = DIL_GROUPS[group]
    length = SEQ // dilation
    halves = DIL_HEADS // DIL_HEADS_PER_STEP
    width = DIL_HEADS_PER_STEP * DIL_HEAD_DIM
    qkv_spec = lambda kind: pl.BlockSpec(
        (None, dilation, length, width), lambda b, u: (b, 0, 0, kind * halves + u))
    return pl.pallas_call(
        functools.partial(_dil_attn_kernel, dilation=dilation, n_blocks=length // DIL_BLOCK),
        grid=(BATCH, halves),
        in_specs=[
            qkv_spec(0), qkv_spec(1), qkv_spec(2),
            pl.BlockSpec((DIL_HEADS_PER_STEP, DIL_BLOCK, 2 * DIL_BLOCK),
                         lambda b, u: (group * halves + u, 0, 0)),
        ],
        out_specs=[
            pl.BlockSpec((None, SEQ, width), lambda b, u: (b, 0, u)),
            pl.BlockSpec((None, SEQ, LANES), lambda b, u: (b, 0, u)),
        ],
        out_shape=[
            jax.ShapeDtypeStruct((BATCH, SEQ, DIL_WIDTH), BF16),
            jax.ShapeDtypeStruct((BATCH, SEQ, halves * LANES), F32),
        ],
        scratch_shapes=[pltpu.VMEM((2, DIL_HEADS_PER_STEP, DIL_BLOCK, 2 * DIL_BLOCK), F32),
                        pltpu.VMEM((DIL_PAIRS_PER_STEP, SEQ, LANES), F32)],
        compiler_params=_params("arbitrary", "arbitrary"),
        name=f"dil_attn_g{group}",
    )(proj, proj, proj, bias)


def _dil_out_kernel(x_ref, o0_ref, o1_ref, o2_ref, s0_ref, s1_ref, s2_ref, e_ref, wo_ref,
                    gate_ref, post_ref, out_ref):
    lane = lax.broadcasted_iota(jnp.int32, (1, LANES), 1)
    in_first_block = functools.reduce(jnp.logical_or, [
        jnp.logical_or(lane == _stat_lane(h), lane == _stat_lane(h) + STAT_MAX_SHIFT)
        for h in range(DIL_HEADS_PER_STEP)])
    sums = [jnp.where(in_first_block, r[:, :LANES], r[:, LANES:]) for r in (s0_ref, s1_ref, s2_ref)]
    maxes = [pltpu.roll(s, LANES - STAT_MAX_SHIFT, 1) for s in sums]
    m = jnp.maximum(jnp.maximum(maxes[0], maxes[1]), maxes[2])
    es = [jnp.exp2(mg - m) for mg in maxes]
    inv = 1.0 / (sums[0] * es[0] + sums[1] * es[1] + sums[2] * es[2])
    e = e_ref[...]
    mix = None
    for eg, o_ref in zip(es, (o0_ref, o1_ref, o2_ref)):
        alpha = eg * inv
        hi = alpha.astype(BF16)
        lo = (alpha - hi.astype(F32)).astype(BF16)
        term = _dot(jnp.concatenate([hi, lo], axis=1), e) * o_ref[...].astype(F32)
        mix = term if mix is None else mix + term
    y = _dot(mix.astype(BF16), wo_ref[...])
    out_ref[...] = x_ref[...] + gate_ref[...] * _rms(y, post_ref[...])


def _dil_out(x, outs, stats, mod, post, w_o, layer):
    tm = TM_OUT
    tiles_per_batch = SEQ // tm
    halves = DIL_HEADS // DIL_HEADS_PER_STEP
    stat_lane_of_col = jnp.array([_stat_lane(c // DIL_HEAD_DIM) for c in range(DIL_WIDTH)], jnp.int32)
    lane_of_row = jnp.arange(2 * LANES) % LANES
    expand = (lane_of_row[:, None] == stat_lane_of_col[None, :]).astype(BF16)
    batch_rows = lambda width: pl.BlockSpec(
        (None, tm, width), lambda t: (t // tiles_per_batch, t % tiles_per_batch, 0))
    tok = pl.BlockSpec((tm, D_MODEL), lambda t: (t, 0))
    return pl.pallas_call(
        _dil_out_kernel,
        grid=(TOKENS // tm,),
        in_specs=[
            tok, *([batch_rows(DIL_WIDTH)] * 3), *([batch_rows(halves * LANES)] * 3),
            pl.BlockSpec(expand.shape, lambda t: (0, 0)), pl.BlockSpec(w_o.shape, lambda t: (0, 0)),
            _mod_spec(layer, 1, 2, tm), _row_spec(layer * N_SUB + 1),
        ],
        out_specs=tok,
        out_shape=jax.ShapeDtypeStruct((TOKENS, D_MODEL), F32),
        compiler_params=_params("arbitrary"),
        name="dil_out",
    )(x, *outs, *stats, expand, w_o, mod, post)


def kernel(x, c, norm_pre, norm_post, w_mod, b_mod, ffn_w_gate, ffn_w_up, ffn_w_down,
           mla_w_in, mla_q_norm, mla_w_q_up, mla_kv_norm, mla_w_kv_up, mla_w_o,
           dil_w_in, dil_w_o, rel_bias):
    assert x.shape == (BATCH, SEQ, D_MODEL) and x.dtype == F32
    h = x.reshape(TOKENS, D_MODEL)
    mod = _modulation(c, w_mod, b_mod)
    pre = norm_pre.reshape(DEPTH * N_SUB, 1, D_MODEL)
    post = norm_post.reshape(DEPTH * N_SUB, 1, D_MODEL)
    wg, wu, wd = ffn_w_gate.astype(BF16), ffn_w_up.astype(BF16), ffn_w_down.astype(BF16)
    tables = _rope_tables()
    bias = _bias_tables(rel_bias)

    for layer in range(DEPTH):
        h = _ffn(h, mod, pre, post, wg, wu, wd, layer, 0)
        idx = layer // 2
        if layer % 2 == 0:
            win, wq, wk, wv = _mla_weights(mla_w_in[idx], mla_w_q_up[idx], mla_w_kv_up[idx])
            q, k, v = _mla_proj(h, mod, pre, win, mla_q_norm[idx][None, :], mla_kv_norm[idx][None, :],
                                wq, wk, wv, tables, layer)
            mla = (_mla_attn(q, k, v), mla_w_o[idx].astype(BF16))
        else:
            w_in = dil_w_in[idx].astype(BF16)
            outs, stats = zip(*[_dil_attn(_dil_proj(h, mod, pre, w_in, layer, g), bias, g)
                                for g in range(N_GROUPS)])
            h = _dil_out(h, outs, stats, mod, post, dil_w_o[idx].astype(BF16), layer)
            mla = None
        h = _ffn(h, mod, pre, post, wg, wu, wd, layer, 1, mla=mla)
    return h.reshape(BATCH, SEQ, D_MODEL)
```

```python
import functools
import math

import jax
import jax.numpy as jnp
from jax import lax
from jax.experimental import pallas as pl
from jax.experimental.pallas import tpu as pltpu

F32 = jnp.float32
BF16 = jnp.bfloat16

D_MODEL = 1024
BATCH = 8
SEQ = 2048
DEPTH = 2
N_SUB = 3
D_FF = 2816
FFN_RES = 0.5
EPS = 1e-6

MLA_HEADS = 16
Q_LORA = 384
KV_LORA = 256
QK_NOPE = 64
QK_ROPE = 32
V_HEAD = 64
ROPE_THETA = 10000.0

DIL_GROUPS = ((128, 1), (512, 4), (2048, 16))
N_GROUPS = 3
DIL_HEADS = 16
DIL_HEAD_DIM = 64
DIL_BLOCK = 128
DIL_WIDTH = DIL_HEADS * DIL_HEAD_DIM
N_BUCKETS = 32
MAX_DISTANCE = 2048

TOKENS = BATCH * SEQ
LANES = 128
HEAD_SLOT = 128
ROPE_HALF = QK_ROPE // 2
MASK_VALUE = -1e30
LOG2_E = math.log2(math.e)
VMEM_LIMIT = 56 * 1024 * 1024

TM_FFN = 1024
FFN_SUB = 512
TM_PROJ = 1024
PROJ_SUB = 256
TM_DIL_PROJ = 1024
TN_DIL_PROJ = 1024
TM_OUT = 512
TQ_MLA = 256
MLA_SUM_ROWS = 16
MLA_LOOKAHEAD = 6
TN_MOD = 1536
FF_CHUNK = 1536


def _params(*sem):
    return pltpu.CompilerParams(dimension_semantics=sem, vmem_limit_bytes=VMEM_LIMIT)


def _rms(x, g):
    ms = jnp.mean(x * x, axis=-1, keepdims=True)
    return x * lax.rsqrt(ms + EPS) * g


def _dot(a, b):
    return jnp.dot(a, b, preferred_element_type=F32)


def _dot_nt(a, b):
    return lax.dot_general(a, b, (((1,), (1,)), ((), ())), preferred_element_type=F32)


def _silu(x):
    return x * jax.nn.sigmoid(x)


def _mod_spec(layer, sub, kind, tm):
    tiles_per_batch = SEQ // tm
    return pl.BlockSpec(
        (None, 1, D_MODEL),
        lambda t, *_: ((layer * BATCH + t // tiles_per_batch) * (N_SUB * 3) + sub * 3 + kind, 0, 0))


def _row_spec(index):
    return pl.BlockSpec((None, 1, D_MODEL), lambda *_: (index, 0, 0))


def _mod_kernel(c_ref, w_ref, b_ref, o_ref):
    h = _silu(c_ref[...])
    o_ref[...] = jnp.dot(h, w_ref[...], preferred_element_type=F32,
                         precision=lax.Precision.HIGHEST) + b_ref[...]


def _modulation(c, w_mod, b_mod):
    n = N_SUB * 3 * D_MODEL
    out = pl.pallas_call(
        _mod_kernel,
        grid=(DEPTH, n // TN_MOD),
        in_specs=[
            pl.BlockSpec((BATCH, D_MODEL), lambda i, j: (0, 0)),
            pl.BlockSpec((None, D_MODEL, TN_MOD), lambda i, j: (i, 0, j)),
            pl.BlockSpec((None, 1, TN_MOD), lambda i, j: (i, 0, j)),
        ],
        out_specs=pl.BlockSpec((None, BATCH, TN_MOD), lambda i, j: (i, 0, j)),
        out_shape=jax.ShapeDtypeStruct((DEPTH, BATCH, n), F32),
        compiler_params=_params("arbitrary", "arbitrary"),
        name="modulation",
    )(c, w_mod, b_mod.reshape(DEPTH, 1, n))
    return out.reshape(DEPTH * BATCH * N_SUB * 3, 1, D_MODEL)


def _mla_mix(rows, att_ref, wo_ref):
    return _dot(att_ref[rows, :], wo_ref[...])


def _ffn_kernel(x_ref, *refs, after_mla):
    if after_mla:
        att_ref, wo_ref, mgate_ref, mpost_ref, *refs = refs
    shift_ref, scale_ref, gate_ref, pre_ref, post_ref, wg_ref, wu_ref, wd_ref, o_ref, a_ref = refs
    tm = x_ref.shape[0]
    subs = [slice(r0, r0 + FFN_SUB) for r0 in range(0, tm, FFN_SUB)]
    hns = []
    for rows in subs:
        x = x_ref[rows, :]
        if after_mla:
            x = x + mgate_ref[...] * _rms(_mla_mix(rows, att_ref, wo_ref), mpost_ref[...])
            o_ref[rows, :] = x
        hns.append((_rms(x, pre_ref[...]) * (1.0 + scale_ref[...]) + shift_ref[...]).astype(BF16))
    for rows, hn in zip(subs, hns):
        for c0 in range(0, D_FF, FF_CHUNK):
            c1 = min(c0 + FF_CHUNK, D_FF)
            g = _dot(hn, wg_ref[:, c0:c1])
            u = _dot(hn, wu_ref[:, c0:c1])
            a_ref[rows, c0:c1] = (_silu(g) * u).astype(BF16)
    for rows in subs:
        y = _dot(a_ref[rows, :], wd_ref[...])
        x = o_ref[rows, :] if after_mla else x_ref[rows, :]
        o_ref[rows, :] = x + FFN_RES * gate_ref[...] * _rms(y, post_ref[...])


def _ffn(x, mod, pre, post, wg, wu, wd, layer, which, mla=None):
    sub = 0 if which == 0 else 2
    tm = TM_FFN
    w_in_spec = pl.BlockSpec((None, None, D_MODEL, D_FF), lambda t: (layer, which, 0, 0),
                             pipeline_mode=pl.Buffered(1))
    w_out_spec = pl.BlockSpec((None, None, D_FF, D_MODEL), lambda t: (layer, which, 0, 0),
                              pipeline_mode=pl.Buffered(1))
    mla_args, mla_specs = [], []
    if mla is not None:
        att, w_o = mla
        mla_args = [att, w_o, mod, post]
        mla_specs = [pl.BlockSpec((tm, att.shape[1]), lambda t: (t, 0)),
                     pl.BlockSpec(w_o.shape, lambda t: (0, 0)),
                     _mod_spec(layer, 1, 2, tm), _row_spec(layer * N_SUB + 1)]
    return pl.pallas_call(
        functools.partial(_ffn_kernel, after_mla=mla is not None),
        grid=(TOKENS // tm,),
        in_specs=[
            pl.BlockSpec((tm, D_MODEL), lambda t: (t, 0)),
            *mla_specs,
            _mod_spec(layer, sub, 0, tm), _mod_spec(layer, sub, 1, tm), _mod_spec(layer, sub, 2, tm),
            _row_spec(layer * N_SUB + sub), _row_spec(layer * N_SUB + sub),
            w_in_spec, w_in_spec, w_out_spec,
        ],
        out_specs=pl.BlockSpec((tm, D_MODEL), lambda t: (t, 0)),
        out_shape=jax.ShapeDtypeStruct((TOKENS, D_MODEL), F32),
        scratch_shapes=[pltpu.VMEM((tm, D_FF), BF16)],
        compiler_params=_params("arbitrary"),
        name="ffn" if mla is None else "mla_out_ffn",
    )(x, *mla_args, mod, mod, mod, pre, post, wg, wu, wd)


def _rope(z, ta, tb):
    return z * ta + pltpu.roll(z, HEAD_SLOT // 2, 1) * tb


def _mla_proj_kernel(x_ref, shift_ref, scale_ref, pre_ref, win_ref, qn_ref, kvn_ref,
                     wq_ref, wk_ref, wv_ref, ta_ref, tb_ref, q_ref, k_ref, v_ref):
    tm = x_ref.shape[0]
    subs = [slice(r0, r0 + PROJ_SUB) for r0 in range(0, tm, PROJ_SUB)]
    lats = []
    for rows in subs:
        hn = (_rms(x_ref[rows, :], pre_ref[...]) * (1.0 + scale_ref[...]) + shift_ref[...]).astype(BF16)
        lats.append(_dot(hn, win_ref[...]))
    for rows, lat in zip(subs, lats):
        cq = _rms(lat[:, :Q_LORA], qn_ref[...]).astype(BF16)
        ckv = _rms(lat[:, Q_LORA:Q_LORA + KV_LORA], kvn_ref[...]).astype(BF16)
        ta, tb = ta_ref[rows, :], tb_ref[rows, :]
        kr = _rope(lat[:, Q_LORA + KV_LORA:], ta, tb)
        q = _dot(cq, wq_ref[...]) * ((QK_NOPE + QK_ROPE) ** -0.5 * LOG2_E)
        kn = _dot(ckv, wk_ref[...])
        v_ref[rows, :] = _dot(ckv, wv_ref[...]).astype(BF16)
        for h in range(MLA_HEADS):
            sl = slice(h * HEAD_SLOT, (h + 1) * HEAD_SLOT)
            q_ref[rows, sl] = _rope(q[:, sl], ta, tb).astype(BF16)
            k_ref[rows, sl] = (kn[:, sl] + kr).astype(BF16)


def _head_slot(nope, rope, like):
    split = HEAD_SLOT // 2 - ROPE_HALF
    zeros = lambda n: jnp.zeros(like.shape[:-1] + (n,), like.dtype)
    nope = zeros(QK_NOPE) if nope is None else nope
    rope = zeros(QK_ROPE) if rope is None else rope
    return jnp.concatenate([rope[..., :ROPE_HALF], nope[..., :split], rope[..., ROPE_HALF:], nope[..., split:],
                            zeros(HEAD_SLOT - QK_NOPE - QK_ROPE)], axis=-1)


def _rope_tables():
    pos = jnp.arange(SEQ, dtype=F32)
    freqs = ROPE_THETA ** (-jnp.arange(ROPE_HALF, dtype=F32) / ROPE_HALF)
    ang = pos[:, None] * freqs[None, :]
    cos, sin = jnp.cos(ang), jnp.sin(ang)
    ones = jnp.ones((SEQ, QK_NOPE), F32)
    ta = _head_slot(ones, jnp.concatenate([cos, cos], axis=1), cos)
    tb = _head_slot(None, jnp.concatenate([-sin, sin], axis=1), sin)
    return ta, tb


def _mla_weights(w_in, w_q_up, w_kv_up):
    win = jnp.concatenate([w_in[:, :Q_LORA + KV_LORA],
                           _head_slot(None, w_in[:, Q_LORA + KV_LORA:], w_in)], axis=1).astype(BF16)
    wq = w_q_up.reshape(Q_LORA, MLA_HEADS, QK_NOPE + QK_ROPE)
    wq = _head_slot(wq[..., :QK_NOPE], wq[..., QK_NOPE:], wq)
    wq = wq.reshape(Q_LORA, MLA_HEADS * HEAD_SLOT).astype(BF16)
    wkv = w_kv_up.reshape(KV_LORA, MLA_HEADS, QK_NOPE + V_HEAD)
    wk = _head_slot(wkv[..., :QK_NOPE], None, wkv).reshape(KV_LORA, MLA_HEADS * HEAD_SLOT).astype(BF16)
    wv = wkv[..., QK_NOPE:].reshape(KV_LORA, MLA_HEADS * V_HEAD).astype(BF16)
    return win, wq, wk, wv


def _mla_proj(x, mod, pre, win, qn, kvn, wq, wk, wv, tables, layer):
    tm = TM_PROJ
    tiles_per_batch = SEQ // tm
    full = lambda a: pl.BlockSpec(a.shape, lambda t: (0,) * a.ndim)
    tab_spec = pl.BlockSpec((tm, HEAD_SLOT), lambda t: (t % tiles_per_batch, 0))
    wide = MLA_HEADS * HEAD_SLOT
    return pl.pallas_call(
        _mla_proj_kernel,
        grid=(TOKENS // tm,),
        in_specs=[
            pl.BlockSpec((tm, D_MODEL), lambda t: (t, 0)),
            _mod_spec(layer, 1, 0, tm), _mod_spec(layer, 1, 1, tm),
            _row_spec(layer * N_SUB + 1),
            full(win), full(qn), full(kvn), full(wq), full(wk), full(wv),
            tab_spec, tab_spec,
        ],
        out_specs=[
            pl.BlockSpec((tm, wide), lambda t: (t, 0)),
            pl.BlockSpec((tm, wide), lambda t: (t, 0)),
            pl.BlockSpec((tm, MLA_HEADS * V_HEAD), lambda t: (t, 0)),
        ],
        out_shape=[
            jax.ShapeDtypeStruct((TOKENS, wide), BF16),
            jax.ShapeDtypeStruct((TOKENS, wide), BF16),
            jax.ShapeDtypeStruct((TOKENS, MLA_HEADS * V_HEAD), BF16),
        ],
        compiler_params=_params("arbitrary"),
        name="mla_proj",
    )(x, mod, mod, pre, win, qn, kvn, wq, wk, wv, *tables)


def _mla_attn_kernel(q_ref, k_ref, v_ref, o_ref, vt_ref, ot_ref, pv_ref):
    tq = TQ_MLA
    key = lax.broadcasted_iota(jnp.int32, (tq, tq), 0)
    query = lax.broadcasted_iota(jnp.int32, (tq, tq), 1)
    causal = key <= query
    vt_ref[:2 * V_HEAD, :] = v_ref[...].astype(F32).T.astype(BF16)
    vt_ref[2 * V_HEAD:, :] = jnp.ones((MLA_SUM_ROWS, SEQ), BF16)

    def logits(hh, qi, c):
        hs = slice(hh * HEAD_SLOT, (hh + 1) * HEAD_SLOT)
        s = _dot_nt(k_ref[c * tq:(c + 1) * tq, hs], q_ref[qi * tq:(qi + 1) * tq, hs])
        return jnp.where(causal, s, MASK_VALUE) if c == qi else s

    chunks = [(hh, qi, c) for hh in range(2) for qi in range(SEQ // tq) for c in range(qi + 1)]
    queued = [logits(*ch) for ch in chunks[:MLA_LOOKAHEAD]]
    for i, (hh, qi, c) in enumerate(chunks):
        if i + MLA_LOOKAHEAD < len(chunks):
            queued.append(logits(*chunks[i + MLA_LOOKAHEAD]))
        s = queued.pop(0)
        m_chunk = jnp.max(s, axis=0, keepdims=True)
        if c == 0:
            m = m_chunk
            acc = _dot(vt_ref[:, 0:tq], jnp.exp2(s - m).astype(BF16))
        else:
            m_new = jnp.maximum(m, m_chunk)
            pv_ref[i % 2] = _dot(vt_ref[:, c * tq:(c + 1) * tq], jnp.exp2(s - m_new).astype(BF16))
            acc = acc * jnp.exp2(m - m_new) + pv_ref[i % 2]
            m = m_new
        if c == qi:
            rows = slice(hh * V_HEAD, (hh + 1) * V_HEAD)
            ot_ref[rows, qi * tq:(qi + 1) * tq] = acc[rows, :] * (1.0 / acc[2 * V_HEAD:2 * V_HEAD + 1, :])
    o_ref[...] = ot_ref[...].T.astype(BF16)


def _mla_attn(q, k, v):
    pairs = MLA_HEADS // 2
    return pl.pallas_call(
        _mla_attn_kernel,
        grid=(BATCH, pairs),
        in_specs=[
            pl.BlockSpec((SEQ, 2 * HEAD_SLOT), lambda b, p: (b, p)),
            pl.BlockSpec((SEQ, 2 * HEAD_SLOT), lambda b, p: (b, p)),
            pl.BlockSpec((SEQ, 2 * V_HEAD), lambda b, p: (b, p)),
        ],
        out_specs=pl.BlockSpec((SEQ, 2 * V_HEAD), lambda b, p: (b, p)),
        out_shape=jax.ShapeDtypeStruct((TOKENS, MLA_HEADS * V_HEAD), BF16),
        scratch_shapes=[pltpu.VMEM((2 * V_HEAD + MLA_SUM_ROWS, SEQ), BF16),
                        pltpu.VMEM((2 * V_HEAD, SEQ), F32),
                        pltpu.VMEM((2, 2 * V_HEAD + MLA_SUM_ROWS, TQ_MLA), F32)],
        compiler_params=_params("arbitrary", "arbitrary"),
        name="mla_attn",
    )(q, k, v)


def _t5_bucket(dist):
    max_exact = N_BUCKETS // 2
    d = jnp.maximum(dist, 1).astype(F32)
    large = max_exact + (jnp.log(d / max_exact) / math.log(MAX_DISTANCE / max_exact)
                         * (N_BUCKETS - max_exact)).astype(jnp.int32)
    large = jnp.minimum(large, N_BUCKETS - 1)
    return jnp.where(dist < max_exact, dist, large)


def _bias_selectors():
    j = jnp.arange(2 * DIL_BLOCK)
    rel = DIL_BLOCK - j
    sel = []
    for window, dilation in DIL_GROUPS:
        assert window // dilation == DIL_BLOCK
        bucket = _t5_bucket(jnp.maximum(rel, 0) * dilation)
        onehot = (bucket[None, :] == jnp.arange(N_BUCKETS)[:, None]) & (rel >= 0)[None, :]
        sel.append(onehot.astype(F32))
    return jnp.stack(sel)


def _bias_kernel(rbt_ref, sel_ref, o_ref):
    rbt = rbt_ref[...]
    sel = sel_ref[...]
    row0 = jnp.zeros((DIL_HEADS, 2 * DIL_BLOCK), F32)
    for b in range(N_BUCKETS):
        row0 = row0 + rbt[:, b:b + 1] * sel[b:b + 1, :]
    future = jnp.sum(sel, axis=0, keepdims=True) < 0.5
    row0 = jnp.where(future, MASK_VALUE, row0 * LOG2_E)
    for h in range(DIL_HEADS):
        t = jnp.broadcast_to(row0[h:h + 1, :], (DIL_BLOCK, 2 * DIL_BLOCK))
        o_ref[h] = pltpu.roll(t, 0, 1, stride=1, stride_axis=0)


def _bias_tables(rel_bias):
    rbt = rel_bias.T.reshape(N_GROUPS, DIL_HEADS, N_BUCKETS)
    return pl.pallas_call(
        _bias_kernel,
        grid=(N_GROUPS,),
        in_specs=[
            pl.BlockSpec((None, DIL_HEADS, N_BUCKETS), lambda g: (g, 0, 0)),
            pl.BlockSpec((None, N_BUCKETS, 2 * DIL_BLOCK), lambda g: (g, 0, 0)),
        ],
        out_specs=pl.BlockSpec((DIL_HEADS, DIL_BLOCK, 2 * DIL_BLOCK), lambda g: (g, 0, 0)),
        out_shape=jax.ShapeDtypeStruct((N_GROUPS * DIL_HEADS, DIL_BLOCK, 2 * DIL_BLOCK), F32),
        compiler_params=_params("arbitrary"),
        name="dil_bias",
    )(rbt, _bias_selectors())


def _dil_proj_kernel(x_ref, shift_ref, scale_ref, pre_ref, w_ref, o_ref, *scratch, dilation):
    tm = x_ref.shape[0]
    per = PROJ_SUB // dilation
    slabs = D_MODEL // LANES
    hbs = []
    for i, r0 in enumerate(range(0, tm, PROJ_SUB)):
        hn = _rms(x_ref[r0:r0 + PROJ_SUB, :], pre_ref[...]) * (1.0 + scale_ref[...]) + shift_ref[...]
        if dilation == 1:
            hbs.append(hn.astype(BF16))
            continue
        slab_ref, = scratch
        for s in range(slabs):
            slab_ref[i, s] = hn[:, s * LANES:(s + 1) * LANES]
        hbs.append(jnp.concatenate(
            [jnp.concatenate([slab_ref[i, s, pl.ds(r, per, stride=dilation), :] for s in range(slabs)],
                             axis=1).astype(BF16)
             for r in range(dilation)], axis=0))
    for i, hb in enumerate(hbs):
        for j in range(3):
            cols = slice(j * DIL_WIDTH, (j + 1) * DIL_WIDTH)
            y = _dot(hb, w_ref[:, cols])
            if j == 0:
                y = y * (DIL_HEAD_DIM ** -0.5 * LOG2_E)
            y = y.astype(BF16)
            for r in range(dilation):
                o_ref[r, i * per:(i + 1) * per, cols] = y[r * per:(r + 1) * per, :]


def _dil_proj(x, mod, pre, w, layer, group):
    _, dilation = DIL_GROUPS[group]
    tm = TM_DIL_PROJ
    tiles_per_batch = SEQ // tm
    sub = tm // dilation
    scratch = [] if dilation == 1 else [
        pltpu.VMEM((tm // PROJ_SUB, D_MODEL // LANES, PROJ_SUB, LANES), F32)]
    return pl.pallas_call(
        functools.partial(_dil_proj_kernel, dilation=dilation),
        grid=(TOKENS // tm,),
        in_specs=[
            pl.BlockSpec((tm, D_MODEL), lambda t: (t, 0)),
            _mod_spec(layer, 1, 0, tm), _mod_spec(layer, 1, 1, tm),
            _row_spec(layer * N_SUB + 1),
            pl.BlockSpec((D_MODEL, 3 * DIL_WIDTH), lambda t: (0, group), pipeline_mode=pl.Buffered(1)),
        ],
        out_specs=pl.BlockSpec((None, dilation, sub, 3 * DIL_WIDTH),
                               lambda t: (t // tiles_per_batch, 0, t % tiles_per_batch, 0)),
        out_shape=jax.ShapeDtypeStruct((BATCH, dilation, SEQ // dilation, 3 * DIL_WIDTH), BF16),
        scratch_shapes=scratch,
        compiler_params=_params("arbitrary"),
        name=f"dil_proj_g{group}",
    )(x, mod, mod, pre, w)


DIL_HEADS_PER_STEP = 8
DIL_PAIRS_PER_STEP = DIL_HEADS_PER_STEP // 2


STAT_MAX_SHIFT = 32


def _stat_lane(head):
    return head if head % 2 else DIL_HEAD_DIM + head


def _dil_attn_kernel(q_ref, k_ref, v_ref, bias_ref, o_ref, stat_ref, s_ref, acc_ref, *, dilation, n_blocks):
    blk = DIL_BLOCK
    lane = lax.broadcasted_iota(jnp.int32, (blk, LANES), 1)
    first_head = lane < DIL_HEAD_DIM
    head0 = pl.program_id(1) * DIL_HEADS_PER_STEP

    def where_unit(i, first):
        if first:
            return i, 0
        later = n_blocks - 1
        if dilation == 1:
            return 0, i + 1
        i = jnp.asarray(i, jnp.int32)
        return lax.div(i, jnp.int32(later)), lax.rem(i, jnp.int32(later)) + 1

    def window(ref, r, n, first, cs):
        if first:
            return ref[r, 0:blk, cs]
        return ref[r, pl.ds(pl.multiple_of(n * blk - blk, blk), 2 * blk), cs]

    def logits(i, slot, first):
        r, n = where_unit(i, first)
        kw = blk if first else 2 * blk
        q_rows = slice(0, blk) if first else pl.ds(pl.multiple_of(n * blk, blk), blk)
        for hp in range(DIL_PAIRS_PER_STEP):
            cs = slice(hp * LANES, (hp + 1) * LANES)
            q2 = q_ref[r, q_rows, cs]
            k2 = window(k_ref, r, n, first, cs)
            for hh in range(2):
                h = 2 * hp + hh
                qh = jnp.where(first_head if hh == 0 else jnp.logical_not(first_head), q2, 0)
                bias = bias_ref[h, :, blk:] if first else bias_ref[h]
                s_ref[slot, h, :, 0:kw] = _dot_nt(qh, k2) + bias

    def attend(i, slot, first):
        r, n = where_unit(i, first)
        kw = blk if first else 2 * blk
        t0 = n * (blk * dilation) + r
        rows = pl.ds(t0, blk) if dilation == 1 else pl.ds(t0, blk, stride=dilation)
        stat_tile = jnp.ones((blk, LANES), F32)
        for hp in range(DIL_PAIRS_PER_STEP):
            cs = slice(hp * LANES, (hp + 1) * LANES)
            v2 = window(v_ref, r, n, first, cs)
            accs = []
            for hh in range(2):
                s = s_ref[slot, 2 * hp + hh, :, 0:kw]
                m = jnp.max(s, axis=-1, keepdims=True)
                key_lane = lax.broadcasted_iota(jnp.int32, (kw, LANES), 1)
                own = key_lane < DIL_HEAD_DIM if hh == 0 else key_lane >= DIL_HEAD_DIM
                acc = _dot(jnp.exp2(s - m).astype(BF16), jnp.where(own, v2, 1))
                sum_lane = head0 + _stat_lane(2 * hp + hh)
                stat_tile = jnp.where(lane == sum_lane, acc, stat_tile)
                stat_tile = jnp.where(lane == sum_lane + STAT_MAX_SHIFT, m, stat_tile)
                accs.append(acc)
            acc_ref[hp, rows, :] = jnp.where(first_head, accs[0], accs[1])
        stat_ref[rows, :] = stat_tile

    def run(count, first):
        logits(0, 0, first)

        def body(j, carry):
            i = 2 * j
            logits(i + 1, 1, first)
            attend(i, 0, first)
            logits(jnp.minimum(i + 2, count - 1), 0, first)
            attend(i + 1, 1, first)
            return carry
        if count >= 2:
            lax.fori_loop(0, count // 2, body, 0)
        if count % 2:
            attend(count - 1, 0, first)

    run(dilation, True)
    if n_blocks > 1:
        run(dilation * (n_blocks - 1), False)
    for hp in range(DIL_PAIRS_PER_STEP):
        o_ref[:, hp * LANES:(hp + 1) * LANES] = acc_ref[hp].astype(BF16)


def _dil_attn(proj, bias, group):
    _, dilation = DIL_GROUPS[group]
    length = SEQ // dilation
    halves = DIL_HEADS // DIL_HEADS_PER_STEP
    width = DIL_HEADS_PER_STEP * DIL_HEAD_DIM
    qkv_spec = lambda kind: pl.BlockSpec(
        (None, dilation, length, width), lambda b, u: (b, 0, 0, kind * halves + u))
    return pl.pallas_call(
        functools.partial(_dil_attn_kernel, dilation=dilation, n_blocks=length // DIL_BLOCK),
        grid=(BATCH, halves),
        in_specs=[
            qkv_spec(0), qkv_spec(1), qkv_spec(2),
            pl.BlockSpec((DIL_HEADS_PER_STEP, DIL_BLOCK, 2 * DIL_BLOCK),
                         lambda b, u: (group * halves + u, 0, 0)),
        ],
        out_specs=[
            pl.BlockSpec((None, SEQ, width), lambda b, u: (b, 0, u)),
            pl.BlockSpec((None, SEQ, LANES), lambda b, u: (b, 0, u)),
        ],
        out_shape=[
            jax.ShapeDtypeStruct((BATCH, SEQ, DIL_WIDTH), BF16),
            jax.ShapeDtypeStruct((BATCH, SEQ, halves * LANES), F32),
        ],
        scratch_shapes=[pltpu.VMEM((2, DIL_HEADS_PER_STEP, DIL_BLOCK, 2 * DIL_BLOCK), F32),
                        pltpu.VMEM((DIL_PAIRS_PER_STEP, SEQ, LANES), F32)],
        compiler_params=_params("arbitrary", "arbitrary"),
        name=f"dil_attn_g{group}",
    )(proj, proj, proj, bias)


def _dil_out_kernel(x_ref, o0_ref, o1_ref, o2_ref, s0_ref, s1_ref, s2_ref, e_ref, wo_ref,
                    gate_ref, post_ref, out_ref):
    lane = lax.broadcasted_iota(jnp.int32, (1, LANES), 1)
    in_first_block = functools.reduce(jnp.logical_or, [
        jnp.logical_or(lane == _stat_lane(h), lane == _stat_lane(h) + STAT_MAX_SHIFT)
        for h in range(DIL_HEADS_PER_STEP)])
    sums = [jnp.where(in_first_block, r[:, :LANES], r[:, LANES:]) for r in (s0_ref, s1_ref, s2_ref)]
    maxes = [pltpu.roll(s, LANES - STAT_MAX_SHIFT, 1) for s in sums]
    m = jnp.maximum(jnp.maximum(maxes[0], maxes[1]), maxes[2])
    es = [jnp.exp2(mg - m) for mg in maxes]
    inv = 1.0 / (sums[0] * es[0] + sums[1] * es[1] + sums[2] * es[2])
    e = e_ref[...]
    mix = None
    for eg, o_ref in zip(es, (o0_ref, o1_ref, o2_ref)):
        alpha = eg * inv
        hi = alpha.astype(BF16)
        lo = (alpha - hi.astype(F32)).astype(BF16)
        term = _dot(jnp.concatenate([hi, lo], axis=1), e) * o_ref[...].astype(F32)
        mix = term if mix is None else mix + term
    y = _dot(mix.astype(BF16), wo_ref[...])
    out_ref[...] = x_ref[...] + gate_ref[...] * _rms(y, post_ref[...])


def _dil_out(x, outs, stats, mod, post, w_o, layer):
    tm = TM_OUT
    tiles_per_batch = SEQ // tm
    halves = DIL_HEADS // DIL_HEADS_PER_STEP
    stat_lane_of_col = jnp.array([_stat_lane(c // DIL_HEAD_DIM) for c in range(DIL_WIDTH)], jnp.int32)
    lane_of_row = jnp.arange(2 * LANES) % LANES
    expand = (lane_of_row[:, None] == stat_lane_of_col[None, :]).astype(BF16)
    batch_rows = lambda width: pl.BlockSpec(
        (None, tm, width), lambda t: (t // tiles_per_batch, t % tiles_per_batch, 0))
    tok = pl.BlockSpec((tm, D_MODEL), lambda t: (t, 0))
    return pl.pallas_call(
        _dil_out_kernel,
        grid=(TOKENS // tm,),
        in_specs=[
            tok, *([batch_rows(DIL_WIDTH)] * 3), *([batch_rows(halves * LANES)] * 3),
            pl.BlockSpec(expand.shape, lambda t: (0, 0)), pl.BlockSpec(w_o.shape, lambda t: (0, 0)),
            _mod_spec(layer, 1, 2, tm), _row_spec(layer * N_SUB + 1),
        ],
        out_specs=tok,
        out_shape=jax.ShapeDtypeStruct((TOKENS, D_MODEL), F32),
        compiler_params=_params("arbitrary"),
        name="dil_out",
    )(x, *outs, *stats, expand, w_o, mod, post)


def kernel(x, c, norm_pre, norm_post, w_mod, b_mod, ffn_w_gate, ffn_w_up, ffn_w_down,
           mla_w_in, mla_q_norm, mla_w_q_up, mla_kv_norm, mla_w_kv_up, mla_w_o,
           dil_w_in, dil_w_o, rel_bias):
    assert x.shape == (BATCH, SEQ, D_MODEL) and x.dtype == F32
    h = x.reshape(TOKENS, D_MODEL)
    mod = _modulation(c, w_mod, b_mod)
    pre = norm_pre.reshape(DEPTH * N_SUB, 1, D_MODEL)
    post = norm_post.reshape(DEPTH * N_SUB, 1, D_MODEL)
    wg, wu, wd = ffn_w_gate.astype(BF16), ffn_w_up.astype(BF16), ffn_w_down.astype(BF16)
    tables = _rope_tables()
    bias = _bias_tables(rel_bias)

    for layer in range(DEPTH):
        h = _ffn(h, mod, pre, post, wg, wu, wd, layer, 0)
        idx = layer // 2
        if layer % 2 == 0:
            win, wq, wk, wv = _mla_weights(mla_w_in[idx], mla_w_q_up[idx], mla_w_kv_up[idx])
            q, k, v = _mla_proj(h, mod, pre, win, mla_q_norm[idx][None, :], mla_kv_norm[idx][None, :],
                                wq, wk, wv, tables, layer)
            mla = (_mla_attn(q, k, v), mla_w_o[idx].astype(BF16))
        else:
            w_in = dil_w_in[idx].astype(BF16)
            outs, stats = zip(*[_dil_attn(_dil_proj(h, mod, pre, w_in, layer, g), bias, g)
                                for g in range(N_GROUPS)])
            h = _dil_out(h, outs, stats, mod, post, dil_w_o[idx].astype(BF16), layer)
            mla = None
        h = _ffn(h, mod, pre, post, wg, wu, wd, layer, 1, mla=mla)
    return h.reshape(BATCH, SEQ, D_MODEL)
```

```python
import functools
import math

import jax
import jax.numpy as jnp
from jax import lax
from jax.experimental import pallas as pl
from jax.experimental.pallas import tpu as pltpu

F32 = jnp.float32
BF16 = jnp.bfloat16

D_MODEL = 1024
BATCH = 8
SEQ = 2048
DEPTH = 2
N_SUB = 3
D_FF = 2816
FFN_RES = 0.5
EPS = 1e-6

MLA_HEADS = 16
Q_LORA = 384
KV_LORA = 256
QK_NOPE = 64
QK_ROPE = 32
V_HEAD = 64
ROPE_THETA = 10000.0

DIL_GROUPS = ((128, 1), (512, 4), (2048, 16))
N_GROUPS = 3
DIL_HEADS = 16
DIL_HEAD_DIM = 64
DIL_BLOCK = 128
DIL_WIDTH = DIL_HEADS * DIL_HEAD_DIM
N_BUCKETS = 32
MAX_DISTANCE = 2048

TOKENS = BATCH * SEQ
LANES = 128
HEAD_SLOT = 128
ROPE_HALF = QK_ROPE // 2
MASK_VALUE = -1e30
LOG2_E = math.log2(math.e)
VMEM_LIMIT = 56 * 1024 * 1024

TM_FFN = 1024
FFN_SUB = 512
TM_PROJ = 1024
PROJ_SUB = 256
TM_DIL_PROJ = 1024
TN_DIL_PROJ = 1024
TM_OUT = 512
TQ_MLA = 256
MLA_SUM_ROWS = 16
MLA_LOOKAHEAD = 6
TN_MOD = 1536
FF_CHUNK = 1536


def _params(*sem):
    return pltpu.CompilerParams(dimension_semantics=sem, vmem_limit_bytes=VMEM_LIMIT)


def _rms(x, g):
    ms = jnp.mean(x * x, axis=-1, keepdims=True)
    return x * lax.rsqrt(ms + EPS) * g


def _dot(a, b):
    return jnp.dot(a, b, preferred_element_type=F32)


def _dot_nt(a, b):
    return lax.dot_general(a, b, (((1,), (1,)), ((), ())), preferred_element_type=F32)


def _silu(x):
    return x * jax.nn.sigmoid(x)


def _mod_spec(layer, sub, kind, tm):
    tiles_per_batch = SEQ // tm
    return pl.BlockSpec(
        (None, 1, D_MODEL),
        lambda t, *_: ((layer * BATCH + t // tiles_per_batch) * (N_SUB * 3) + sub * 3 + kind, 0, 0))


def _row_spec(index):
    return pl.BlockSpec((None, 1, D_MODEL), lambda *_: (index, 0, 0))


def _mod_kernel(c_ref, w_ref, b_ref, o_ref):
    h = _silu(c_ref[...])
    hi = h.astype(BF16)
    lo = (h - hi.astype(F32)).astype(BF16)
    y = _dot(jnp.concatenate([hi, lo], axis=0), w_ref[...].astype(BF16))
    o_ref[...] = y[:BATCH] + y[BATCH:] + b_ref[...]


def _modulation(c, w_mod, b_mod):
    n = N_SUB * 3 * D_MODEL
    out = pl.pallas_call(
        _mod_kernel,
        grid=(DEPTH, n // TN_MOD),
        in_specs=[
            pl.BlockSpec((BATCH, D_MODEL), lambda i, j: (0, 0)),
            pl.BlockSpec((None, D_MODEL, TN_MOD), lambda i, j: (i, 0, j)),
            pl.BlockSpec((None, 1, TN_MOD), lambda i, j: (i, 0, j)),
        ],
        out_specs=pl.BlockSpec((None, BATCH, TN_MOD), lambda i, j: (i, 0, j)),
        out_shape=jax.ShapeDtypeStruct((DEPTH, BATCH, n), F32),
        compiler_params=_params("arbitrary", "arbitrary"),
        name="modulation",
    )(c, w_mod, b_mod.reshape(DEPTH, 1, n))
    return out.reshape(DEPTH * BATCH * N_SUB * 3, 1, D_MODEL)


def _mla_mix(rows, att_ref, wo_ref):
    return _dot(att_ref[rows, :], wo_ref[...])


def _ffn_kernel(x_ref, *refs, after_mla):
    if after_mla:
        att_ref, wo_ref, mgate_ref, mpost_ref, *refs = refs
    shift_ref, scale_ref, gate_ref, pre_ref, post_ref, wg_ref, wu_ref, wd_ref, o_ref, a_ref = refs
    tm = x_ref.shape[0]
    subs = [slice(r0, r0 + FFN_SUB) for r0 in range(0, tm, FFN_SUB)]
    hns = []
    for rows in subs:
        x = x_ref[rows, :]
        if after_mla:
            x = x + mgate_ref[...] * _rms(_mla_mix(rows, att_ref, wo_ref), mpost_ref[...])
            o_ref[rows, :] = x
        hns.append((_rms(x, pre_ref[...]) * (1.0 + scale_ref[...]) + shift_ref[...]).astype(BF16))
    for rows, hn in zip(subs, hns):
        for c0 in range(0, D_FF, FF_CHUNK):
            c1 = min(c0 + FF_CHUNK, D_FF)
            g = _dot(hn, wg_ref[:, c0:c1])
            u = _dot(hn, wu_ref[:, c0:c1])
            a_ref[rows, c0:c1] = (_silu(g) * u).astype(BF16)
    for rows in subs:
        y = _dot(a_ref[rows, :], wd_ref[...])
        x = o_ref[rows, :] if after_mla else x_ref[rows, :]
        o_ref[rows, :] = x + FFN_RES * gate_ref[...] * _rms(y, post_ref[...])


def _ffn(x, mod, pre, post, weights, layer, which, mla=None):
    wg, wu, wd, w_set = weights
    sub = 0 if which == 0 else 2
    tm = TM_FFN
    w_in_spec = pl.BlockSpec((None, D_MODEL, D_FF), lambda t: (w_set, 0, 0), pipeline_mode=pl.Buffered(1))
    w_out_spec = pl.BlockSpec((None, D_FF, D_MODEL), lambda t: (w_set, 0, 0), pipeline_mode=pl.Buffered(1))
    mla_args, mla_specs = [], []
    if mla is not None:
        att, w_o = mla
        mla_args = [att, w_o, mod, post]
        mla_specs = [pl.BlockSpec((tm, att.shape[1]), lambda t: (t, 0)),
                     pl.BlockSpec(w_o.shape, lambda t: (0, 0)),
                     _mod_spec(layer, 1, 2, tm), _row_spec(layer * N_SUB + 1)]
    return pl.pallas_call(
        functools.partial(_ffn_kernel, after_mla=mla is not None),
        grid=(TOKENS // tm,),
        in_specs=[
            pl.BlockSpec((tm, D_MODEL), lambda t: (t, 0)),
            *mla_specs,
            _mod_spec(layer, sub, 0, tm), _mod_spec(layer, sub, 1, tm), _mod_spec(layer, sub, 2, tm),
            _row_spec(layer * N_SUB + sub), _row_spec(layer * N_SUB + sub),
            w_in_spec, w_in_spec, w_out_spec,
        ],
        out_specs=pl.BlockSpec((tm, D_MODEL), lambda t: (t, 0)),
        out_shape=jax.ShapeDtypeStruct((TOKENS, D_MODEL), F32),
        scratch_shapes=[pltpu.VMEM((tm, D_FF), BF16)],
        compiler_params=_params("arbitrary"),
        name="ffn" if mla is None else "mla_out_ffn",
    )(x, *mla_args, mod, mod, mod, pre, post, wg, wu, wd)


def _rope(z, ta, tb):
    return z * ta + pltpu.roll(z, HEAD_SLOT // 2, 1) * tb


def _mla_proj_kernel(x_ref, shift_ref, scale_ref, pre_ref, win_ref, qn_ref, kvn_ref,
                     wq_ref, wk_ref, wv_ref, ta_ref, tb_ref, q_ref, k_ref, v_ref):
    tm = x_ref.shape[0]
    subs = [slice(r0, r0 + PROJ_SUB) for r0 in range(0, tm, PROJ_SUB)]
    lats = []
    for rows in subs:
        hn = (_rms(x_ref[rows, :], pre_ref[...]) * (1.0 + scale_ref[...]) + shift_ref[...]).astype(BF16)
        lats.append(_dot(hn, win_ref[...]))
    for rows, lat in zip(subs, lats):
        cq = _rms(lat[:, :Q_LORA], qn_ref[...]).astype(BF16)
        ckv = _rms(lat[:, Q_LORA:Q_LORA + KV_LORA], kvn_ref[...]).astype(BF16)
        ta, tb = ta_ref[rows, :], tb_ref[rows, :]
        kr = _rope(lat[:, Q_LORA + KV_LORA:], ta, tb)
        q = _dot(cq, wq_ref[...]) * ((QK_NOPE + QK_ROPE) ** -0.5 * LOG2_E)
        kn = _dot(ckv, wk_ref[...])
        v_ref[rows, :] = _dot(ckv, wv_ref[...]).astype(BF16)
        for h in range(MLA_HEADS):
            sl = slice(h * HEAD_SLOT, (h + 1) * HEAD_SLOT)
            q_ref[rows, sl] = _rope(q[:, sl], ta, tb).astype(BF16)
            k_ref[rows, sl] = (kn[:, sl] + kr).astype(BF16)


def _head_slot(nope, rope, like):
    split = HEAD_SLOT // 2 - ROPE_HALF
    zeros = lambda n: jnp.zeros(like.shape[:-1] + (n,), like.dtype)
    nope = zeros(QK_NOPE) if nope is None else nope
    rope = zeros(QK_ROPE) if rope is None else rope
    return jnp.concatenate([rope[..., :ROPE_HALF], nope[..., :split], rope[..., ROPE_HALF:], nope[..., split:],
                            zeros(HEAD_SLOT - QK_NOPE - QK_ROPE)], axis=-1)


def _rope_tables():
    pos = jnp.arange(SEQ, dtype=F32)
    freqs = ROPE_THETA ** (-jnp.arange(ROPE_HALF, dtype=F32) / ROPE_HALF)
    ang = pos[:, None] * freqs[None, :]
    cos, sin = jnp.cos(ang), jnp.sin(ang)
    ones = jnp.ones((SEQ, QK_NOPE), F32)
    ta = _head_slot(ones, jnp.concatenate([cos, cos], axis=1), cos)
    tb = _head_slot(None, jnp.concatenate([-sin, sin], axis=1), sin)
    return ta, tb


def _mla_weights(w_in, w_q_up, w_kv_up):
    win = jnp.concatenate([w_in[:, :Q_LORA + KV_LORA],
                           _head_slot(None, w_in[:, Q_LORA + KV_LORA:], w_in)], axis=1).astype(BF16)
    wq = w_q_up.reshape(Q_LORA, MLA_HEADS, QK_NOPE + QK_ROPE)
    wq = _head_slot(wq[..., :QK_NOPE], wq[..., QK_NOPE:], wq)
    wq = wq.reshape(Q_LORA, MLA_HEADS * HEAD_SLOT).astype(BF16)
    wkv = w_kv_up.reshape(KV_LORA, MLA_HEADS, QK_NOPE + V_HEAD)
    wk = _head_slot(wkv[..., :QK_NOPE], None, wkv).reshape(KV_LORA, MLA_HEADS * HEAD_SLOT).astype(BF16)
    wv = wkv[..., QK_NOPE:].reshape(KV_LORA, MLA_HEADS * V_HEAD).astype(BF16)
    return win, wq, wk, wv


def _mla_proj(x, mod, pre, win, qn, kvn, wq, wk, wv, tables, layer):
    tm = TM_PROJ
    tiles_per_batch = SEQ // tm
    full = lambda a: pl.BlockSpec(a.shape, lambda t: (0,) * a.ndim)
    tab_spec = pl.BlockSpec((tm, HEAD_SLOT), lambda t: (t % tiles_per_batch, 0))
    wide = MLA_HEADS * HEAD_SLOT
    return pl.pallas_call(
        _mla_proj_kernel,
        grid=(TOKENS // tm,),
        in_specs=[
            pl.BlockSpec((tm, D_MODEL), lambda t: (t, 0)),
            _mod_spec(layer, 1, 0, tm), _mod_spec(layer, 1, 1, tm),
            _row_spec(layer * N_SUB + 1),
            full(win), full(qn), full(kvn), full(wq), full(wk), full(wv),
            tab_spec, tab_spec,
        ],
        out_specs=[
            pl.BlockSpec((tm, wide), lambda t: (t, 0)),
            pl.BlockSpec((tm, wide), lambda t: (t, 0)),
            pl.BlockSpec((tm, MLA_HEADS * V_HEAD), lambda t: (t, 0)),
        ],
        out_shape=[
            jax.ShapeDtypeStruct((TOKENS, wide), BF16),
            jax.ShapeDtypeStruct((TOKENS, wide), BF16),
            jax.ShapeDtypeStruct((TOKENS, MLA_HEADS * V_HEAD), BF16),
        ],
        compiler_params=_params("arbitrary"),
        name="mla_proj",
    )(x, mod, mod, pre, win, qn, kvn, wq, wk, wv, *tables)


def _mla_attn_kernel(q_ref, k_ref, v_ref, *refs, n_cast):
    cast_in, (o_ref, *cast_out), (vt_ref, ot_ref, pv_ref) = (
        refs[:n_cast], refs[n_cast:2 * n_cast + 1], refs[2 * n_cast + 1:])
    tq = TQ_MLA
    key = lax.broadcasted_iota(jnp.int32, (tq, tq), 0)
    query = lax.broadcasted_iota(jnp.int32, (tq, tq), 1)
    causal = key <= query
    vt_ref[:2 * V_HEAD, :] = v_ref[...].astype(F32).T.astype(BF16)
    vt_ref[2 * V_HEAD:, :] = jnp.ones((MLA_SUM_ROWS, SEQ), BF16)

    def logits(hh, qi, c):
        hs = slice(hh * HEAD_SLOT, (hh + 1) * HEAD_SLOT)
        s = _dot_nt(k_ref[c * tq:(c + 1) * tq, hs], q_ref[qi * tq:(qi + 1) * tq, hs])
        return jnp.where(causal, s, MASK_VALUE) if c == qi else s

    chunks = [(hh, qi, c) for hh in range(2) for qi in range(SEQ // tq) for c in range(qi + 1)]
    queued = [logits(*ch) for ch in chunks[:MLA_LOOKAHEAD]]
    for i, (hh, qi, c) in enumerate(chunks):
        if i + MLA_LOOKAHEAD < len(chunks):
            queued.append(logits(*chunks[i + MLA_LOOKAHEAD]))
        s = queued.pop(0)
        m_chunk = jnp.max(s, axis=0, keepdims=True)
        if c == 0:
            m = m_chunk
            acc = _dot(vt_ref[:, 0:tq], jnp.exp2(s - m).astype(BF16))
        else:
            m_new = jnp.maximum(m, m_chunk)
            pv_ref[i % 2] = _dot(vt_ref[:, c * tq:(c + 1) * tq], jnp.exp2(s - m_new).astype(BF16))
            acc = acc * jnp.exp2(m - m_new) + pv_ref[i % 2]
            m = m_new
        if c == qi:
            rows = slice(hh * V_HEAD, (hh + 1) * V_HEAD)
            ot_ref[rows, qi * tq:(qi + 1) * tq] = acc[rows, :] * (1.0 / acc[2 * V_HEAD:2 * V_HEAD + 1, :])
    o_ref[...] = ot_ref[...].T.astype(BF16)
    for src_ref, dst_ref in zip(cast_in, cast_out):
        dst_ref[...] = src_ref[...].astype(BF16)


def _mla_attn(q, k, v, casts):
    pairs = MLA_HEADS // 2
    cast_args, cast_in_specs, cast_out_specs, cast_shapes = [], [], [], []
    for array, first_set, block_rows in casts:
        sets, rows, cols = array.shape
        first_block, n_blocks = first_set * rows // block_rows, (sets - first_set) * rows // block_rows
        assert first_set * rows % block_rows == 0 and rows % block_rows == 0 and n_blocks <= BATCH * pairs

        def block(b, p, first=0, n_blocks=n_blocks):
            return first + jnp.minimum(b * pairs + p, n_blocks - 1), 0
        cast_args.append(array.reshape(sets * rows, cols))
        cast_in_specs.append(pl.BlockSpec((block_rows, cols), functools.partial(block, first=first_block)))
        cast_out_specs.append(pl.BlockSpec((block_rows, cols), block))
        cast_shapes.append(jax.ShapeDtypeStruct(((sets - first_set) * rows, cols), BF16))
    att, *copies = pl.pallas_call(
        functools.partial(_mla_attn_kernel, n_cast=len(casts)),
        grid=(BATCH, pairs),
        in_specs=[
            pl.BlockSpec((SEQ, 2 * HEAD_SLOT), lambda b, p: (b, p)),
            pl.BlockSpec((SEQ, 2 * HEAD_SLOT), lambda b, p: (b, p)),
            pl.BlockSpec((SEQ, 2 * V_HEAD), lambda b, p: (b, p)),
            *cast_in_specs,
        ],
        out_specs=[pl.BlockSpec((SEQ, 2 * V_HEAD), lambda b, p: (b, p)), *cast_out_specs],
        out_shape=[jax.ShapeDtypeStruct((TOKENS, MLA_HEADS * V_HEAD), BF16), *cast_shapes],
        scratch_shapes=[pltpu.VMEM((2 * V_HEAD + MLA_SUM_ROWS, SEQ), BF16),
                        pltpu.VMEM((2 * V_HEAD, SEQ), F32),
                        pltpu.VMEM((2, 2 * V_HEAD + MLA_SUM_ROWS, TQ_MLA), F32)],
        compiler_params=_params("arbitrary", "arbitrary"),
        name="mla_attn",
    )(q, k, v, *cast_args)
    return att, [copy.reshape(-1, *array.shape[1:]) for copy, (array, _, _) in zip(copies, casts)]


def _t5_bucket(dist):
    max_exact = N_BUCKETS // 2
    d = jnp.maximum(dist, 1).astype(F32)
    large = max_exact + (jnp.log(d / max_exact) / math.log(MAX_DISTANCE / max_exact)
                         * (N_BUCKETS - max_exact)).astype(jnp.int32)
    large = jnp.minimum(large, N_BUCKETS - 1)
    return jnp.where(dist < max_exact, dist, large)


def _bias_selectors():
    j = jnp.arange(2 * DIL_BLOCK)
    rel = DIL_BLOCK - j
    sel = []
    for window, dilation in DIL_GROUPS:
        assert window // dilation == DIL_BLOCK
        bucket = _t5_bucket(jnp.maximum(rel, 0) * dilation)
        onehot = (bucket[None, :] == jnp.arange(N_BUCKETS)[:, None]) & (rel >= 0)[None, :]
        sel.append(onehot.astype(F32))
    return jnp.stack(sel)


def _bias_kernel(rbt_ref, sel_ref, o_ref):
    rbt = rbt_ref[...]
    sel = sel_ref[...]
    row0 = jnp.zeros((DIL_HEADS, 2 * DIL_BLOCK), F32)
    for b in range(N_BUCKETS):
        row0 = row0 + rbt[:, b:b + 1] * sel[b:b + 1, :]
    future = jnp.sum(sel, axis=0, keepdims=True) < 0.5
    row0 = jnp.where(future, MASK_VALUE, row0 * LOG2_E)
    for h in range(DIL_HEADS):
        t = jnp.broadcast_to(row0[h:h + 1, :], (DIL_BLOCK, 2 * DIL_BLOCK))
        o_ref[h] = pltpu.roll(t, 0, 1, stride=1, stride_axis=0)


def _bias_tables(rel_bias):
    rbt = rel_bias.T.reshape(N_GROUPS, DIL_HEADS, N_BUCKETS)
    return pl.pallas_call(
        _bias_kernel,
        grid=(N_GROUPS,),
        in_specs=[
            pl.BlockSpec((None, DIL_HEADS, N_BUCKETS), lambda g: (g, 0, 0)),
            pl.BlockSpec((None, N_BUCKETS, 2 * DIL_BLOCK), lambda g: (g, 0, 0)),
        ],
        out_specs=pl.BlockSpec((DIL_HEADS, DIL_BLOCK, 2 * DIL_BLOCK), lambda g: (g, 0, 0)),
        out_shape=jax.ShapeDtypeStruct((N_GROUPS * DIL_HEADS, DIL_BLOCK, 2 * DIL_BLOCK), F32),
        compiler_params=_params("arbitrary"),
        name="dil_bias",
    )(rbt, _bias_selectors())


def _dil_proj_kernel(x_ref, shift_ref, scale_ref, pre_ref, w_ref, o_ref, *scratch, dilation):
    tm = x_ref.shape[0]
    per = PROJ_SUB // dilation
    slabs = D_MODEL // LANES
    hbs = []
    for i, r0 in enumerate(range(0, tm, PROJ_SUB)):
        hn = _rms(x_ref[r0:r0 + PROJ_SUB, :], pre_ref[...]) * (1.0 + scale_ref[...]) + shift_ref[...]
        if dilation == 1:
            hbs.append(hn.astype(BF16))
            continue
        slab_ref, = scratch
        for s in range(slabs):
            slab_ref[i, s] = hn[:, s * LANES:(s + 1) * LANES]
        hbs.append(jnp.concatenate(
            [jnp.concatenate([slab_ref[i, s, pl.ds(r, per, stride=dilation), :] for s in range(slabs)],
                             axis=1).astype(BF16)
             for r in range(dilation)], axis=0))
    for i, hb in enumerate(hbs):
        for j in range(3):
            cols = slice(j * DIL_WIDTH, (j + 1) * DIL_WIDTH)
            y = _dot(hb, w_ref[:, cols])
            if j == 0:
                y = y * (DIL_HEAD_DIM ** -0.5 * LOG2_E)
            y = y.astype(BF16)
            for r in range(dilation):
                o_ref[r, i * per:(i + 1) * per, cols] = y[r * per:(r + 1) * per, :]


def _dil_proj(x, mod, pre, w, layer, group):
    _, dilation = DIL_GROUPS[group]
    tm = TM_DIL_PROJ
    tiles_per_batch = SEQ // tm
    sub = tm // dilation
    scratch = [] if dilation == 1 else [
        pltpu.VMEM((tm // PROJ_SUB, D_MODEL // LANES, PROJ_SUB, LANES), F32)]
    return pl.pallas_call(
        functools.partial(_dil_proj_kernel, dilation=dilation),
        grid=(TOKENS // tm,),
        in_specs=[
            pl.BlockSpec((tm, D_MODEL), lambda t: (t, 0)),
            _mod_spec(layer, 1, 0, tm), _mod_spec(layer, 1, 1, tm),
            _row_spec(layer * N_SUB + 1),
            pl.BlockSpec((D_MODEL, 3 * DIL_WIDTH), lambda t: (0, group), pipeline_mode=pl.Buffered(1)),
        ],
        out_specs=pl.BlockSpec((None, dilation, sub, 3 * DIL_WIDTH),
                               lambda t: (t // tiles_per_batch, 0, t % tiles_per_batch, 0)),
        out_shape=jax.ShapeDtypeStruct((BATCH, dilation, SEQ // dilation, 3 * DIL_WIDTH), BF16),
        scratch_shapes=scratch,
        compiler_params=_params("arbitrary"),
        name=f"dil_proj_g{group}",
    )(x, mod, mod, pre, w)


DIL_HEADS_PER_STEP = 8
DIL_PAIRS_PER_STEP = DIL_HEADS_PER_STEP // 2


STAT_MAX_SHIFT = 32


def _stat_lane(head):
    return head if head % 2 else DIL_HEAD_DIM + head


def _dil_attn_kernel(q_ref, k_ref, v_ref, bias_ref, o_ref, stat_ref, s_ref, acc_ref, *, dilation, n_blocks):
    blk = DIL_BLOCK
    lane = lax.broadcasted_iota(jnp.int32, (blk, LANES), 1)
    first_head = lane < DIL_HEAD_DIM
    head0 = pl.program_id(1) * DIL_HEADS_PER_STEP

    def where_unit(i, first):
        if first:
            return i, 0
        later = n_blocks - 1
        if dilation == 1:
            return 0, i + 1
        i = jnp.asarray(i, jnp.int32)
        return lax.div(i, jnp.int32(later)), lax.rem(i, jnp.int32(later)) + 1

    def window(ref, r, n, first, cs):
        if first:
            return ref[r, 0:blk, cs]
        return ref[r, pl.ds(pl.multiple_of(n * blk - blk, blk), 2 * blk), cs]

    def logits(i, slot, first):
        r, n = where_unit(i, first)
        kw = blk if first else 2 * blk
        q_rows = slice(0, blk) if first else pl.ds(pl.multiple_of(n * blk, blk), blk)
        for hp in range(DIL_PAIRS_PER_STEP):
            cs = slice(hp * LANES, (hp + 1) * LANES)
            q2 = q_ref[r, q_rows, cs]
            k2 = window(k_ref, r, n, first, cs)
            for hh in range(2):
                h = 2 * hp + hh
                qh = jnp.where(first_head if hh == 0 else jnp.logical_not(first_head), q2, 0)
                bias = bias_ref[h, :, blk:] if first else bias_ref[h]
                s_ref[slot, h, :, 0:kw] = _dot_nt(qh, k2) + bias

    def attend(i, slot, first):
        r, n = where_unit(i, first)
        kw = blk if first else 2 * blk
        t0 = n * (blk * dilation) + r
        rows = pl.ds(t0, blk) if dilation == 1 else pl.ds(t0, blk, stride=dilation)
        stat_tile = jnp.ones((blk, LANES), F32)
        for hp in range(DIL_PAIRS_PER_STEP):
            cs = slice(hp * LANES, (hp + 1) * LANES)
            v2 = window(v_ref, r, n, first, cs)
            accs = []
            for hh in range(2):
                s = s_ref[slot, 2 * hp + hh, :, 0:kw]
                m = jnp.max(s, axis=-1, keepdims=True)
                key_lane = lax.broadcasted_iota(jnp.int32, (kw, LANES), 1)
                own = key_lane < DIL_HEAD_DIM if hh == 0 else key_lane >= DIL_HEAD_DIM
                acc = _dot(jnp.exp2(s - m).astype(BF16), jnp.where(own, v2, 1))
                sum_lane = head0 + _stat_lane(2 * hp + hh)
                stat_tile = jnp.where(lane == sum_lane, acc, stat_tile)
                stat_tile = jnp.where(lane == sum_lane + STAT_MAX_SHIFT, m, stat_tile)
                accs.append(acc)
            acc_ref[hp, rows, :] = jnp.where(first_head, accs[0], accs[1])
        stat_ref[rows, :] = stat_tile

    def run(count, first):
        logits(0, 0, first)

        def body(j, carry):
            i = 2 * j
            logits(i + 1, 1, first)
            attend(i, 0, first)
            logits(jnp.minimum(i + 2, count - 1), 0, first)
            attend(i + 1, 1, first)
            return carry
        if count >= 2:
            lax.fori_loop(0, count // 2, body, 0)
        if count % 2:
            attend(count - 1, 0, first)

    run(dilation, True)
    if n_blocks > 1:
        run(dilation * (n_blocks - 1), False)
    for hp in range(DIL_PAIRS_PER_STEP):
        o_ref[:, hp * LANES:(hp + 1) * LANES] = acc_ref[hp].astype(BF16)


def _dil_attn(proj, bias, group):
    _, dilation = DIL_GROUPS[group]
    length = SEQ // dilation
    halves = DIL_HEADS // DIL_HEADS_PER_STEP
    width = DIL_HEADS_PER_STEP * DIL_HEAD_DIM
    qkv_spec = lambda kind: pl.BlockSpec(
        (None, dilation, length, width), lambda b, u: (b, 0, 0, kind * halves + u))
    return pl.pallas_call(
        functools.partial(_dil_attn_kernel, dilation=dilation, n_blocks=length // DIL_BLOCK),
        grid=(BATCH, halves),
        in_specs=[
            qkv_spec(0), qkv_spec(1), qkv_spec(2),
            pl.BlockSpec((DIL_HEADS_PER_STEP, DIL_BLOCK, 2 * DIL_BLOCK),
                         lambda b, u: (group * halves + u, 0, 0)),
        ],
        out_specs=[
            pl.BlockSpec((None, SEQ, width), lambda b, u: (b, 0, u)),
            pl.BlockSpec((None, SEQ, LANES), lambda b, u: (b, 0, u)),
        ],
        out_shape=[
            jax.ShapeDtypeStruct((BATCH, SEQ, DIL_WIDTH), BF16),
            jax.ShapeDtypeStruct((BATCH, SEQ, halves * LANES), F32),
        ],
        scratch_shapes=[pltpu.VMEM((2, DIL_HEADS_PER_STEP, DIL_BLOCK, 2 * DIL_BLOCK), F32),
                        pltpu.VMEM((DIL_PAIRS_PER_STEP, SEQ, LANES), F32)],
        compiler_params=_params("arbitrary", "arbitrary"),
        name=f"dil_attn_g{group}",
    )(proj, proj, proj, bias)


def _dil_out_kernel(x_ref, o0_ref, o1_ref, o2_ref, s0_ref, s1_ref, s2_ref, e_ref, wo_ref,
                    gate_ref, post_ref, out_ref):
    lane = lax.broadcasted_iota(jnp.int32, (1, LANES), 1)
    in_first_block = functools.reduce(jnp.logical_or, [
        jnp.logical_or(lane == _stat_lane(h), lane == _stat_lane(h) + STAT_MAX_SHIFT)
        for h in range(DIL_HEADS_PER_STEP)])
    sums = [jnp.where(in_first_block, r[:, :LANES], r[:, LANES:]) for r in (s0_ref, s1_ref, s2_ref)]
    maxes = [pltpu.roll(s, LANES - STAT_MAX_SHIFT, 1) for s in sums]
    m = jnp.maximum(jnp.maximum(maxes[0], maxes[1]), maxes[2])
    es = [jnp.exp2(mg - m) for mg in maxes]
    inv = 1.0 / (sums[0] * es[0] + sums[1] * es[1] + sums[2] * es[2])
    e = e_ref[...]
    mix = None
    for eg, o_ref in zip(es, (o0_ref, o1_ref, o2_ref)):
        alpha = eg * inv
        hi = alpha.astype(BF16)
        lo = (alpha - hi.astype(F32)).astype(BF16)
        term = _dot(jnp.concatenate([hi, lo], axis=1), e) * o_ref[...].astype(F32)
        mix = term if mix is None else mix + term
    y = _dot(mix.astype(BF16), wo_ref[...])
    out_ref[...] = x_ref[...] + gate_ref[...] * _rms(y, post_ref[...])


def _dil_out(x, outs, stats, mod, post, w_o, layer):
    tm = TM_OUT
    tiles_per_batch = SEQ // tm
    halves = DIL_HEADS // DIL_HEADS_PER_STEP
    stat_lane_of_col = jnp.array([_stat_lane(c // DIL_HEAD_DIM) for c in range(DIL_WIDTH)], jnp.int32)
    lane_of_row = jnp.arange(2 * LANES) % LANES
    expand = (lane_of_row[:, None] == stat_lane_of_col[None, :]).astype(BF16)
    batch_rows = lambda width: pl.BlockSpec(
        (None, tm, width), lambda t: (t // tiles_per_batch, t % tiles_per_batch, 0))
    tok = pl.BlockSpec((tm, D_MODEL), lambda t: (t, 0))
    return pl.pallas_call(
        _dil_out_kernel,
        grid=(TOKENS // tm,),
        in_specs=[
            tok, *([batch_rows(DIL_WIDTH)] * 3), *([batch_rows(halves * LANES)] * 3),
            pl.BlockSpec(expand.shape, lambda t: (0, 0)), pl.BlockSpec(w_o.shape, lambda t: (0, 0)),
            _mod_spec(layer, 1, 2, tm), _row_spec(layer * N_SUB + 1),
        ],
        out_specs=tok,
        out_shape=jax.ShapeDtypeStruct((TOKENS, D_MODEL), F32),
        compiler_params=_params("arbitrary"),
        name="dil_out",
    )(x, *outs, *stats, expand, w_o, mod, post)


def kernel(x, c, norm_pre, norm_post, w_mod, b_mod, ffn_w_gate, ffn_w_up, ffn_w_down,
           mla_w_in, mla_q_norm, mla_w_q_up, mla_kv_norm, mla_w_kv_up, mla_w_o,
           dil_w_in, dil_w_o, rel_bias):
    assert x.shape == (BATCH, SEQ, D_MODEL) and x.dtype == F32
    assert DEPTH == 2
    h = x.reshape(TOKENS, D_MODEL)
    mod = _modulation(c, w_mod, b_mod)
    pre = norm_pre.reshape(DEPTH * N_SUB, 1, D_MODEL)
    post = norm_post.reshape(DEPTH * N_SUB, 1, D_MODEL)
    tables = _rope_tables()
    bias = _bias_tables(rel_bias)
    ffn_f32 = [w.reshape(DEPTH * 2, *w.shape[2:]) for w in (ffn_w_gate, ffn_w_up, ffn_w_down)]

    h = _ffn(h, mod, pre, post, (*[w[:1].astype(BF16) for w in ffn_f32], 0), 0, 0)
    win, wq, wk, wv = _mla_weights(mla_w_in[0], mla_w_q_up[0], mla_w_kv_up[0])
    q, k, v = _mla_proj(h, mod, pre, win, mla_q_norm[0][None, :], mla_kv_norm[0][None, :],
                        wq, wk, wv, tables, 0)
    casts = [(ffn_f32[0], 1, 64), (ffn_f32[1], 1, 64), (ffn_f32[2], 1, 256), (dil_w_in, 0, 16)]
    att, (wg, wu, wd, dil_in) = _mla_attn(q, k, v, casts)
    h = _ffn(h, mod, pre, post, (wg, wu, wd, 0), 0, 1, mla=(att, mla_w_o[0].astype(BF16)))

    h = _ffn(h, mod, pre, post, (wg, wu, wd, 1), 1, 0)
    outs, stats = zip(*[_dil_attn(_dil_proj(h, mod, pre, dil_in[0], 1, g), bias, g) for g in range(N_GROUPS)])
    h = _dil_out(h, outs, stats, mod, post, dil_w_o[0].astype(BF16), 1)
    h = _ffn(h, mod, pre, post, (wg, wu, wd, 2), 1, 1)
    return h.reshape(BATCH, SEQ, D_MODEL)
```

```python
import functools
import math

import jax
import jax.numpy as jnp
import numpy as np
from jax import lax
from jax.experimental import pallas as pl
from jax.experimental.pallas import tpu as pltpu

F32 = jnp.float32
BF16 = jnp.bfloat16

D_MODEL = 1024
BATCH = 8
SEQ = 2048
DEPTH = 2
N_SUB = 3
D_FF = 2816
FFN_RES = 0.5
EPS = 1e-6

MLA_HEADS = 16
Q_LORA = 384
KV_LORA = 256
QK_NOPE = 64
QK_ROPE = 32
V_HEAD = 64
ROPE_THETA = 10000.0

DIL_GROUPS = ((128, 1), (512, 4), (2048, 16))
N_GROUPS = 3
DIL_HEADS = 16
DIL_HEAD_DIM = 64
DIL_BLOCK = 128
DIL_WIDTH = DIL_HEADS * DIL_HEAD_DIM
N_BUCKETS = 32
MAX_DISTANCE = 2048

TOKENS = BATCH * SEQ
LANES = 128
HEAD_SLOT = 128
ROPE_HALF = QK_ROPE // 2
MASK_VALUE = -1e30
LOG2_E = math.log2(math.e)
VMEM_LIMIT = 56 * 1024 * 1024

TM_FFN = 1024
FFN_SUB = 512
TM_PROJ = 1024
PROJ_SUB = 256
TM_DIL_PROJ = 1024
TN_DIL_PROJ = 1024
TM_OUT = 512
TQ_MLA = 256
MLA_SUM_ROWS = 16
MLA_LOOKAHEAD = 6
TN_MOD = 1536
FF_CHUNK = 1536


def _params(*sem):
    return pltpu.CompilerParams(dimension_semantics=sem, vmem_limit_bytes=VMEM_LIMIT)


def _rms(x, g):
    ms = jnp.mean(x * x, axis=-1, keepdims=True)
    return x * lax.rsqrt(ms + EPS) * g


def _dot(a, b):
    return jnp.dot(a, b, preferred_element_type=F32)


def _dot_nt(a, b):
    return lax.dot_general(a, b, (((1,), (1,)), ((), ())), preferred_element_type=F32)


def _silu(x):
    return x * jax.nn.sigmoid(x)


def _mod_spec(layer, sub, kind, tm):
    tiles_per_batch = SEQ // tm
    return pl.BlockSpec(
        (None, 1, D_MODEL),
        lambda t, *_: ((layer * BATCH + t // tiles_per_batch) * (N_SUB * 3) + sub * 3 + kind, 0, 0))


def _row_spec(index):
    return pl.BlockSpec((None, 1, D_MODEL), lambda *_: (index, 0, 0))


def _mod_kernel(c_ref, w_ref, b_ref, o_ref):
    h = _silu(c_ref[...])
    hi = h.astype(BF16)
    lo = (h - hi.astype(F32)).astype(BF16)
    y = _dot(jnp.concatenate([hi, lo], axis=0), w_ref[...].astype(BF16))
    o_ref[...] = y[:BATCH] + y[BATCH:] + b_ref[...]


def _modulation(c, w_mod, b_mod):
    n = N_SUB * 3 * D_MODEL
    out = pl.pallas_call(
        _mod_kernel,
        grid=(DEPTH, n // TN_MOD),
        in_specs=[
            pl.BlockSpec((BATCH, D_MODEL), lambda i, j: (0, 0)),
            pl.BlockSpec((None, D_MODEL, TN_MOD), lambda i, j: (i, 0, j)),
            pl.BlockSpec((None, 1, TN_MOD), lambda i, j: (i, 0, j)),
        ],
        out_specs=pl.BlockSpec((None, BATCH, TN_MOD), lambda i, j: (i, 0, j)),
        out_shape=jax.ShapeDtypeStruct((DEPTH, BATCH, n), F32),
        compiler_params=_params("arbitrary", "arbitrary"),
        name="modulation",
    )(c, w_mod, b_mod.reshape(DEPTH, 1, n))
    return out.reshape(DEPTH * BATCH * N_SUB * 3, 1, D_MODEL)


def _mla_mix(rows, att_ref, wo_ref):
    return _dot(att_ref[rows, :], wo_ref[...])


def _ffn_kernel(x_ref, *refs, after_mla):
    if after_mla:
        att_ref, wo_ref, mgate_ref, mpost_ref, *refs = refs
    shift_ref, scale_ref, gate_ref, pre_ref, post_ref, wg_ref, wu_ref, wd_ref, o_ref, a_ref = refs
    tm = x_ref.shape[0]
    subs = [slice(r0, r0 + FFN_SUB) for r0 in range(0, tm, FFN_SUB)]
    hns = []
    for rows in subs:
        x = x_ref[rows, :]
        if after_mla:
            x = x + mgate_ref[...] * _rms(_mla_mix(rows, att_ref, wo_ref), mpost_ref[...])
            o_ref[rows, :] = x
        hns.append((_rms(x, pre_ref[...]) * (1.0 + scale_ref[...]) + shift_ref[...]).astype(BF16))
    for rows, hn in zip(subs, hns):
        for c0 in range(0, D_FF, FF_CHUNK):
            c1 = min(c0 + FF_CHUNK, D_FF)
            g = _dot(hn, wg_ref[:, c0:c1])
            u = _dot(hn, wu_ref[:, c0:c1])
            a_ref[rows, c0:c1] = (_silu(g) * u).astype(BF16)
    for rows in subs:
        y = _dot(a_ref[rows, :], wd_ref[...])
        x = o_ref[rows, :] if after_mla else x_ref[rows, :]
        o_ref[rows, :] = x + FFN_RES * gate_ref[...] * _rms(y, post_ref[...])


def _ffn(x, mod, pre, post, weights, layer, which, mla=None):
    wg, wu, wd, w_set = weights
    sub = 0 if which == 0 else 2
    tm = TM_FFN
    w_in_spec = pl.BlockSpec((None, D_MODEL, D_FF), lambda t: (w_set, 0, 0), pipeline_mode=pl.Buffered(1))
    w_out_spec = pl.BlockSpec((None, D_FF, D_MODEL), lambda t: (w_set, 0, 0), pipeline_mode=pl.Buffered(1))
    mla_args, mla_specs = [], []
    if mla is not None:
        att, w_o = mla
        mla_args = [att, w_o, mod, post]
        mla_specs = [pl.BlockSpec((tm, att.shape[1]), lambda t: (t, 0)),
                     pl.BlockSpec(w_o.shape, lambda t: (0, 0)),
                     _mod_spec(layer, 1, 2, tm), _row_spec(layer * N_SUB + 1)]
    return pl.pallas_call(
        functools.partial(_ffn_kernel, after_mla=mla is not None),
        grid=(TOKENS // tm,),
        in_specs=[
            pl.BlockSpec((tm, D_MODEL), lambda t: (t, 0)),
            *mla_specs,
            _mod_spec(layer, sub, 0, tm), _mod_spec(layer, sub, 1, tm), _mod_spec(layer, sub, 2, tm),
            _row_spec(layer * N_SUB + sub), _row_spec(layer * N_SUB + sub),
            w_in_spec, w_in_spec, w_out_spec,
        ],
        out_specs=pl.BlockSpec((tm, D_MODEL), lambda t: (t, 0)),
        out_shape=jax.ShapeDtypeStruct((TOKENS, D_MODEL), F32),
        scratch_shapes=[pltpu.VMEM((tm, D_FF), BF16)],
        compiler_params=_params("arbitrary"),
        name="ffn" if mla is None else "mla_out_ffn",
    )(x, *mla_args, mod, mod, mod, pre, post, wg, wu, wd)


def _rope(z, ta, tb):
    return z * ta + pltpu.roll(z, HEAD_SLOT // 2, 1) * tb


def _mla_proj_kernel(x_ref, shift_ref, scale_ref, pre_ref, win_ref, qn_ref, kvn_ref,
                     wq_ref, wk_ref, wv_ref, ta_ref, tb_ref, q_ref, k_ref, v_ref):
    tm = x_ref.shape[0]
    subs = [slice(r0, r0 + PROJ_SUB) for r0 in range(0, tm, PROJ_SUB)]
    lats = []
    for rows in subs:
        hn = (_rms(x_ref[rows, :], pre_ref[...]) * (1.0 + scale_ref[...]) + shift_ref[...]).astype(BF16)
        lats.append(_dot(hn, win_ref[...]))
    for rows, lat in zip(subs, lats):
        cq = _rms(lat[:, :Q_LORA], qn_ref[...]).astype(BF16)
        ckv = _rms(lat[:, Q_LORA:Q_LORA + KV_LORA], kvn_ref[...]).astype(BF16)
        ta, tb = ta_ref[rows, :], tb_ref[rows, :]
        kr = _rope(lat[:, Q_LORA + KV_LORA:], ta, tb)
        q = _dot(cq, wq_ref[...]) * ((QK_NOPE + QK_ROPE) ** -0.5 * LOG2_E)
        kn = _dot(ckv, wk_ref[...])
        v_ref[rows, :] = _dot(ckv, wv_ref[...]).astype(BF16)
        for h in range(MLA_HEADS):
            sl = slice(h * HEAD_SLOT, (h + 1) * HEAD_SLOT)
            q_ref[rows, sl] = _rope(q[:, sl], ta, tb).astype(BF16)
            k_ref[rows, sl] = (kn[:, sl] + kr).astype(BF16)


def _head_slot(nope, rope, like):
    split = HEAD_SLOT // 2 - ROPE_HALF
    zeros = lambda n: jnp.zeros(like.shape[:-1] + (n,), like.dtype)
    nope = zeros(QK_NOPE) if nope is None else nope
    rope = zeros(QK_ROPE) if rope is None else rope
    return jnp.concatenate([rope[..., :ROPE_HALF], nope[..., :split], rope[..., ROPE_HALF:], nope[..., split:],
                            zeros(HEAD_SLOT - QK_NOPE - QK_ROPE)], axis=-1)


def _rope_tables():
    pos = jnp.arange(SEQ, dtype=F32)
    freqs = ROPE_THETA ** (-jnp.arange(ROPE_HALF, dtype=F32) / ROPE_HALF)
    ang = pos[:, None] * freqs[None, :]
    cos, sin = jnp.cos(ang), jnp.sin(ang)
    ones = jnp.ones((SEQ, QK_NOPE), F32)
    ta = _head_slot(ones, jnp.concatenate([cos, cos], axis=1), cos)
    tb = _head_slot(None, jnp.concatenate([-sin, sin], axis=1), sin)
    return ta, tb


def _mla_weights(w_in, w_q_up, w_kv_up):
    win = jnp.concatenate([w_in[:, :Q_LORA + KV_LORA],
                           _head_slot(None, w_in[:, Q_LORA + KV_LORA:], w_in)], axis=1).astype(BF16)
    wq = w_q_up.reshape(Q_LORA, MLA_HEADS, QK_NOPE + QK_ROPE)
    wq = _head_slot(wq[..., :QK_NOPE], wq[..., QK_NOPE:], wq)
    wq = wq.reshape(Q_LORA, MLA_HEADS * HEAD_SLOT).astype(BF16)
    wkv = w_kv_up.reshape(KV_LORA, MLA_HEADS, QK_NOPE + V_HEAD)
    wk = _head_slot(wkv[..., :QK_NOPE], None, wkv).reshape(KV_LORA, MLA_HEADS * HEAD_SLOT).astype(BF16)
    wv = wkv[..., QK_NOPE:].reshape(KV_LORA, MLA_HEADS * V_HEAD).astype(BF16)
    return win, wq, wk, wv


def _mla_proj(x, mod, pre, win, qn, kvn, wq, wk, wv, tables, layer):
    tm = TM_PROJ
    tiles_per_batch = SEQ // tm
    full = lambda a: pl.BlockSpec(a.shape, lambda t: (0,) * a.ndim)
    tab_spec = pl.BlockSpec((tm, HEAD_SLOT), lambda t: (t % tiles_per_batch, 0))
    wide = MLA_HEADS * HEAD_SLOT
    return pl.pallas_call(
        _mla_proj_kernel,
        grid=(TOKENS // tm,),
        in_specs=[
            pl.BlockSpec((tm, D_MODEL), lambda t: (t, 0)),
            _mod_spec(layer, 1, 0, tm), _mod_spec(layer, 1, 1, tm),
            _row_spec(layer * N_SUB + 1),
            full(win), full(qn), full(kvn), full(wq), full(wk), full(wv),
            tab_spec, tab_spec,
        ],
        out_specs=[
            pl.BlockSpec((tm, wide), lambda t: (t, 0)),
            pl.BlockSpec((tm, wide), lambda t: (t, 0)),
            pl.BlockSpec((tm, MLA_HEADS * V_HEAD), lambda t: (t, 0)),
        ],
        out_shape=[
            jax.ShapeDtypeStruct((TOKENS, wide), BF16),
            jax.ShapeDtypeStruct((TOKENS, wide), BF16),
            jax.ShapeDtypeStruct((TOKENS, MLA_HEADS * V_HEAD), BF16),
        ],
        compiler_params=_params("arbitrary"),
        name="mla_proj",
    )(x, mod, mod, pre, win, qn, kvn, wq, wk, wv, *tables)


def _mla_attn_kernel(q_ref, k_ref, v_ref, *refs, n_cast):
    cast_in, (o_ref, *cast_out), (vt_ref, ot_ref, pv_ref) = (
        refs[:n_cast], refs[n_cast:2 * n_cast + 1], refs[2 * n_cast + 1:])
    tq = TQ_MLA
    key = lax.broadcasted_iota(jnp.int32, (tq, tq), 0)
    query = lax.broadcasted_iota(jnp.int32, (tq, tq), 1)
    causal = key <= query
    vt_ref[:2 * V_HEAD, :] = v_ref[...].astype(F32).T.astype(BF16)
    vt_ref[2 * V_HEAD:, :] = jnp.ones((MLA_SUM_ROWS, SEQ), BF16)

    def logits(hh, qi, c):
        hs = slice(hh * HEAD_SLOT, (hh + 1) * HEAD_SLOT)
        s = _dot_nt(k_ref[c * tq:(c + 1) * tq, hs], q_ref[qi * tq:(qi + 1) * tq, hs])
        return jnp.where(causal, s, MASK_VALUE) if c == qi else s

    chunks = [(hh, qi, c) for hh in range(2) for qi in range(SEQ // tq) for c in range(qi + 1)]
    queued = [logits(*ch) for ch in chunks[:MLA_LOOKAHEAD]]
    for i, (hh, qi, c) in enumerate(chunks):
        if i + MLA_LOOKAHEAD < len(chunks):
            queued.append(logits(*chunks[i + MLA_LOOKAHEAD]))
        s = queued.pop(0)
        m_chunk = jnp.max(s, axis=0, keepdims=True)
        if c == 0:
            m = m_chunk
            acc = _dot(vt_ref[:, 0:tq], jnp.exp2(s - m).astype(BF16))
        else:
            m_new = jnp.maximum(m, m_chunk)
            pv_ref[i % 2] = _dot(vt_ref[:, c * tq:(c + 1) * tq], jnp.exp2(s - m_new).astype(BF16))
            acc = acc * jnp.exp2(m - m_new) + pv_ref[i % 2]
            m = m_new
        if c == qi:
            rows = slice(hh * V_HEAD, (hh + 1) * V_HEAD)
            ot_ref[rows, qi * tq:(qi + 1) * tq] = acc[rows, :] * (1.0 / acc[2 * V_HEAD:2 * V_HEAD + 1, :])
    o_ref[...] = ot_ref[...].T.astype(BF16)
    for src_ref, dst_ref in zip(cast_in, cast_out):
        dst_ref[...] = src_ref[...].astype(BF16)


def _mla_attn(q, k, v, casts):
    pairs = MLA_HEADS // 2
    cast_args, cast_in_specs, cast_out_specs, cast_shapes = [], [], [], []
    for array, first_set, block_rows in casts:
        sets, rows, cols = array.shape
        first_block, n_blocks = first_set * rows // block_rows, (sets - first_set) * rows // block_rows
        assert first_set * rows % block_rows == 0 and rows % block_rows == 0 and n_blocks <= BATCH * pairs

        def block(b, p, first=0, n_blocks=n_blocks):
            return first + jnp.minimum(b * pairs + p, n_blocks - 1), 0
        cast_args.append(array.reshape(sets * rows, cols))
        cast_in_specs.append(pl.BlockSpec((block_rows, cols), functools.partial(block, first=first_block)))
        cast_out_specs.append(pl.BlockSpec((block_rows, cols), block))
        cast_shapes.append(jax.ShapeDtypeStruct(((sets - first_set) * rows, cols), BF16))
    att, *copies = pl.pallas_call(
        functools.partial(_mla_attn_kernel, n_cast=len(casts)),
        grid=(BATCH, pairs),
        in_specs=[
            pl.BlockSpec((SEQ, 2 * HEAD_SLOT), lambda b, p: (b, p)),
            pl.BlockSpec((SEQ, 2 * HEAD_SLOT), lambda b, p: (b, p)),
            pl.BlockSpec((SEQ, 2 * V_HEAD), lambda b, p: (b, p)),
            *cast_in_specs,
        ],
        out_specs=[pl.BlockSpec((SEQ, 2 * V_HEAD), lambda b, p: (b, p)), *cast_out_specs],
        out_shape=[jax.ShapeDtypeStruct((TOKENS, MLA_HEADS * V_HEAD), BF16), *cast_shapes],
        scratch_shapes=[pltpu.VMEM((2 * V_HEAD + MLA_SUM_ROWS, SEQ), BF16),
                        pltpu.VMEM((2 * V_HEAD, SEQ), F32),
                        pltpu.VMEM((2, 2 * V_HEAD + MLA_SUM_ROWS, TQ_MLA), F32)],
        compiler_params=_params("arbitrary", "arbitrary"),
        name="mla_attn",
    )(q, k, v, *cast_args)
    return att, [copy.reshape(-1, *array.shape[1:]) for copy, (array, _, _) in zip(copies, casts)]


def _t5_bucket(dist):
    max_exact = N_BUCKETS // 2
    d = np.maximum(dist, 1).astype(np.float32)
    scale = np.float32(math.log(MAX_DISTANCE / max_exact))
    large = max_exact + (np.log(d / np.float32(max_exact)) / scale
                         * np.float32(N_BUCKETS - max_exact)).astype(np.int32)
    large = np.minimum(large, N_BUCKETS - 1)
    return np.where(dist < max_exact, dist, large)


def _bias_selectors():
    j = np.arange(2 * DIL_BLOCK)
    rel = DIL_BLOCK - j
    sel = []
    for window, dilation in DIL_GROUPS:
        assert window // dilation == DIL_BLOCK
        bucket = _t5_bucket(np.maximum(rel, 0) * dilation)
        onehot = (bucket[None, :] == np.arange(N_BUCKETS)[:, None]) & (rel >= 0)[None, :]
        sel.append(onehot.astype(np.float32))
    return jnp.asarray(np.stack(sel))


def _bias_kernel(rbt_ref, sel_ref, o_ref):
    rbt = rbt_ref[...]
    sel = sel_ref[...]
    row0 = jnp.zeros((DIL_HEADS, 2 * DIL_BLOCK), F32)
    for b in range(N_BUCKETS):
        row0 = row0 + rbt[:, b:b + 1] * sel[b:b + 1, :]
    future = jnp.sum(sel, axis=0, keepdims=True) < 0.5
    row0 = jnp.where(future, MASK_VALUE, row0 * LOG2_E)
    for h in range(DIL_HEADS):
        t = jnp.broadcast_to(row0[h:h + 1, :], (DIL_BLOCK, 2 * DIL_BLOCK))
        o_ref[h] = pltpu.roll(t, 0, 1, stride=1, stride_axis=0)


def _bias_tables(rel_bias):
    rbt = rel_bias.T.reshape(N_GROUPS, DIL_HEADS, N_BUCKETS)
    return pl.pallas_call(
        _bias_kernel,
        grid=(N_GROUPS,),
        in_specs=[
            pl.BlockSpec((None, DIL_HEADS, N_BUCKETS), lambda g: (g, 0, 0)),
            pl.BlockSpec((None, N_BUCKETS, 2 * DIL_BLOCK), lambda g: (g, 0, 0)),
        ],
        out_specs=pl.BlockSpec((DIL_HEADS, DIL_BLOCK, 2 * DIL_BLOCK), lambda g: (g, 0, 0)),
        out_shape=jax.ShapeDtypeStruct((N_GROUPS * DIL_HEADS, DIL_BLOCK, 2 * DIL_BLOCK), F32),
        compiler_params=_params("arbitrary"),
        name="dil_bias",
    )(rbt, _bias_selectors())


def _dil_proj_kernel(x_ref, shift_ref, scale_ref, pre_ref, w_ref, o_ref, *scratch, dilation):
    tm = x_ref.shape[0]
    per = PROJ_SUB // dilation
    slabs = D_MODEL // LANES
    hbs = []
    for i, r0 in enumerate(range(0, tm, PROJ_SUB)):
        hn = _rms(x_ref[r0:r0 + PROJ_SUB, :], pre_ref[...]) * (1.0 + scale_ref[...]) + shift_ref[...]
        if dilation == 1:
            hbs.append(hn.astype(BF16))
            continue
        slab_ref, = scratch
        for s in range(slabs):
            slab_ref[i, s] = hn[:, s * LANES:(s + 1) * LANES]
        hbs.append(jnp.concatenate(
            [jnp.concatenate([slab_ref[i, s, pl.ds(r, per, stride=dilation), :] for s in range(slabs)],
                             axis=1).astype(BF16)
             for r in range(dilation)], axis=0))
    for i, hb in enumerate(hbs):
        for j in range(3):
            cols = slice(j * DIL_WIDTH, (j + 1) * DIL_WIDTH)
            y = _dot(hb, w_ref[:, cols])
            if j == 0:
                y = y * (DIL_HEAD_DIM ** -0.5 * LOG2_E)
            y = y.astype(BF16)
            for r in range(dilation):
                o_ref[r, i * per:(i + 1) * per, cols] = y[r * per:(r + 1) * per, :]


def _dil_proj(x, mod, pre, w, layer, group):
    _, dilation = DIL_GROUPS[group]
    tm = TM_DIL_PROJ
    tiles_per_batch = SEQ // tm
    sub = tm // dilation
    scratch = [] if dilation == 1 else [
        pltpu.VMEM((tm // PROJ_SUB, D_MODEL // LANES, PROJ_SUB, LANES), F32)]
    return pl.pallas_call(
        functools.partial(_dil_proj_kernel, dilation=dilation),
        grid=(TOKENS // tm,),
        in_specs=[
            pl.BlockSpec((tm, D_MODEL), lambda t: (t, 0)),
            _mod_spec(layer, 1, 0, tm), _mod_spec(layer, 1, 1, tm),
            _row_spec(layer * N_SUB + 1),
            pl.BlockSpec((D_MODEL, 3 * DIL_WIDTH), lambda t: (0, group), pipeline_mode=pl.Buffered(1)),
        ],
        out_specs=pl.BlockSpec((None, dilation, sub, 3 * DIL_WIDTH),
                               lambda t: (t // tiles_per_batch, 0, t % tiles_per_batch, 0)),
        out_shape=jax.ShapeDtypeStruct((BATCH, dilation, SEQ // dilation, 3 * DIL_WIDTH), BF16),
        scratch_shapes=scratch,
        compiler_params=_params("arbitrary"),
        name=f"dil_proj_g{group}",
    )(x, mod, mod, pre, w)


DIL_HEADS_PER_STEP = 8
DIL_PAIRS_PER_STEP = DIL_HEADS_PER_STEP // 2
DIL_UNITS_PER_TRIP = 4


STAT_MAX_SHIFT = 32


def _stat_lane(head):
    return head if head % 2 else DIL_HEAD_DIM + head


def _dil_attn_kernel(q_ref, k_ref, v_ref, bias_ref, o_ref, stat_ref, s_ref, acc_ref, *, dilation, n_blocks):
    blk = DIL_BLOCK
    lane = lax.broadcasted_iota(jnp.int32, (blk, LANES), 1)
    first_head = lane < DIL_HEAD_DIM
    head0 = pl.program_id(1) * DIL_HEADS_PER_STEP

    def where_unit(i, first):
        if first:
            return i, 0
        later = n_blocks - 1
        if dilation == 1:
            return 0, i + 1
        i = jnp.asarray(i, jnp.int32)
        return lax.div(i, jnp.int32(later)), lax.rem(i, jnp.int32(later)) + 1

    def window(ref, r, n, first, cs):
        if first:
            return ref[r, 0:blk, cs]
        return ref[r, pl.ds(pl.multiple_of(n * blk - blk, blk), 2 * blk), cs]

    def logits(i, slot, first):
        r, n = where_unit(i, first)
        kw = blk if first else 2 * blk
        q_rows = slice(0, blk) if first else pl.ds(pl.multiple_of(n * blk, blk), blk)
        for hp in range(DIL_PAIRS_PER_STEP):
            cs = slice(hp * LANES, (hp + 1) * LANES)
            q2 = q_ref[r, q_rows, cs]
            k2 = window(k_ref, r, n, first, cs)
            for hh in range(2):
                h = 2 * hp + hh
                qh = jnp.where(first_head if hh == 0 else jnp.logical_not(first_head), q2, 0)
                bias = bias_ref[h, :, blk:] if first else bias_ref[h]
                s_ref[slot, h, :, 0:kw] = _dot_nt(qh, k2) + bias

    def attend(i, slot, first):
        r, n = where_unit(i, first)
        kw = blk if first else 2 * blk
        t0 = n * (blk * dilation) + r
        rows = pl.ds(t0, blk) if dilation == 1 else pl.ds(t0, blk, stride=dilation)
        stat_tile = jnp.ones((blk, LANES), F32)
        for hp in range(DIL_PAIRS_PER_STEP):
            cs = slice(hp * LANES, (hp + 1) * LANES)
            v2 = window(v_ref, r, n, first, cs)
            accs = []
            for hh in range(2):
                s = s_ref[slot, 2 * hp + hh, :, 0:kw]
                m = jnp.max(s, axis=-1, keepdims=True)
                key_lane = lax.broadcasted_iota(jnp.int32, (kw, LANES), 1)
                own = key_lane < DIL_HEAD_DIM if hh == 0 else key_lane >= DIL_HEAD_DIM
                acc = _dot(jnp.exp2(s - m).astype(BF16), jnp.where(own, v2, 1))
                sum_lane = head0 + _stat_lane(2 * hp + hh)
                stat_tile = jnp.where(lane == sum_lane, acc, stat_tile)
                stat_tile = jnp.where(lane == sum_lane + STAT_MAX_SHIFT, m, stat_tile)
                accs.append(acc)
            acc_ref[hp, rows, :] = jnp.where(first_head, accs[0], accs[1])
        stat_ref[rows, :] = stat_tile

    def run(count, first):
        trips, rest = divmod(count, DIL_UNITS_PER_TRIP)

        def units(base, n, more):
            for u in range(n):
                if u + 1 < n or more:
                    logits(base + u + 1, (u + 1) % 2, first)
                attend(base + u, u % 2, first)

        logits(0, 0, first)
        if trips:
            def body(j, carry):
                units(j * DIL_UNITS_PER_TRIP, DIL_UNITS_PER_TRIP, True)
                return carry
            lax.fori_loop(0, trips - (0 if rest else 1), body, 0)
            if not rest:
                units((trips - 1) * DIL_UNITS_PER_TRIP, DIL_UNITS_PER_TRIP, False)
        if rest:
            units(trips * DIL_UNITS_PER_TRIP, rest, False)

    run(dilation, True)
    if n_blocks > 1:
        run(dilation * (n_blocks - 1), False)
    for hp in range(DIL_PAIRS_PER_STEP):
        o_ref[:, hp * LANES:(hp + 1) * LANES] = acc_ref[hp].astype(BF16)


def _dil_attn(proj, bias, group):
    _, dilation = DIL_GROUPS[group]
    length = SEQ // dilation
    halves = DIL_HEADS // DIL_HEADS_PER_STEP
    width = DIL_HEADS_PER_STEP * DIL_HEAD_DIM
    qkv_spec = lambda kind: pl.BlockSpec(
        (None, dilation, length, width), lambda b, u: (b, 0, 0, kind * halves + u))
    return pl.pallas_call(
        functools.partial(_dil_attn_kernel, dilation=dilation, n_blocks=length // DIL_BLOCK),
        grid=(BATCH, halves),
        in_specs=[
            qkv_spec(0), qkv_spec(1), qkv_spec(2),
            pl.BlockSpec((DIL_HEADS_PER_STEP, DIL_BLOCK, 2 * DIL_BLOCK),
                         lambda b, u: (group * halves + u, 0, 0)),
        ],
        out_specs=[
            pl.BlockSpec((None, SEQ, width), lambda b, u: (b, 0, u)),
            pl.BlockSpec((None, SEQ, LANES), lambda b, u: (b, 0, u)),
        ],
        out_shape=[
            jax.ShapeDtypeStruct((BATCH, SEQ, DIL_WIDTH), BF16),
            jax.ShapeDtypeStruct((BATCH, SEQ, halves * LANES), F32),
        ],
        scratch_shapes=[pltpu.VMEM((2, DIL_HEADS_PER_STEP, DIL_BLOCK, 2 * DIL_BLOCK), F32),
                        pltpu.VMEM((DIL_PAIRS_PER_STEP, SEQ, LANES), F32)],
        compiler_params=_params("arbitrary", "arbitrary"),
        name=f"dil_attn_g{group}",
    )(proj, proj, proj, bias)


def _dil_out_kernel(x_ref, o0_ref, o1_ref, o2_ref, s0_ref, s1_ref, s2_ref, e_ref, wo_ref,
                    gate_ref, post_ref, out_ref):
    lane = lax.broadcasted_iota(jnp.int32, (1, LANES), 1)
    in_first_block = functools.reduce(jnp.logical_or, [
        jnp.logical_or(lane == _stat_lane(h), lane == _stat_lane(h) + STAT_MAX_SHIFT)
        for h in range(DIL_HEADS_PER_STEP)])
    sums = [jnp.where(in_first_block, r[:, :LANES], r[:, LANES:]) for r in (s0_ref, s1_ref, s2_ref)]
    maxes = [pltpu.roll(s, LANES - STAT_MAX_SHIFT, 1) for s in sums]
    m = jnp.maximum(jnp.maximum(maxes[0], maxes[1]), maxes[2])
    es = [jnp.exp2(mg - m) for mg in maxes]
    inv = 1.0 / (sums[0] * es[0] + sums[1] * es[1] + sums[2] * es[2])
    e = e_ref[...]
    mix = None
    for eg, o_ref in zip(es, (o0_ref, o1_ref, o2_ref)):
        alpha = eg * inv
        hi = alpha.astype(BF16)
        lo = (alpha - hi.astype(F32)).astype(BF16)
        term = _dot(jnp.concatenate([hi, lo], axis=1), e) * o_ref[...].astype(F32)
        mix = term if mix is None else mix + term
    y = _dot(mix.astype(BF16), wo_ref[...])
    out_ref[...] = x_ref[...] + gate_ref[...] * _rms(y, post_ref[...])


def _dil_out(x, outs, stats, mod, post, w_o, layer):
    tm = TM_OUT
    tiles_per_batch = SEQ // tm
    halves = DIL_HEADS // DIL_HEADS_PER_STEP
    stat_lane_of_col = jnp.array([_stat_lane(c // DIL_HEAD_DIM) for c in range(DIL_WIDTH)], jnp.int32)
    lane_of_row = jnp.arange(2 * LANES) % LANES
    expand = (lane_of_row[:, None] == stat_lane_of_col[None, :]).astype(BF16)
    batch_rows = lambda width: pl.BlockSpec(
        (None, tm, width), lambda t: (t // tiles_per_batch, t % tiles_per_batch, 0))
    tok = pl.BlockSpec((tm, D_MODEL), lambda t: (t, 0))
    return pl.pallas_call(
        _dil_out_kernel,
        grid=(TOKENS // tm,),
        in_specs=[
            tok, *([batch_rows(DIL_WIDTH)] * 3), *([batch_rows(halves * LANES)] * 3),
            pl.BlockSpec(expand.shape, lambda t: (0, 0)), pl.BlockSpec(w_o.shape, lambda t: (0, 0)),
            _mod_spec(layer, 1, 2, tm), _row_spec(layer * N_SUB + 1),
        ],
        out_specs=tok,
        out_shape=jax.ShapeDtypeStruct((TOKENS, D_MODEL), F32),
        compiler_params=_params("arbitrary"),
        name="dil_out",
    )(x, *outs, *stats, expand, w_o, mod, post)


def kernel(x, c, norm_pre, norm_post, w_mod, b_mod, ffn_w_gate, ffn_w_up, ffn_w_down,
           mla_w_in, mla_q_norm, mla_w_q_up, mla_kv_norm, mla_w_kv_up, mla_w_o,
           dil_w_in, dil_w_o, rel_bias):
    assert x.shape == (BATCH, SEQ, D_MODEL) and x.dtype == F32
    assert DEPTH == 2
    h = x.reshape(TOKENS, D_MODEL)
    mod = _modulation(c, w_mod, b_mod)
    pre = norm_pre.reshape(DEPTH * N_SUB, 1, D_MODEL)
    post = norm_post.reshape(DEPTH * N_SUB, 1, D_MODEL)
    tables = _rope_tables()
    bias = _bias_tables(rel_bias)
    ffn_f32 = [w.reshape(DEPTH * 2, *w.shape[2:]) for w in (ffn_w_gate, ffn_w_up, ffn_w_down)]

    h = _ffn(h, mod, pre, post, (*[w[:1].astype(BF16) for w in ffn_f32], 0), 0, 0)
    win, wq, wk, wv = _mla_weights(mla_w_in[0], mla_w_q_up[0], mla_w_kv_up[0])
    q, k, v = _mla_proj(h, mod, pre, win, mla_q_norm[0][None, :], mla_kv_norm[0][None, :],
                        wq, wk, wv, tables, 0)
    casts = [(ffn_f32[0], 1, 64), (ffn_f32[1], 1, 64), (ffn_f32[2], 1, 256), (dil_w_in, 0, 16)]
    att, (wg, wu, wd, dil_in) = _mla_attn(q, k, v, casts)
    h = _ffn(h, mod, pre, post, (wg, wu, wd, 0), 0, 1, mla=(att, mla_w_o[0].astype(BF16)))

    h = _ffn(h, mod, pre, post, (wg, wu, wd, 1), 1, 0)
    outs, stats = zip(*[_dil_attn(_dil_proj(h, mod, pre, dil_in[0], 1, g), bias, g) for g in range(N_GROUPS)])
    h = _dil_out(h, outs, stats, mod, post, dil_w_o[0].astype(BF16), 1)
    h = _ffn(h, mod, pre, post, (wg, wu, wd, 2), 1, 1)
    return h.reshape(BATCH, SEQ, D_MODEL)
```

```python
import functools
import math

import jax
import jax.numpy as jnp
import numpy as np
from jax import lax
from jax.experimental import pallas as pl
from jax.experimental.pallas import tpu as pltpu

F32 = jnp.float32
BF16 = jnp.bfloat16

D_MODEL = 1024
BATCH = 8
SEQ = 2048
DEPTH = 2
N_SUB = 3
D_FF = 2816
FFN_RES = 0.5
EPS = 1e-6

MLA_HEADS = 16
Q_LORA = 384
KV_LORA = 256
QK_NOPE = 64
QK_ROPE = 32
V_HEAD = 64
ROPE_THETA = 10000.0

DIL_GROUPS = ((128, 1), (512, 4), (2048, 16))
N_GROUPS = 3
DIL_HEADS = 16
DIL_HEAD_DIM = 64
DIL_BLOCK = 128
DIL_WIDTH = DIL_HEADS * DIL_HEAD_DIM
N_BUCKETS = 32
MAX_DISTANCE = 2048

TOKENS = BATCH * SEQ
LANES = 128
HEAD_SLOT = 128
ROPE_HALF = QK_ROPE // 2
MASK_VALUE = -1e30
LOG2_E = math.log2(math.e)
VMEM_LIMIT = 56 * 1024 * 1024

TM_FFN = 1024
FFN_SUB = 256
TM_PROJ = 1024
PROJ_SUB = 256
TM_DIL_PROJ = 1024
TM_OUT = 1024
OUT_SUB = 256
TQ_MLA = 256
MLA_SUM_ROWS = 16
MLA_LOOKAHEAD = 6
TN_MOD = 1536
FF_CHUNK = 1536


def _params(*sem):
    return pltpu.CompilerParams(dimension_semantics=sem, vmem_limit_bytes=VMEM_LIMIT)


def _rms(x, g):
    ms = jnp.mean(x * x, axis=-1, keepdims=True)
    return x * lax.rsqrt(ms + EPS) * g


def _dot(a, b):
    return jnp.dot(a, b, preferred_element_type=F32)


def _dot_nt(a, b):
    return lax.dot_general(a, b, (((1,), (1,)), ((), ())), preferred_element_type=F32)


def _silu(x):
    return x * jax.nn.sigmoid(x)


def _mod_spec(layer, sub, kind, tm):
    tiles_per_batch = SEQ // tm
    return pl.BlockSpec(
        (None, 1, D_MODEL),
        lambda t, *_: ((layer * BATCH + t // tiles_per_batch) * (N_SUB * 3) + sub * 3 + kind, 0, 0))


def _row_spec(index):
    return pl.BlockSpec((None, 1, D_MODEL), lambda *_: (index, 0, 0))


def _mod_kernel(c_ref, w_ref, b_ref, o_ref):
    h = _silu(c_ref[...])
    hi = h.astype(BF16)
    lo = (h - hi.astype(F32)).astype(BF16)
    y = _dot(jnp.concatenate([hi, lo], axis=0), w_ref[...].astype(BF16))
    o_ref[...] = y[:BATCH] + y[BATCH:] + b_ref[...]


def _modulation(c, w_mod, b_mod):
    n = N_SUB * 3 * D_MODEL
    out = pl.pallas_call(
        _mod_kernel,
        grid=(DEPTH, n // TN_MOD),
        in_specs=[
            pl.BlockSpec((BATCH, D_MODEL), lambda i, j: (0, 0)),
            pl.BlockSpec((None, D_MODEL, TN_MOD), lambda i, j: (i, 0, j)),
            pl.BlockSpec((None, 1, TN_MOD), lambda i, j: (i, 0, j)),
        ],
        out_specs=pl.BlockSpec((None, BATCH, TN_MOD), lambda i, j: (i, 0, j)),
        out_shape=jax.ShapeDtypeStruct((DEPTH, BATCH, n), F32),
        compiler_params=_params("arbitrary", "arbitrary"),
        name="modulation",
    )(c, w_mod, b_mod.reshape(DEPTH, 1, n))
    return out.reshape(DEPTH * BATCH * N_SUB * 3, 1, D_MODEL)


def _mla_mix(rows, att_ref, wo_ref):
    return _dot(att_ref[rows, :], wo_ref[...])


def _ffn_kernel(x_ref, *refs, after_mla):
    if after_mla:
        att_ref, wo_ref, mgate_ref, mpost_ref, *refs = refs
    shift_ref, scale_ref, gate_ref, pre_ref, post_ref, wg_ref, wu_ref, wd_ref, o_ref, a_ref = refs
    tm = x_ref.shape[0]
    subs = [slice(r0, r0 + FFN_SUB) for r0 in range(0, tm, FFN_SUB)]
    hns = []
    for rows in subs:
        x = x_ref[rows, :]
        if after_mla:
            x = x + mgate_ref[...] * _rms(_mla_mix(rows, att_ref, wo_ref), mpost_ref[...])
            o_ref[rows, :] = x
        hns.append((_rms(x, pre_ref[...]) * (1.0 + scale_ref[...]) + shift_ref[...]).astype(BF16))
    for rows, hn in zip(subs, hns):
        for c0 in range(0, D_FF, FF_CHUNK):
            c1 = min(c0 + FF_CHUNK, D_FF)
            g = _dot(hn, wg_ref[:, c0:c1])
            u = _dot(hn, wu_ref[:, c0:c1])
            a_ref[rows, c0:c1] = (_silu(g) * u).astype(BF16)
    for rows in subs:
        y = _dot(a_ref[rows, :], wd_ref[...])
        x = o_ref[rows, :] if after_mla else x_ref[rows, :]
        o_ref[rows, :] = x + FFN_RES * gate_ref[...] * _rms(y, post_ref[...])


def _ffn(x, mod, pre, post, weights, layer, which, mla=None):
    wg, wu, wd, w_set = weights
    sub = 0 if which == 0 else 2
    tm = TM_FFN
    w_in_spec = pl.BlockSpec((None, D_MODEL, D_FF), lambda t: (w_set, 0, 0), pipeline_mode=pl.Buffered(1))
    w_out_spec = pl.BlockSpec((None, D_FF, D_MODEL), lambda t: (w_set, 0, 0), pipeline_mode=pl.Buffered(1))
    mla_args, mla_specs = [], []
    if mla is not None:
        att, w_o = mla
        mla_args = [att, w_o, mod, post]
        mla_specs = [pl.BlockSpec((tm, att.shape[1]), lambda t: (t, 0)),
                     pl.BlockSpec(w_o.shape, lambda t: (0, 0)),
                     _mod_spec(layer, 1, 2, tm), _row_spec(layer * N_SUB + 1)]
    return pl.pallas_call(
        functools.partial(_ffn_kernel, after_mla=mla is not None),
        grid=(TOKENS // tm,),
        in_specs=[
            pl.BlockSpec((tm, D_MODEL), lambda t: (t, 0)),
            *mla_specs,
            _mod_spec(layer, sub, 0, tm), _mod_spec(layer, sub, 1, tm), _mod_spec(layer, sub, 2, tm),
            _row_spec(layer * N_SUB + sub), _row_spec(layer * N_SUB + sub),
            w_in_spec, w_in_spec, w_out_spec,
        ],
        out_specs=pl.BlockSpec((tm, D_MODEL), lambda t: (t, 0)),
        out_shape=jax.ShapeDtypeStruct((TOKENS, D_MODEL), F32),
        scratch_shapes=[pltpu.VMEM((tm, D_FF), BF16)],
        compiler_params=_params("arbitrary"),
        name="ffn" if mla is None else "mla_out_ffn",
    )(x, *mla_args, mod, mod, mod, pre, post, wg, wu, wd)


def _rope(z, ta, tb):
    return z * ta + pltpu.roll(z, HEAD_SLOT // 2, 1) * tb


def _mla_proj_kernel(x_ref, shift_ref, scale_ref, pre_ref, win_ref, qn_ref, kvn_ref,
                     wq_ref, wk_ref, wv_ref, ta_ref, tb_ref, q_ref, k_ref, v_ref):
    tm = x_ref.shape[0]
    subs = [slice(r0, r0 + PROJ_SUB) for r0 in range(0, tm, PROJ_SUB)]
    lats = []
    for rows in subs:
        hn = (_rms(x_ref[rows, :], pre_ref[...]) * (1.0 + scale_ref[...]) + shift_ref[...]).astype(BF16)
        lats.append(_dot(hn, win_ref[...]))
    for rows, lat in zip(subs, lats):
        cq = _rms(lat[:, :Q_LORA], qn_ref[...]).astype(BF16)
        ckv = _rms(lat[:, Q_LORA:Q_LORA + KV_LORA], kvn_ref[...]).astype(BF16)
        ta, tb = ta_ref[rows, :], tb_ref[rows, :]
        kr = _rope(lat[:, Q_LORA + KV_LORA:], ta, tb)
        q = _dot(cq, wq_ref[...]) * ((QK_NOPE + QK_ROPE) ** -0.5 * LOG2_E)
        kn = _dot(ckv, wk_ref[...])
        v_ref[rows, :] = _dot(ckv, wv_ref[...]).astype(BF16)
        for h in range(MLA_HEADS):
            sl = slice(h * HEAD_SLOT, (h + 1) * HEAD_SLOT)
            q_ref[rows, sl] = _rope(q[:, sl], ta, tb).astype(BF16)
            k_ref[rows, sl] = (kn[:, sl] + kr).astype(BF16)


def _head_slot(nope, rope, like):
    split = HEAD_SLOT // 2 - ROPE_HALF
    zeros = lambda n: jnp.zeros(like.shape[:-1] + (n,), like.dtype)
    nope = zeros(QK_NOPE) if nope is None else nope
    rope = zeros(QK_ROPE) if rope is None else rope
    return jnp.concatenate([rope[..., :ROPE_HALF], nope[..., :split], rope[..., ROPE_HALF:], nope[..., split:],
                            zeros(HEAD_SLOT - QK_NOPE - QK_ROPE)], axis=-1)


def _rope_tables():
    pos = jnp.arange(SEQ, dtype=F32)
    freqs = ROPE_THETA ** (-jnp.arange(ROPE_HALF, dtype=F32) / ROPE_HALF)
    ang = pos[:, None] * freqs[None, :]
    cos, sin = jnp.cos(ang), jnp.sin(ang)
    ones = jnp.ones((SEQ, QK_NOPE), F32)
    ta = _head_slot(ones, jnp.concatenate([cos, cos], axis=1), cos)
    tb = _head_slot(None, jnp.concatenate([-sin, sin], axis=1), sin)
    return ta, tb


def _mla_weights(w_in, w_q_up, w_kv_up):
    win = jnp.concatenate([w_in[:, :Q_LORA + KV_LORA],
                           _head_slot(None, w_in[:, Q_LORA + KV_LORA:], w_in)], axis=1).astype(BF16)
    wq = w_q_up.reshape(Q_LORA, MLA_HEADS, QK_NOPE + QK_ROPE)
    wq = _head_slot(wq[..., :QK_NOPE], wq[..., QK_NOPE:], wq)
    wq = wq.reshape(Q_LORA, MLA_HEADS * HEAD_SLOT).astype(BF16)
    wkv = w_kv_up.reshape(KV_LORA, MLA_HEADS, QK_NOPE + V_HEAD)
    wk = _head_slot(wkv[..., :QK_NOPE], None, wkv).reshape(KV_LORA, MLA_HEADS * HEAD_SLOT).astype(BF16)
    wv = wkv[..., QK_NOPE:].reshape(KV_LORA, MLA_HEADS * V_HEAD).astype(BF16)
    return win, wq, wk, wv


def _mla_proj(x, mod, pre, win, qn, kvn, wq, wk, wv, tables, layer):
    tm = TM_PROJ
    tiles_per_batch = SEQ // tm
    full = lambda a: pl.BlockSpec(a.shape, lambda t: (0,) * a.ndim)
    tab_spec = pl.BlockSpec((tm, HEAD_SLOT), lambda t: (t % tiles_per_batch, 0))
    wide = MLA_HEADS * HEAD_SLOT
    return pl.pallas_call(
        _mla_proj_kernel,
        grid=(TOKENS // tm,),
        in_specs=[
            pl.BlockSpec((tm, D_MODEL), lambda t: (t, 0)),
            _mod_spec(layer, 1, 0, tm), _mod_spec(layer, 1, 1, tm),
            _row_spec(layer * N_SUB + 1),
            full(win), full(qn), full(kvn), full(wq), full(wk), full(wv),
            tab_spec, tab_spec,
        ],
        out_specs=[
            pl.BlockSpec((tm, wide), lambda t: (t, 0)),
            pl.BlockSpec((tm, wide), lambda t: (t, 0)),
            pl.BlockSpec((tm, MLA_HEADS * V_HEAD), lambda t: (t, 0)),
        ],
        out_shape=[
            jax.ShapeDtypeStruct((TOKENS, wide), BF16),
            jax.ShapeDtypeStruct((TOKENS, wide), BF16),
            jax.ShapeDtypeStruct((TOKENS, MLA_HEADS * V_HEAD), BF16),
        ],
        compiler_params=_params("arbitrary"),
        name="mla_proj",
    )(x, mod, mod, pre, win, qn, kvn, wq, wk, wv, *tables)


def _mla_attn_kernel(q_ref, k_ref, v_ref, *refs, n_cast):
    cast_in, (o_ref, *cast_out), (vt_ref, ot_ref, pv_ref) = (
        refs[:n_cast], refs[n_cast:2 * n_cast + 1], refs[2 * n_cast + 1:])
    tq = TQ_MLA
    key = lax.broadcasted_iota(jnp.int32, (tq, tq), 0)
    query = lax.broadcasted_iota(jnp.int32, (tq, tq), 1)
    causal = key <= query
    vt_ref[:2 * V_HEAD, :] = v_ref[...].astype(F32).T.astype(BF16)
    vt_ref[2 * V_HEAD:, :] = jnp.ones((MLA_SUM_ROWS, SEQ), BF16)

    def logits(hh, qi, c):
        hs = slice(hh * HEAD_SLOT, (hh + 1) * HEAD_SLOT)
        s = _dot_nt(k_ref[c * tq:(c + 1) * tq, hs], q_ref[qi * tq:(qi + 1) * tq, hs])
        return jnp.where(causal, s, MASK_VALUE) if c == qi else s

    chunks = [(hh, qi, c) for hh in range(2) for qi in range(SEQ // tq) for c in range(qi + 1)]
    queued = [logits(*ch) for ch in chunks[:MLA_LOOKAHEAD]]
    for i, (hh, qi, c) in enumerate(chunks):
        if i + MLA_LOOKAHEAD < len(chunks):
            queued.append(logits(*chunks[i + MLA_LOOKAHEAD]))
        s = queued.pop(0)
        m_chunk = jnp.max(s, axis=0, keepdims=True)
        if c == 0:
            m = m_chunk
            acc = _dot(vt_ref[:, 0:tq], jnp.exp2(s - m).astype(BF16))
        else:
            m_new = jnp.maximum(m, m_chunk)
            pv_ref[i % 2] = _dot(vt_ref[:, c * tq:(c + 1) * tq], jnp.exp2(s - m_new).astype(BF16))
            acc = acc * jnp.exp2(m - m_new) + pv_ref[i % 2]
            m = m_new
        if c == qi:
            rows = slice(hh * V_HEAD, (hh + 1) * V_HEAD)
            ot_ref[rows, qi * tq:(qi + 1) * tq] = acc[rows, :] * (1.0 / acc[2 * V_HEAD:2 * V_HEAD + 1, :])
    o_ref[...] = ot_ref[...].T.astype(BF16)
    for src_ref, dst_ref in zip(cast_in, cast_out):
        dst_ref[...] = src_ref[...].astype(BF16)


def _mla_attn(q, k, v, casts):
    pairs = MLA_HEADS // 2
    cast_args, cast_in_specs, cast_out_specs, cast_shapes = [], [], [], []
    for array, first_set, block_rows in casts:
        sets, rows, cols = array.shape
        first_block, n_blocks = first_set * rows // block_rows, (sets - first_set) * rows // block_rows
        assert first_set * rows % block_rows == 0 and rows % block_rows == 0 and n_blocks <= BATCH * pairs

        def block(b, p, first=0, n_blocks=n_blocks):
            return first + jnp.minimum(b * pairs + p, n_blocks - 1), 0
        cast_args.append(array.reshape(sets * rows, cols))
        cast_in_specs.append(pl.BlockSpec((block_rows, cols), functools.partial(block, first=first_block)))
        cast_out_specs.append(pl.BlockSpec((block_rows, cols), block))
        cast_shapes.append(jax.ShapeDtypeStruct(((sets - first_set) * rows, cols), BF16))
    att, *copies = pl.pallas_call(
        functools.partial(_mla_attn_kernel, n_cast=len(casts)),
        grid=(BATCH, pairs),
        in_specs=[
            pl.BlockSpec((SEQ, 2 * HEAD_SLOT), lambda b, p: (b, p)),
            pl.BlockSpec((SEQ, 2 * HEAD_SLOT), lambda b, p: (b, p)),
            pl.BlockSpec((SEQ, 2 * V_HEAD), lambda b, p: (b, p)),
            *cast_in_specs,
        ],
        out_specs=[pl.BlockSpec((SEQ, 2 * V_HEAD), lambda b, p: (b, p)), *cast_out_specs],
        out_shape=[jax.ShapeDtypeStruct((TOKENS, MLA_HEADS * V_HEAD), BF16), *cast_shapes],
        scratch_shapes=[pltpu.VMEM((2 * V_HEAD + MLA_SUM_ROWS, SEQ), BF16),
                        pltpu.VMEM((2 * V_HEAD, SEQ), F32),
                        pltpu.VMEM((2, 2 * V_HEAD + MLA_SUM_ROWS, TQ_MLA), F32)],
        compiler_params=_params("arbitrary", "arbitrary"),
        name="mla_attn",
    )(q, k, v, *cast_args)
    return att, [copy.reshape(-1, *array.shape[1:]) for copy, (array, _, _) in zip(copies, casts)]


def _t5_bucket(dist):
    max_exact = N_BUCKETS // 2
    d = np.maximum(dist, 1).astype(np.float32)
    scale = np.float32(math.log(MAX_DISTANCE / max_exact))
    large = max_exact + (np.log(d / np.float32(max_exact)) / scale
                         * np.float32(N_BUCKETS - max_exact)).astype(np.int32)
    large = np.minimum(large, N_BUCKETS - 1)
    return np.where(dist < max_exact, dist, large)


def _bias_selectors():
    j = np.arange(2 * DIL_BLOCK)
    rel = DIL_BLOCK - j
    sel = []
    for window, dilation in DIL_GROUPS:
        assert window // dilation == DIL_BLOCK
        bucket = _t5_bucket(np.maximum(rel, 0) * dilation)
        onehot = (bucket[None, :] == np.arange(N_BUCKETS)[:, None]) & (rel >= 0)[None, :]
        sel.append(onehot.astype(np.float32))
    return jnp.asarray(np.stack(sel))


def _bias_kernel(rbt_ref, sel_ref, o_ref):
    rbt = rbt_ref[...]
    sel = sel_ref[...]
    row0 = jnp.zeros((DIL_HEADS, 2 * DIL_BLOCK), F32)
    for b in range(N_BUCKETS):
        row0 = row0 + rbt[:, b:b + 1] * sel[b:b + 1, :]
    future = jnp.sum(sel, axis=0, keepdims=True) < 0.5
    row0 = jnp.where(future, MASK_VALUE, row0 * LOG2_E)
    for h in range(DIL_HEADS):
        t = jnp.broadcast_to(row0[h:h + 1, :], (DIL_BLOCK, 2 * DIL_BLOCK))
        o_ref[h] = pltpu.roll(t, 0, 1, stride=1, stride_axis=0)


def _bias_tables(rel_bias):
    rbt = rel_bias.T.reshape(N_GROUPS, DIL_HEADS, N_BUCKETS)
    return pl.pallas_call(
        _bias_kernel,
        grid=(N_GROUPS,),
        in_specs=[
            pl.BlockSpec((None, DIL_HEADS, N_BUCKETS), lambda g: (g, 0, 0)),
            pl.BlockSpec((None, N_BUCKETS, 2 * DIL_BLOCK), lambda g: (g, 0, 0)),
        ],
        out_specs=pl.BlockSpec((DIL_HEADS, DIL_BLOCK, 2 * DIL_BLOCK), lambda g: (g, 0, 0)),
        out_shape=jax.ShapeDtypeStruct((N_GROUPS * DIL_HEADS, DIL_BLOCK, 2 * DIL_BLOCK), F32),
        compiler_params=_params("arbitrary"),
        name="dil_bias",
    )(rbt, _bias_selectors())


def _dil_proj_kernel(x_ref, shift_ref, scale_ref, pre_ref, w_ref, o_ref, *scratch, dilation):
    tm = x_ref.shape[0]
    per = PROJ_SUB // dilation
    slabs = D_MODEL // LANES
    hbs = []
    for i, r0 in enumerate(range(0, tm, PROJ_SUB)):
        hn = _rms(x_ref[r0:r0 + PROJ_SUB, :], pre_ref[...]) * (1.0 + scale_ref[...]) + shift_ref[...]
        if dilation == 1:
            hbs.append(hn.astype(BF16))
            continue
        slab_ref, = scratch
        for s in range(slabs):
            slab_ref[i, s] = hn[:, s * LANES:(s + 1) * LANES]
        hbs.append(jnp.concatenate(
            [jnp.concatenate([slab_ref[i, s, pl.ds(r, per, stride=dilation), :] for s in range(slabs)],
                             axis=1).astype(BF16)
             for r in range(dilation)], axis=0))
    for i, hb in enumerate(hbs):
        for j in range(3):
            cols = slice(j * DIL_WIDTH, (j + 1) * DIL_WIDTH)
            y = _dot(hb, w_ref[:, cols])
            if j == 0:
                y = y * (DIL_HEAD_DIM ** -0.5 * LOG2_E)
            y = y.astype(BF16)
            for r in range(dilation):
                o_ref[r, i * per:(i + 1) * per, cols] = y[r * per:(r + 1) * per, :]


def _dil_proj(x, mod, pre, w, layer, group):
    _, dilation = DIL_GROUPS[group]
    tm = TM_DIL_PROJ
    tiles_per_batch = SEQ // tm
    sub = tm // dilation
    scratch = [] if dilation == 1 else [
        pltpu.VMEM((tm // PROJ_SUB, D_MODEL // LANES, PROJ_SUB, LANES), F32)]
    return pl.pallas_call(
        functools.partial(_dil_proj_kernel, dilation=dilation),
        grid=(TOKENS // tm,),
        in_specs=[
            pl.BlockSpec((tm, D_MODEL), lambda t: (t, 0)),
            _mod_spec(layer, 1, 0, tm), _mod_spec(layer, 1, 1, tm),
            _row_spec(layer * N_SUB + 1),
            pl.BlockSpec((D_MODEL, 3 * DIL_WIDTH), lambda t: (0, group), pipeline_mode=pl.Buffered(1)),
        ],
        out_specs=pl.BlockSpec((None, dilation, sub, 3 * DIL_WIDTH),
                               lambda t: (t // tiles_per_batch, 0, t % tiles_per_batch, 0)),
        out_shape=jax.ShapeDtypeStruct((BATCH, dilation, SEQ // dilation, 3 * DIL_WIDTH), BF16),
        scratch_shapes=scratch,
        compiler_params=_params("arbitrary"),
        name=f"dil_proj_g{group}",
    )(x, mod, mod, pre, w)


DIL_HEADS_PER_STEP = 8
DIL_PAIRS_PER_STEP = DIL_HEADS_PER_STEP // 2
DIL_UNITS_PER_TRIP = 4


STAT_MAX_SHIFT = 32


def _stat_lane(head):
    return head if head % 2 else DIL_HEAD_DIM + head


def _dil_attn_kernel(q_ref, k_ref, v_ref, bias_ref, o_ref, stat_ref, s_ref, acc_ref, *, dilation, n_blocks):
    blk = DIL_BLOCK
    lane = lax.broadcasted_iota(jnp.int32, (blk, LANES), 1)
    first_head = lane < DIL_HEAD_DIM
    head0 = pl.program_id(1) * DIL_HEADS_PER_STEP

    def where_unit(i, first):
        if first:
            return i, 0
        later = n_blocks - 1
        if dilation == 1:
            return 0, i + 1
        i = jnp.asarray(i, jnp.int32)
        return lax.div(i, jnp.int32(later)), lax.rem(i, jnp.int32(later)) + 1

    def window(ref, r, n, first, cs):
        if first:
            return ref[r, 0:blk, cs]
        return ref[r, pl.ds(pl.multiple_of(n * blk - blk, blk), 2 * blk), cs]

    def logits(i, slot, first):
        r, n = where_unit(i, first)
        kw = blk if first else 2 * blk
        q_rows = slice(0, blk) if first else pl.ds(pl.multiple_of(n * blk, blk), blk)
        for hp in range(DIL_PAIRS_PER_STEP):
            cs = slice(hp * LANES, (hp + 1) * LANES)
            q2 = q_ref[r, q_rows, cs]
            k2 = window(k_ref, r, n, first, cs)
            for hh in range(2):
                h = 2 * hp + hh
                qh = jnp.where(first_head if hh == 0 else jnp.logical_not(first_head), q2, 0)
                bias = bias_ref[h, :, blk:] if first else bias_ref[h]
                s_ref[slot, h, :, 0:kw] = _dot_nt(qh, k2) + bias

    def attend(i, slot, first):
        r, n = where_unit(i, first)
        kw = blk if first else 2 * blk
        t0 = n * (blk * dilation) + r
        rows = pl.ds(t0, blk) if dilation == 1 else pl.ds(t0, blk, stride=dilation)
        stat_tile = jnp.ones((blk, LANES), F32)
        for hp in range(DIL_PAIRS_PER_STEP):
            cs = slice(hp * LANES, (hp + 1) * LANES)
            v2 = window(v_ref, r, n, first, cs)
            accs = []
            for hh in range(2):
                s = s_ref[slot, 2 * hp + hh, :, 0:kw]
                m = jnp.max(s, axis=-1, keepdims=True)
                key_lane = lax.broadcasted_iota(jnp.int32, (kw, LANES), 1)
                own = key_lane < DIL_HEAD_DIM if hh == 0 else key_lane >= DIL_HEAD_DIM
                acc = _dot(jnp.exp2(s - m).astype(BF16), jnp.where(own, v2, 1))
                sum_lane = head0 + _stat_lane(2 * hp + hh)
                stat_tile = jnp.where(lane == sum_lane, acc, stat_tile)
                stat_tile = jnp.where(lane == sum_lane + STAT_MAX_SHIFT, m, stat_tile)
                accs.append(acc)
            acc_ref[hp, rows, :] = jnp.where(first_head, accs[0], accs[1])
        stat_ref[rows, :] = stat_tile

    def run(count, first):
        trips, rest = divmod(count, DIL_UNITS_PER_TRIP)

        def units(base, n, more):
            for u in range(n):
                if u + 1 < n or more:
                    logits(base + u + 1, (u + 1) % 2, first)
                attend(base + u, u % 2, first)

        logits(0, 0, first)
        if trips:
            def body(j, carry):
                units(j * DIL_UNITS_PER_TRIP, DIL_UNITS_PER_TRIP, True)
                return carry
            lax.fori_loop(0, trips - (0 if rest else 1), body, 0)
            if not rest:
                units((trips - 1) * DIL_UNITS_PER_TRIP, DIL_UNITS_PER_TRIP, False)
        if rest:
            units(trips * DIL_UNITS_PER_TRIP, rest, False)

    run(dilation, True)
    if n_blocks > 1:
        run(dilation * (n_blocks - 1), False)
    for hp in range(DIL_PAIRS_PER_STEP):
        o_ref[:, hp * LANES:(hp + 1) * LANES] = acc_ref[hp].astype(BF16)


def _dil_attn(proj, bias, group):
    _, dilation = DIL_GROUPS[group]
    length = SEQ // dilation
    halves = DIL_HEADS // DIL_HEADS_PER_STEP
    width = DIL_HEADS_PER_STEP * DIL_HEAD_DIM
    qkv_spec = lambda kind: pl.BlockSpec(
        (None, dilation, length, width), lambda b, u: (b, 0, 0, kind * halves + u))
    return pl.pallas_call(
        functools.partial(_dil_attn_kernel, dilation=dilation, n_blocks=length // DIL_BLOCK),
        grid=(BATCH, halves),
        in_specs=[
            qkv_spec(0), qkv_spec(1), qkv_spec(2),
            pl.BlockSpec((DIL_HEADS_PER_STEP, DIL_BLOCK, 2 * DIL_BLOCK),
                         lambda b, u: (group * halves + u, 0, 0)),
        ],
        out_specs=[
            pl.BlockSpec((None, SEQ, width), lambda b, u: (b, 0, u)),
            pl.BlockSpec((None, SEQ, LANES), lambda b, u: (b, 0, u)),
        ],
        out_shape=[
            jax.ShapeDtypeStruct((BATCH, SEQ, DIL_WIDTH), BF16),
            jax.ShapeDtypeStruct((BATCH, SEQ, halves * LANES), F32),
        ],
        scratch_shapes=[pltpu.VMEM((2, DIL_HEADS_PER_STEP, DIL_BLOCK, 2 * DIL_BLOCK), F32),
                        pltpu.VMEM((DIL_PAIRS_PER_STEP, SEQ, LANES), F32)],
        compiler_params=_params("arbitrary", "arbitrary"),
        name=f"dil_attn_g{group}",
    )(proj, proj, proj, bias)


def _dil_out_kernel(x_ref, o0_ref, o1_ref, o2_ref, s0_ref, s1_ref, s2_ref, e_ref, wo_ref,
                    gate_ref, post_ref, out_ref):
    lane = lax.broadcasted_iota(jnp.int32, (1, LANES), 1)
    in_first_block = functools.reduce(jnp.logical_or, [
        jnp.logical_or(lane == _stat_lane(h), lane == _stat_lane(h) + STAT_MAX_SHIFT)
        for h in range(DIL_HEADS_PER_STEP)])
    e = e_ref[...]
    for r0 in range(0, x_ref.shape[0], OUT_SUB):
        rows = slice(r0, r0 + OUT_SUB)
        sums = [jnp.where(in_first_block, r[rows, :LANES], r[rows, LANES:]) for r in (s0_ref, s1_ref, s2_ref)]
        maxes = [pltpu.roll(s, LANES - STAT_MAX_SHIFT, 1) for s in sums]
        m = jnp.maximum(jnp.maximum(maxes[0], maxes[1]), maxes[2])
        es = [jnp.exp2(mg - m) for mg in maxes]
        inv = 1.0 / (sums[0] * es[0] + sums[1] * es[1] + sums[2] * es[2])
        mix = None
        for eg, o_ref in zip(es, (o0_ref, o1_ref, o2_ref)):
            alpha = eg * inv
            hi = alpha.astype(BF16)
            lo = (alpha - hi.astype(F32)).astype(BF16)
            term = _dot(jnp.concatenate([hi, lo], axis=1), e) * o_ref[rows, :].astype(F32)
            mix = term if mix is None else mix + term
        y = _dot(mix.astype(BF16), wo_ref[...])
        out_ref[rows, :] = x_ref[rows, :] + gate_ref[...] * _rms(y, post_ref[...])


def _dil_out(x, outs, stats, mod, post, w_o, layer):
    tm = TM_OUT
    tiles_per_batch = SEQ // tm
    halves = DIL_HEADS // DIL_HEADS_PER_STEP
    stat_lane_of_col = jnp.array([_stat_lane(c // DIL_HEAD_DIM) for c in range(DIL_WIDTH)], jnp.int32)
    lane_of_row = jnp.arange(2 * LANES) % LANES
    expand = (lane_of_row[:, None] == stat_lane_of_col[None, :]).astype(BF16)
    batch_rows = lambda width: pl.BlockSpec(
        (None, tm, width), lambda t: (t // tiles_per_batch, t % tiles_per_batch, 0))
    tok = pl.BlockSpec((tm, D_MODEL), lambda t: (t, 0))
    return pl.pallas_call(
        _dil_out_kernel,
        grid=(TOKENS // tm,),
        in_specs=[
            tok, *([batch_rows(DIL_WIDTH)] * 3), *([batch_rows(halves * LANES)] * 3),
            pl.BlockSpec(expand.shape, lambda t: (0, 0)), pl.BlockSpec(w_o.shape, lambda t: (0, 0)),
            _mod_spec(layer, 1, 2, tm), _row_spec(layer * N_SUB + 1),
        ],
        out_specs=tok,
        out_shape=jax.ShapeDtypeStruct((TOKENS, D_MODEL), F32),
        compiler_params=_params("arbitrary"),
        name="dil_out",
    )(x, *outs, *stats, expand, w_o, mod, post)


def kernel(x, c, norm_pre, norm_post, w_mod, b_mod, ffn_w_gate, ffn_w_up, ffn_w_down,
           mla_w_in, mla_q_norm, mla_w_q_up, mla_kv_norm, mla_w_kv_up, mla_w_o,
           dil_w_in, dil_w_o, rel_bias):
    assert x.shape == (BATCH, SEQ, D_MODEL) and x.dtype == F32
    assert DEPTH == 2
    h = x.reshape(TOKENS, D_MODEL)
    mod = _modulation(c, w_mod, b_mod)
    pre = norm_pre.reshape(DEPTH * N_SUB, 1, D_MODEL)
    post = norm_post.reshape(DEPTH * N_SUB, 1, D_MODEL)
    tables = _rope_tables()
    bias = _bias_tables(rel_bias)
    ffn_f32 = [w.reshape(DEPTH * 2, *w.shape[2:]) for w in (ffn_w_gate, ffn_w_up, ffn_w_down)]

    h = _ffn(h, mod, pre, post, (*[w[:1].astype(BF16) for w in ffn_f32], 0), 0, 0)
    win, wq, wk, wv = _mla_weights(mla_w_in[0], mla_w_q_up[0], mla_w_kv_up[0])
    q, k, v = _mla_proj(h, mod, pre, win, mla_q_norm[0][None, :], mla_kv_norm[0][None, :],
                        wq, wk, wv, tables, 0)
    casts = [(ffn_f32[0], 1, 64), (ffn_f32[1], 1, 64), (ffn_f32[2], 1, 256), (dil_w_in, 0, 16)]
    att, (wg, wu, wd, dil_in) = _mla_attn(q, k, v, casts)
    h = _ffn(h, mod, pre, post, (wg, wu, wd, 0), 0, 1, mla=(att, mla_w_o[0].astype(BF16)))

    h = _ffn(h, mod, pre, post, (wg, wu, wd, 1), 1, 0)
    outs, stats = zip(*[_dil_attn(_dil_proj(h, mod, pre, dil_in[0], 1, g), bias, g) for g in range(N_GROUPS)])
    h = _dil_out(h, outs, stats, mod, post, dil_w_o[0].astype(BF16), 1)
    h = _ffn(h, mod, pre, post, (wg, wu, wd, 2), 1, 1)
    return h.reshape(BATCH, SEQ, D_MODEL)
```

```python
import functools
import math

import jax
import jax.numpy as jnp
import numpy as np
from jax import lax
from jax.experimental import pallas as pl
from jax.experimental.pallas import tpu as pltpu

F32 = jnp.float32
BF16 = jnp.bfloat16

D_MODEL = 1024
BATCH = 8
SEQ = 2048
DEPTH = 2
N_SUB = 3
D_FF = 2816
FFN_RES = 0.5
EPS = 1e-6

MLA_HEADS = 16
Q_LORA = 384
KV_LORA = 256
QK_NOPE = 64
QK_ROPE = 32
V_HEAD = 64
ROPE_THETA = 10000.0

DIL_GROUPS = ((128, 1), (512, 4), (2048, 16))
N_GROUPS = 3
DIL_HEADS = 16
DIL_HEAD_DIM = 64
DIL_BLOCK = 128
DIL_WIDTH = DIL_HEADS * DIL_HEAD_DIM
N_BUCKETS = 32
MAX_DISTANCE = 2048

TOKENS = BATCH * SEQ
LANES = 128
HEAD_SLOT = 128
ROPE_HALF = QK_ROPE // 2
MASK_VALUE = -1e30
LOG2_E = math.log2(math.e)
VMEM_LIMIT = 56 * 1024 * 1024

TM_FFN = 1024
FFN_SUB = 256
TM_PROJ = 1024
PROJ_SUB = 256
TM_DIL_PROJ = 1024
TM_OUT = 1024
OUT_SUB = 256
TQ_MLA = 256
MLA_SUM_ROWS = 16
MLA_LOOKAHEAD = 6
TN_MOD = 1536
FF_CHUNK = 1536


def _params(*sem):
    return pltpu.CompilerParams(dimension_semantics=sem, vmem_limit_bytes=VMEM_LIMIT)


def _rms(x, g):
    ms = jnp.mean(x * x, axis=-1, keepdims=True)
    return x * lax.rsqrt(ms + EPS) * g


def _dot(a, b):
    return jnp.dot(a, b, preferred_element_type=F32)


def _dot_nt(a, b):
    return lax.dot_general(a, b, (((1,), (1,)), ((), ())), preferred_element_type=F32)


def _silu(x):
    return x * jax.nn.sigmoid(x)


def _mod_spec(layer, sub, kind, tm):
    tiles_per_batch = SEQ // tm
    return pl.BlockSpec(
        (None, 1, D_MODEL),
        lambda t, *_: ((layer * BATCH + t // tiles_per_batch) * (N_SUB * 3) + sub * 3 + kind, 0, 0))


def _row_spec(index):
    return pl.BlockSpec((None, 1, D_MODEL), lambda *_: (index, 0, 0))


def _mod_kernel(c_ref, w_ref, b_ref, o_ref):
    h = _silu(c_ref[...])
    hi = h.astype(BF16)
    lo = (h - hi.astype(F32)).astype(BF16)
    y = _dot(jnp.concatenate([hi, lo], axis=0), w_ref[...].astype(BF16))
    o_ref[...] = y[:BATCH] + y[BATCH:] + b_ref[...]


def _modulation(c, w_mod, b_mod):
    n = N_SUB * 3 * D_MODEL
    out = pl.pallas_call(
        _mod_kernel,
        grid=(DEPTH, n // TN_MOD),
        in_specs=[
            pl.BlockSpec((BATCH, D_MODEL), lambda i, j: (0, 0)),
            pl.BlockSpec((None, D_MODEL, TN_MOD), lambda i, j: (i, 0, j)),
            pl.BlockSpec((None, 1, TN_MOD), lambda i, j: (i, 0, j)),
        ],
        out_specs=pl.BlockSpec((None, BATCH, TN_MOD), lambda i, j: (i, 0, j)),
        out_shape=jax.ShapeDtypeStruct((DEPTH, BATCH, n), F32),
        compiler_params=_params("arbitrary", "arbitrary"),
        name="modulation",
    )(c, w_mod, b_mod.reshape(DEPTH, 1, n))
    return out.reshape(DEPTH * BATCH * N_SUB * 3, 1, D_MODEL)


def _mla_mix(rows, att_ref, wo_ref):
    return _dot(att_ref[rows, :], wo_ref[...])


def _ffn_kernel(x_ref, *refs, after_mla):
    if after_mla:
        att_ref, wo_ref, mgate_ref, mpost_ref, *refs = refs
    shift_ref, scale_ref, gate_ref, pre_ref, post_ref, wg_ref, wu_ref, wd_ref, o_ref, a_ref = refs
    tm = x_ref.shape[0]
    subs = [slice(r0, r0 + FFN_SUB) for r0 in range(0, tm, FFN_SUB)]
    hns = []
    for rows in subs:
        x = x_ref[rows, :]
        if after_mla:
            x = x + mgate_ref[...] * _rms(_mla_mix(rows, att_ref, wo_ref), mpost_ref[...])
            o_ref[rows, :] = x
        hns.append((_rms(x, pre_ref[...]) * (1.0 + scale_ref[...]) + shift_ref[...]).astype(BF16))
    for rows, hn in zip(subs, hns):
        for c0 in range(0, D_FF, FF_CHUNK):
            c1 = min(c0 + FF_CHUNK, D_FF)
            g = _dot(hn, wg_ref[:, c0:c1])
            u = _dot(hn, wu_ref[:, c0:c1])
            a_ref[rows, c0:c1] = (_silu(g) * u).astype(BF16)
    for rows in subs:
        y = _dot(a_ref[rows, :], wd_ref[...])
        x = o_ref[rows, :] if after_mla else x_ref[rows, :]
        o_ref[rows, :] = x + FFN_RES * gate_ref[...] * _rms(y, post_ref[...])


def _ffn(x, mod, pre, post, weights, layer, which, mla=None):
    wg, wu, wd, w_set = weights
    sub = 0 if which == 0 else 2
    tm = TM_FFN
    w_in_spec = pl.BlockSpec((None, D_MODEL, D_FF), lambda t: (w_set, 0, 0), pipeline_mode=pl.Buffered(1))
    w_out_spec = pl.BlockSpec((None, D_FF, D_MODEL), lambda t: (w_set, 0, 0), pipeline_mode=pl.Buffered(1))
    mla_args, mla_specs = [], []
    if mla is not None:
        att, w_o = mla
        mla_args = [att, w_o, mod, post]
        mla_specs = [pl.BlockSpec((tm, att.shape[1]), lambda t: (t, 0)),
                     pl.BlockSpec(w_o.shape, lambda t: (0, 0)),
                     _mod_spec(layer, 1, 2, tm), _row_spec(layer * N_SUB + 1)]
    return pl.pallas_call(
        functools.partial(_ffn_kernel, after_mla=mla is not None),
        grid=(TOKENS // tm,),
        in_specs=[
            pl.BlockSpec((tm, D_MODEL), lambda t: (t, 0)),
            *mla_specs,
            _mod_spec(layer, sub, 0, tm), _mod_spec(layer, sub, 1, tm), _mod_spec(layer, sub, 2, tm),
            _row_spec(layer * N_SUB + sub), _row_spec(layer * N_SUB + sub),
            w_in_spec, w_in_spec, w_out_spec,
        ],
        out_specs=pl.BlockSpec((tm, D_MODEL), lambda t: (t, 0)),
        out_shape=jax.ShapeDtypeStruct((TOKENS, D_MODEL), F32),
        scratch_shapes=[pltpu.VMEM((tm, D_FF), BF16)],
        compiler_params=_params("arbitrary"),
        name="ffn" if mla is None else "mla_out_ffn",
    )(x, *mla_args, mod, mod, mod, pre, post, wg, wu, wd)


def _rope(z, ta, tb):
    return z * ta + pltpu.roll(z, HEAD_SLOT // 2, 1) * tb


def _mla_proj_kernel(x_ref, shift_ref, scale_ref, pre_ref, win_ref, qn_ref, kvn_ref,
                     wq_ref, wk_ref, wv_ref, ta_ref, tb_ref, q_ref, k_ref, v_ref):
    tm = x_ref.shape[0]
    subs = [slice(r0, r0 + PROJ_SUB) for r0 in range(0, tm, PROJ_SUB)]
    lats = []
    for rows in subs:
        hn = (_rms(x_ref[rows, :], pre_ref[...]) * (1.0 + scale_ref[...]) + shift_ref[...]).astype(BF16)
        lats.append(_dot(hn, win_ref[...]))
    for rows, lat in zip(subs, lats):
        cq = _rms(lat[:, :Q_LORA], qn_ref[...]).astype(BF16)
        ckv = _rms(lat[:, Q_LORA:Q_LORA + KV_LORA], kvn_ref[...]).astype(BF16)
        ta, tb = ta_ref[rows, :], tb_ref[rows, :]
        kr = _rope(lat[:, Q_LORA + KV_LORA:], ta, tb)
        q = _dot(cq, wq_ref[...]) * ((QK_NOPE + QK_ROPE) ** -0.5 * LOG2_E)
        kn = _dot(ckv, wk_ref[...])
        v_ref[rows, :] = _dot(ckv, wv_ref[...]).astype(BF16)
        for h in range(MLA_HEADS):
            sl = slice(h * HEAD_SLOT, (h + 1) * HEAD_SLOT)
            q_ref[rows, sl] = _rope(q[:, sl], ta, tb).astype(BF16)
            k_ref[rows, sl] = (kn[:, sl] + kr).astype(BF16)


def _head_slot(nope, rope, like):
    split = HEAD_SLOT // 2 - ROPE_HALF
    zeros = lambda n: jnp.zeros(like.shape[:-1] + (n,), like.dtype)
    nope = zeros(QK_NOPE) if nope is None else nope
    rope = zeros(QK_ROPE) if rope is None else rope
    return jnp.concatenate([rope[..., :ROPE_HALF], nope[..., :split], rope[..., ROPE_HALF:], nope[..., split:],
                            zeros(HEAD_SLOT - QK_NOPE - QK_ROPE)], axis=-1)


def _rope_tables():
    pos = jnp.arange(SEQ, dtype=F32)
    freqs = ROPE_THETA ** (-jnp.arange(ROPE_HALF, dtype=F32) / ROPE_HALF)
    ang = pos[:, None] * freqs[None, :]
    cos, sin = jnp.cos(ang), jnp.sin(ang)
    ones = jnp.ones((SEQ, QK_NOPE), F32)
    ta = _head_slot(ones, jnp.concatenate([cos, cos], axis=1), cos)
    tb = _head_slot(None, jnp.concatenate([-sin, sin], axis=1), sin)
    return ta, tb


def _mla_weights(w_in, w_q_up, w_kv_up):
    win = jnp.concatenate([w_in[:, :Q_LORA + KV_LORA],
                           _head_slot(None, w_in[:, Q_LORA + KV_LORA:], w_in)], axis=1).astype(BF16)
    wq = w_q_up.reshape(Q_LORA, MLA_HEADS, QK_NOPE + QK_ROPE)
    wq = _head_slot(wq[..., :QK_NOPE], wq[..., QK_NOPE:], wq)
    wq = wq.reshape(Q_LORA, MLA_HEADS * HEAD_SLOT).astype(BF16)
    wkv = w_kv_up.reshape(KV_LORA, MLA_HEADS, QK_NOPE + V_HEAD)
    wk = _head_slot(wkv[..., :QK_NOPE], None, wkv).reshape(KV_LORA, MLA_HEADS * HEAD_SLOT).astype(BF16)
    wv = wkv[..., QK_NOPE:].reshape(KV_LORA, MLA_HEADS * V_HEAD).astype(BF16)
    return win, wq, wk, wv


def _mla_proj(x, mod, pre, win, qn, kvn, wq, wk, wv, tables, layer):
    tm = TM_PROJ
    tiles_per_batch = SEQ // tm
    full = lambda a: pl.BlockSpec(a.shape, lambda t: (0,) * a.ndim)
    tab_spec = pl.BlockSpec((tm, HEAD_SLOT), lambda t: (t % tiles_per_batch, 0))
    wide = MLA_HEADS * HEAD_SLOT
    return pl.pallas_call(
        _mla_proj_kernel,
        grid=(TOKENS // tm,),
        in_specs=[
            pl.BlockSpec((tm, D_MODEL), lambda t: (t, 0)),
            _mod_spec(layer, 1, 0, tm), _mod_spec(layer, 1, 1, tm),
            _row_spec(layer * N_SUB + 1),
            full(win), full(qn), full(kvn), full(wq), full(wk), full(wv),
            tab_spec, tab_spec,
        ],
        out_specs=[
            pl.BlockSpec((tm, wide), lambda t: (t, 0)),
            pl.BlockSpec((tm, wide), lambda t: (t, 0)),
            pl.BlockSpec((tm, MLA_HEADS * V_HEAD), lambda t: (t, 0)),
        ],
        out_shape=[
            jax.ShapeDtypeStruct((TOKENS, wide), BF16),
            jax.ShapeDtypeStruct((TOKENS, wide), BF16),
            jax.ShapeDtypeStruct((TOKENS, MLA_HEADS * V_HEAD), BF16),
        ],
        compiler_params=_params("arbitrary"),
        name="mla_proj",
    )(x, mod, mod, pre, win, qn, kvn, wq, wk, wv, *tables)


def _mla_attn_kernel(q_ref, k_ref, v_ref, *refs, n_cast):
    cast_in, (o_ref, *cast_out), (vt_ref, ot_ref, pv_ref) = (
        refs[:n_cast], refs[n_cast:2 * n_cast + 1], refs[2 * n_cast + 1:])
    tq = TQ_MLA
    key = lax.broadcasted_iota(jnp.int32, (tq, tq), 0)
    query = lax.broadcasted_iota(jnp.int32, (tq, tq), 1)
    causal = key <= query
    vt_ref[:2 * V_HEAD, :] = v_ref[...].astype(F32).T.astype(BF16)
    vt_ref[2 * V_HEAD:, :] = jnp.ones((MLA_SUM_ROWS, SEQ), BF16)

    def logits(hh, qi, c):
        hs = slice(hh * HEAD_SLOT, (hh + 1) * HEAD_SLOT)
        s = _dot_nt(k_ref[c * tq:(c + 1) * tq, hs], q_ref[qi * tq:(qi + 1) * tq, hs])
        return jnp.where(causal, s, MASK_VALUE) if c == qi else s

    chunks = [(hh, qi, c) for hh in range(2) for qi in range(SEQ // tq) for c in range(qi + 1)]
    queued = [logits(*ch) for ch in chunks[:MLA_LOOKAHEAD]]
    for i, (hh, qi, c) in enumerate(chunks):
        if i + MLA_LOOKAHEAD < len(chunks):
            queued.append(logits(*chunks[i + MLA_LOOKAHEAD]))
        s = queued.pop(0)
        m_chunk = jnp.max(s, axis=0, keepdims=True)
        if c == 0:
            m = m_chunk
            acc = _dot(vt_ref[:, 0:tq], jnp.exp2(s - m).astype(BF16))
        else:
            m_new = jnp.maximum(m, m_chunk)
            pv_ref[i % 2] = _dot(vt_ref[:, c * tq:(c + 1) * tq], jnp.exp2(s - m_new).astype(BF16))
            acc = acc * jnp.exp2(m - m_new) + pv_ref[i % 2]
            m = m_new
        if c == qi:
            rows = slice(hh * V_HEAD, (hh + 1) * V_HEAD)
            ot_ref[rows, qi * tq:(qi + 1) * tq] = acc[rows, :] * (1.0 / acc[2 * V_HEAD:2 * V_HEAD + 1, :])
    o_ref[...] = ot_ref[...].T.astype(BF16)
    for src_ref, dst_ref in zip(cast_in, cast_out):
        dst_ref[...] = src_ref[...].astype(BF16)


def _mla_attn(q, k, v, casts):
    pairs = MLA_HEADS // 2
    cast_args, cast_in_specs, cast_out_specs, cast_shapes = [], [], [], []
    for array, first_set, block_rows in casts:
        sets, rows, cols = array.shape
        first_block, n_blocks = first_set * rows // block_rows, (sets - first_set) * rows // block_rows
        assert first_set * rows % block_rows == 0 and rows % block_rows == 0 and n_blocks <= BATCH * pairs

        def block(b, p, first=0, n_blocks=n_blocks):
            return first + jnp.minimum(b * pairs + p, n_blocks - 1), 0
        cast_args.append(array.reshape(sets * rows, cols))
        cast_in_specs.append(pl.BlockSpec((block_rows, cols), functools.partial(block, first=first_block)))
        cast_out_specs.append(pl.BlockSpec((block_rows, cols), block))
        cast_shapes.append(jax.ShapeDtypeStruct(((sets - first_set) * rows, cols), BF16))
    att, *copies = pl.pallas_call(
        functools.partial(_mla_attn_kernel, n_cast=len(casts)),
        grid=(BATCH, pairs),
        in_specs=[
            pl.BlockSpec((SEQ, 2 * HEAD_SLOT), lambda b, p: (b, p)),
            pl.BlockSpec((SEQ, 2 * HEAD_SLOT), lambda b, p: (b, p)),
            pl.BlockSpec((SEQ, 2 * V_HEAD), lambda b, p: (b, p)),
            *cast_in_specs,
        ],
        out_specs=[pl.BlockSpec((SEQ, 2 * V_HEAD), lambda b, p: (b, p)), *cast_out_specs],
        out_shape=[jax.ShapeDtypeStruct((TOKENS, MLA_HEADS * V_HEAD), BF16), *cast_shapes],
        scratch_shapes=[pltpu.VMEM((2 * V_HEAD + MLA_SUM_ROWS, SEQ), BF16),
                        pltpu.VMEM((2 * V_HEAD, SEQ), F32),
                        pltpu.VMEM((2, 2 * V_HEAD + MLA_SUM_ROWS, TQ_MLA), F32)],
        compiler_params=_params("arbitrary", "arbitrary"),
        name="mla_attn",
    )(q, k, v, *cast_args)
    return att, [copy.reshape(-1, *array.shape[1:]) for copy, (array, _, _) in zip(copies, casts)]


def _t5_bucket(dist):
    max_exact = N_BUCKETS // 2
    d = np.maximum(dist, 1).astype(np.float32)
    scale = np.float32(math.log(MAX_DISTANCE / max_exact))
    large = max_exact + (np.log(d / np.float32(max_exact)) / scale
                         * np.float32(N_BUCKETS - max_exact)).astype(np.int32)
    large = np.minimum(large, N_BUCKETS - 1)
    return np.where(dist < max_exact, dist, large)


def _bias_selectors():
    j = np.arange(2 * DIL_BLOCK)
    rel = DIL_BLOCK - j
    sel = []
    for window, dilation in DIL_GROUPS:
        assert window // dilation == DIL_BLOCK
        bucket = _t5_bucket(np.maximum(rel, 0) * dilation)
        onehot = (bucket[None, :] == np.arange(N_BUCKETS)[:, None]) & (rel >= 0)[None, :]
        sel.append(onehot.astype(np.float32))
    return jnp.asarray(np.stack(sel))


def _bias_kernel(rbt_ref, sel_ref, o_ref):
    rbt = rbt_ref[...]
    sel = sel_ref[...]
    row0 = jnp.zeros((DIL_HEADS, 2 * DIL_BLOCK), F32)
    for b in range(N_BUCKETS):
        row0 = row0 + rbt[:, b:b + 1] * sel[b:b + 1, :]
    future = jnp.sum(sel, axis=0, keepdims=True) < 0.5
    row0 = jnp.where(future, MASK_VALUE, row0 * LOG2_E)
    for h in range(DIL_HEADS):
        t = jnp.broadcast_to(row0[h:h + 1, :], (DIL_BLOCK, 2 * DIL_BLOCK))
        o_ref[h] = pltpu.roll(t, 0, 1, stride=1, stride_axis=0)


def _bias_tables(rel_bias):
    rbt = rel_bias.T.reshape(N_GROUPS, DIL_HEADS, N_BUCKETS)
    return pl.pallas_call(
        _bias_kernel,
        grid=(N_GROUPS,),
        in_specs=[
            pl.BlockSpec((None, DIL_HEADS, N_BUCKETS), lambda g: (g, 0, 0)),
            pl.BlockSpec((None, N_BUCKETS, 2 * DIL_BLOCK), lambda g: (g, 0, 0)),
        ],
        out_specs=pl.BlockSpec((DIL_HEADS, DIL_BLOCK, 2 * DIL_BLOCK), lambda g: (g, 0, 0)),
        out_shape=jax.ShapeDtypeStruct((N_GROUPS * DIL_HEADS, DIL_BLOCK, 2 * DIL_BLOCK), F32),
        compiler_params=_params("arbitrary"),
        name="dil_bias",
    )(rbt, _bias_selectors())


def _dil_proj_kernel(x_ref, shift_ref, scale_ref, pre_ref, w_ref, o_ref, *scratch, dilation):
    tm = x_ref.shape[0]
    per = PROJ_SUB // dilation
    slabs = D_MODEL // LANES
    hbs = []
    for i, r0 in enumerate(range(0, tm, PROJ_SUB)):
        hn = _rms(x_ref[r0:r0 + PROJ_SUB, :], pre_ref[...]) * (1.0 + scale_ref[...]) + shift_ref[...]
        if dilation == 1:
            hbs.append(hn.astype(BF16))
            continue
        slab_ref, = scratch
        for s in range(slabs):
            slab_ref[i, s] = hn[:, s * LANES:(s + 1) * LANES]
        hbs.append(jnp.concatenate(
            [jnp.concatenate([slab_ref[i, s, pl.ds(r, per, stride=dilation), :] for s in range(slabs)],
                             axis=1).astype(BF16)
             for r in range(dilation)], axis=0))
    for i, hb in enumerate(hbs):
        for j in range(3):
            cols = slice(j * DIL_WIDTH, (j + 1) * DIL_WIDTH)
            y = _dot(hb, w_ref[:, cols])
            if j == 0:
                y = y * (DIL_HEAD_DIM ** -0.5 * LOG2_E)
            y = y.astype(BF16)
            for r in range(dilation):
                o_ref[r, i * per:(i + 1) * per, cols] = y[r * per:(r + 1) * per, :]


def _dil_proj(x, mod, pre, w, layer, group):
    _, dilation = DIL_GROUPS[group]
    tm = TM_DIL_PROJ
    tiles_per_batch = SEQ // tm
    sub = tm // dilation
    scratch = [] if dilation == 1 else [
        pltpu.VMEM((tm // PROJ_SUB, D_MODEL // LANES, PROJ_SUB, LANES), F32)]
    return pl.pallas_call(
        functools.partial(_dil_proj_kernel, dilation=dilation),
        grid=(TOKENS // tm,),
        in_specs=[
            pl.BlockSpec((tm, D_MODEL), lambda t: (t, 0)),
            _mod_spec(layer, 1, 0, tm), _mod_spec(layer, 1, 1, tm),
            _row_spec(layer * N_SUB + 1),
            pl.BlockSpec((D_MODEL, 3 * DIL_WIDTH), lambda t: (0, group), pipeline_mode=pl.Buffered(1)),
        ],
        out_specs=pl.BlockSpec((None, dilation, sub, 3 * DIL_WIDTH),
                               lambda t: (t // tiles_per_batch, 0, t % tiles_per_batch, 0)),
        out_shape=jax.ShapeDtypeStruct((BATCH, dilation, SEQ // dilation, 3 * DIL_WIDTH), BF16),
        scratch_shapes=scratch,
        compiler_params=_params("arbitrary"),
        name=f"dil_proj_g{group}",
    )(x, mod, mod, pre, w)


DIL_HEADS_PER_STEP = 8
DIL_PAIRS_PER_STEP = DIL_HEADS_PER_STEP // 2
DIL_UNITS_PER_TRIP = 8


STAT_MAX_SHIFT = 32


def _stat_lane(head):
    return head if head % 2 else DIL_HEAD_DIM + head


def _dil_attn_kernel(q_ref, k_ref, v_ref, bias_ref, o_ref, stat_ref, s_ref, acc_ref, *, dilation, n_blocks):
    blk = DIL_BLOCK
    lane = lax.broadcasted_iota(jnp.int32, (blk, LANES), 1)
    first_head = lane < DIL_HEAD_DIM
    head0 = pl.program_id(1) * DIL_HEADS_PER_STEP

    def where_unit(i, first):
        if first:
            return i, 0
        later = n_blocks - 1
        if dilation == 1:
            return 0, i + 1
        i = jnp.asarray(i, jnp.int32)
        return lax.div(i, jnp.int32(later)), lax.rem(i, jnp.int32(later)) + 1

    def window(ref, r, n, first, cs):
        if first:
            return ref[r, 0:blk, cs]
        return ref[r, pl.ds(pl.multiple_of(n * blk - blk, blk), 2 * blk), cs]

    def logits(i, slot, first):
        r, n = where_unit(i, first)
        kw = blk if first else 2 * blk
        q_rows = slice(0, blk) if first else pl.ds(pl.multiple_of(n * blk, blk), blk)
        for hp in range(DIL_PAIRS_PER_STEP):
            cs = slice(hp * LANES, (hp + 1) * LANES)
            q2 = q_ref[r, q_rows, cs]
            k2 = window(k_ref, r, n, first, cs)
            for hh in range(2):
                h = 2 * hp + hh
                qh = jnp.where(first_head if hh == 0 else jnp.logical_not(first_head), q2, 0)
                bias = bias_ref[h, :, blk:] if first else bias_ref[h]
                s_ref[slot, h, :, 0:kw] = _dot_nt(qh, k2) + bias

    def attend(i, slot, first):
        r, n = where_unit(i, first)
        kw = blk if first else 2 * blk
        t0 = n * (blk * dilation) + r
        rows = pl.ds(t0, blk) if dilation == 1 else pl.ds(t0, blk, stride=dilation)
        stat_tile = jnp.ones((blk, LANES), F32)
        for hp in range(DIL_PAIRS_PER_STEP):
            cs = slice(hp * LANES, (hp + 1) * LANES)
            v2 = window(v_ref, r, n, first, cs)
            accs = []
            for hh in range(2):
                s = s_ref[slot, 2 * hp + hh, :, 0:kw]
                m = jnp.max(s, axis=-1, keepdims=True)
                key_lane = lax.broadcasted_iota(jnp.int32, (kw, LANES), 1)
                own = key_lane < DIL_HEAD_DIM if hh == 0 else key_lane >= DIL_HEAD_DIM
                acc = _dot(jnp.exp2(s - m).astype(BF16), jnp.where(own, v2, 1))
                sum_lane = head0 + _stat_lane(2 * hp + hh)
                stat_tile = jnp.where(lane == sum_lane, acc, stat_tile)
                stat_tile = jnp.where(lane == sum_lane + STAT_MAX_SHIFT, m, stat_tile)
                accs.append(acc)
            acc_ref[hp, rows, :] = jnp.where(first_head, accs[0], accs[1])
        stat_ref[rows, :] = stat_tile

    def run(count, first):
        trips, rest = divmod(count, DIL_UNITS_PER_TRIP)

        def units(base, n, more):
            for u in range(n):
                if u + 1 < n or more:
                    logits(base + u + 1, (u + 1) % 2, first)
                attend(base + u, u % 2, first)

        logits(0, 0, first)
        if trips:
            def body(j, carry):
                units(j * DIL_UNITS_PER_TRIP, DIL_UNITS_PER_TRIP, True)
                return carry
            lax.fori_loop(0, trips - (0 if rest else 1), body, 0)
            if not rest:
                units((trips - 1) * DIL_UNITS_PER_TRIP, DIL_UNITS_PER_TRIP, False)
        if rest:
            units(trips * DIL_UNITS_PER_TRIP, rest, False)

    run(dilation, True)
    if n_blocks > 1:
        run(dilation * (n_blocks - 1), False)
    for hp in range(DIL_PAIRS_PER_STEP):
        o_ref[:, hp * LANES:(hp + 1) * LANES] = acc_ref[hp].astype(BF16)


def _dil_attn(proj, bias, group):
    _, dilation = DIL_GROUPS[group]
    length = SEQ // dilation
    halves = DIL_HEADS // DIL_HEADS_PER_STEP
    width = DIL_HEADS_PER_STEP * DIL_HEAD_DIM
    qkv_spec = lambda kind: pl.BlockSpec(
        (None, dilation, length, width), lambda b, u: (b, 0, 0, kind * halves + u))
    return pl.pallas_call(
        functools.partial(_dil_attn_kernel, dilation=dilation, n_blocks=length // DIL_BLOCK),
        grid=(BATCH, halves),
        in_specs=[
            qkv_spec(0), qkv_spec(1), qkv_spec(2),
            pl.BlockSpec((DIL_HEADS_PER_STEP, DIL_BLOCK, 2 * DIL_BLOCK),
                         lambda b, u: (group * halves + u, 0, 0)),
        ],
        out_specs=[
            pl.BlockSpec((None, SEQ, width), lambda b, u: (b, 0, u)),
            pl.BlockSpec((None, SEQ, LANES), lambda b, u: (b, 0, u)),
        ],
        out_shape=[
            jax.ShapeDtypeStruct((BATCH, SEQ, DIL_WIDTH), BF16),
            jax.ShapeDtypeStruct((BATCH, SEQ, halves * LANES), F32),
        ],
        scratch_shapes=[pltpu.VMEM((2, DIL_HEADS_PER_STEP, DIL_BLOCK, 2 * DIL_BLOCK), F32),
                        pltpu.VMEM((DIL_PAIRS_PER_STEP, SEQ, LANES), F32)],
        compiler_params=_params("arbitrary", "arbitrary"),
        name=f"dil_attn_g{group}",
    )(proj, proj, proj, bias)


def _dil_out_kernel(x_ref, o0_ref, o1_ref, o2_ref, s0_ref, s1_ref, s2_ref, e_ref, wo_ref,
                    gate_ref, post_ref, out_ref):
    lane = lax.broadcasted_iota(jnp.int32, (1, LANES), 1)
    in_first_block = functools.reduce(jnp.logical_or, [
        jnp.logical_or(lane == _stat_lane(h), lane == _stat_lane(h) + STAT_MAX_SHIFT)
        for h in range(DIL_HEADS_PER_STEP)])
    e = e_ref[...]
    for r0 in range(0, x_ref.shape[0], OUT_SUB):
        rows = slice(r0, r0 + OUT_SUB)
        sums = [jnp.where(in_first_block, r[rows, :LANES], r[rows, LANES:]) for r in (s0_ref, s1_ref, s2_ref)]
        maxes = [pltpu.roll(s, LANES - STAT_MAX_SHIFT, 1) for s in sums]
        m = jnp.maximum(jnp.maximum(maxes[0], maxes[1]), maxes[2])
        es = [jnp.exp2(mg - m) for mg in maxes]
        inv = 1.0 / (sums[0] * es[0] + sums[1] * es[1] + sums[2] * es[2])
        mix = None
        for eg, o_ref in zip(es, (o0_ref, o1_ref, o2_ref)):
            alpha = eg * inv
            hi = alpha.astype(BF16)
            lo = (alpha - hi.astype(F32)).astype(BF16)
            term = _dot(jnp.concatenate([hi, lo], axis=1), e) * o_ref[rows, :].astype(F32)
            mix = term if mix is None else mix + term
        y = _dot(mix.astype(BF16), wo_ref[...])
        out_ref[rows, :] = x_ref[rows, :] + gate_ref[...] * _rms(y, post_ref[...])


def _dil_out(x, outs, stats, mod, post, w_o, layer):
    tm = TM_OUT
    tiles_per_batch = SEQ // tm
    halves = DIL_HEADS // DIL_HEADS_PER_STEP
    stat_lane_of_col = jnp.array([_stat_lane(c // DIL_HEAD_DIM) for c in range(DIL_WIDTH)], jnp.int32)
    lane_of_row = jnp.arange(2 * LANES) % LANES
    expand = (lane_of_row[:, None] == stat_lane_of_col[None, :]).astype(BF16)
    batch_rows = lambda width: pl.BlockSpec(
        (None, tm, width), lambda t: (t // tiles_per_batch, t % tiles_per_batch, 0))
    tok = pl.BlockSpec((tm, D_MODEL), lambda t: (t, 0))
    return pl.pallas_call(
        _dil_out_kernel,
        grid=(TOKENS // tm,),
        in_specs=[
            tok, *([batch_rows(DIL_WIDTH)] * 3), *([batch_rows(halves * LANES)] * 3),
            pl.BlockSpec(expand.shape, lambda t: (0, 0)), pl.BlockSpec(w_o.shape, lambda t: (0, 0)),
            _mod_spec(layer, 1, 2, tm), _row_spec(layer * N_SUB + 1),
        ],
        out_specs=tok,
        out_shape=jax.ShapeDtypeStruct((TOKENS, D_MODEL), F32),
        compiler_params=_params("arbitrary"),
        name="dil_out",
    )(x, *outs, *stats, expand, w_o, mod, post)


def kernel(x, c, norm_pre, norm_post, w_mod, b_mod, ffn_w_gate, ffn_w_up, ffn_w_down,
           mla_w_in, mla_q_norm, mla_w_q_up, mla_kv_norm, mla_w_kv_up, mla_w_o,
           dil_w_in, dil_w_o, rel_bias):
    assert x.shape == (BATCH, SEQ, D_MODEL) and x.dtype == F32
    assert DEPTH == 2
    h = x.reshape(TOKENS, D_MODEL)
    mod = _modulation(c, w_mod, b_mod)
    pre = norm_pre.reshape(DEPTH * N_SUB, 1, D_MODEL)
    post = norm_post.reshape(DEPTH * N_SUB, 1, D_MODEL)
    tables = _rope_tables()
    bias = _bias_tables(rel_bias)
    ffn_f32 = [w.reshape(DEPTH * 2, *w.shape[2:]) for w in (ffn_w_gate, ffn_w_up, ffn_w_down)]

    h = _ffn(h, mod, pre, post, (*[w[:1].astype(BF16) for w in ffn_f32], 0), 0, 0)
    win, wq, wk, wv = _mla_weights(mla_w_in[0], mla_w_q_up[0], mla_w_kv_up[0])
    q, k, v = _mla_proj(h, mod, pre, win, mla_q_norm[0][None, :], mla_kv_norm[0][None, :],
                        wq, wk, wv, tables, 0)
    casts = [(ffn_f32[0], 1, 64), (ffn_f32[1], 1, 64), (ffn_f32[2], 1, 256), (dil_w_in, 0, 16)]
    att, (wg, wu, wd, dil_in) = _mla_attn(q, k, v, casts)
    h = _ffn(h, mod, pre, post, (wg, wu, wd, 0), 0, 1, mla=(att, mla_w_o[0].astype(BF16)))

    h = _ffn(h, mod, pre, post, (wg, wu, wd, 1), 1, 0)
    outs, stats = zip(*[_dil_attn(_dil_proj(h, mod, pre, dil_in[0], 1, g), bias, g) for g in range(N_GROUPS)])
    h = _dil_out(h, outs, stats, mod, post, dil_w_o[0].astype(BF16), 1)
    h = _ffn(h, mod, pre, post, (wg, wu, wd, 2), 1, 1)
    return h.reshape(BATCH, SEQ, D_MODEL)
```

```python
import functools
import math

import jax
import jax.numpy as jnp
import numpy as np
from jax import lax
from jax.experimental import pallas as pl
from jax.experimental.pallas import tpu as pltpu

F32 = jnp.float32
BF16 = jnp.bfloat16

D_MODEL = 1024
BATCH = 8
SEQ = 2048
DEPTH = 2
N_SUB = 3
D_FF = 2816
FFN_RES = 0.5
EPS = 1e-6

MLA_HEADS = 16
Q_LORA = 384
KV_LORA = 256
QK_NOPE = 64
QK_ROPE = 32
V_HEAD = 64
ROPE_THETA = 10000.0

DIL_GROUPS = ((128, 1), (512, 4), (2048, 16))
N_GROUPS = 3
DIL_HEADS = 16
DIL_HEAD_DIM = 64
DIL_BLOCK = 128
DIL_WIDTH = DIL_HEADS * DIL_HEAD_DIM
N_BUCKETS = 32
MAX_DISTANCE = 2048

TOKENS = BATCH * SEQ
LANES = 128
HEAD_SLOT = 128
ROPE_HALF = QK_ROPE // 2
MASK_VALUE = -1e30
LOG2_E = math.log2(math.e)
VMEM_LIMIT = 56 * 1024 * 1024

TM_FFN = 1024
FFN_SUB = 256
TM_PROJ = 1024
PROJ_SUB = 256
TM_DIL_PROJ = 1024
TM_OUT = 1024
OUT_SUB = 256
TQ_MLA = 256
MLA_SUM_ROWS = 16
MLA_LOOKAHEAD = 6
TN_MOD = 1536
FF_CHUNK = 1536


def _params(*sem):
    return pltpu.CompilerParams(dimension_semantics=sem, vmem_limit_bytes=VMEM_LIMIT)


def _rms(x, g):
    ms = jnp.mean(x * x, axis=-1, keepdims=True)
    return x * lax.rsqrt(ms + EPS) * g


def _dot(a, b):
    return jnp.dot(a, b, preferred_element_type=F32)


def _dot_nt(a, b):
    return lax.dot_general(a, b, (((1,), (1,)), ((), ())), preferred_element_type=F32)


def _silu(x):
    return x * jax.nn.sigmoid(x)


def _mod_spec(layer, sub, kind, tm):
    tiles_per_batch = SEQ // tm
    return pl.BlockSpec(
        (None, 1, D_MODEL),
        lambda t, *_: ((layer * BATCH + t // tiles_per_batch) * (N_SUB * 3) + sub * 3 + kind, 0, 0))


def _row_spec(index):
    return pl.BlockSpec((None, 1, D_MODEL), lambda *_: (index, 0, 0))


def _mod_kernel(c_ref, w_ref, b_ref, o_ref):
    h = _silu(c_ref[...])
    hi = h.astype(BF16)
    lo = (h - hi.astype(F32)).astype(BF16)
    y = _dot(jnp.concatenate([hi, lo], axis=0), w_ref[...].astype(BF16))
    o_ref[...] = y[:BATCH] + y[BATCH:] + b_ref[...]


def _modulation(c, w_mod, b_mod):
    n = N_SUB * 3 * D_MODEL
    out = pl.pallas_call(
        _mod_kernel,
        grid=(DEPTH, n // TN_MOD),
        in_specs=[
            pl.BlockSpec((BATCH, D_MODEL), lambda i, j: (0, 0)),
            pl.BlockSpec((None, D_MODEL, TN_MOD), lambda i, j: (i, 0, j)),
            pl.BlockSpec((None, 1, TN_MOD), lambda i, j: (i, 0, j)),
        ],
        out_specs=pl.BlockSpec((None, BATCH, TN_MOD), lambda i, j: (i, 0, j)),
        out_shape=jax.ShapeDtypeStruct((DEPTH, BATCH, n), F32),
        compiler_params=_params("arbitrary", "arbitrary"),
        name="modulation",
    )(c, w_mod, b_mod.reshape(DEPTH, 1, n))
    return out.reshape(DEPTH * BATCH * N_SUB * 3, 1, D_MODEL)


def _mla_mix(rows, att_ref, wo_ref):
    return _dot(att_ref[rows, :], wo_ref[...])


def _ffn_kernel(x_ref, *refs, after_mla):
    if after_mla:
        att_ref, wo_ref, mgate_ref, mpost_ref, *refs = refs
    shift_ref, scale_ref, gate_ref, pre_ref, post_ref, wg_ref, wu_ref, wd_ref, o_ref, a_ref = refs
    tm = x_ref.shape[0]
    subs = [slice(r0, r0 + FFN_SUB) for r0 in range(0, tm, FFN_SUB)]
    hns = []
    for rows in subs:
        x = x_ref[rows, :]
        if after_mla:
            x = x + mgate_ref[...] * _rms(_mla_mix(rows, att_ref, wo_ref), mpost_ref[...])
            o_ref[rows, :] = x
        hns.append((_rms(x, pre_ref[...]) * (1.0 + scale_ref[...]) + shift_ref[...]).astype(BF16))
    for rows, hn in zip(subs, hns):
        for c0 in range(0, D_FF, FF_CHUNK):
            c1 = min(c0 + FF_CHUNK, D_FF)
            g = _dot(hn, wg_ref[:, c0:c1])
            u = _dot(hn, wu_ref[:, c0:c1])
            a_ref[rows, c0:c1] = (_silu(g) * u).astype(BF16)
    for rows in subs:
        y = _dot(a_ref[rows, :], wd_ref[...])
        x = o_ref[rows, :] if after_mla else x_ref[rows, :]
        o_ref[rows, :] = x + FFN_RES * gate_ref[...] * _rms(y, post_ref[...])


def _ffn(x, mod, pre, post, weights, layer, which, mla=None):
    wg, wu, wd, w_set = weights
    sub = 0 if which == 0 else 2
    tm = TM_FFN
    w_in_spec = pl.BlockSpec((None, D_MODEL, D_FF), lambda t: (w_set, 0, 0), pipeline_mode=pl.Buffered(1))
    w_out_spec = pl.BlockSpec((None, D_FF, D_MODEL), lambda t: (w_set, 0, 0), pipeline_mode=pl.Buffered(1))
    mla_args, mla_specs = [], []
    if mla is not None:
        att, w_o = mla
        mla_args = [att, w_o, mod, post]
        mla_specs = [pl.BlockSpec((tm, att.shape[1]), lambda t: (t, 0)),
                     pl.BlockSpec(w_o.shape, lambda t: (0, 0)),
                     _mod_spec(layer, 1, 2, tm), _row_spec(layer * N_SUB + 1)]
    return pl.pallas_call(
        functools.partial(_ffn_kernel, after_mla=mla is not None),
        grid=(TOKENS // tm,),
        in_specs=[
            pl.BlockSpec((tm, D_MODEL), lambda t: (t, 0)),
            *mla_specs,
            _mod_spec(layer, sub, 0, tm), _mod_spec(layer, sub, 1, tm), _mod_spec(layer, sub, 2, tm),
            _row_spec(layer * N_SUB + sub), _row_spec(layer * N_SUB + sub),
            w_in_spec, w_in_spec, w_out_spec,
        ],
        out_specs=pl.BlockSpec((tm, D_MODEL), lambda t: (t, 0)),
        out_shape=jax.ShapeDtypeStruct((TOKENS, D_MODEL), F32),
        scratch_shapes=[pltpu.VMEM((tm, D_FF), BF16)],
        compiler_params=_params("arbitrary"),
        name="ffn" if mla is None else "mla_out_ffn",
    )(x, *mla_args, mod, mod, mod, pre, post, wg, wu, wd)


def _rope(z, ta, tb):
    return z * ta + pltpu.roll(z, HEAD_SLOT // 2, 1) * tb


def _mla_proj_kernel(x_ref, shift_ref, scale_ref, pre_ref, win_ref, qn_ref, kvn_ref,
                     wq_ref, wk_ref, wv_ref, ta_ref, tb_ref, q_ref, k_ref, v_ref):
    tm = x_ref.shape[0]
    subs = [slice(r0, r0 + PROJ_SUB) for r0 in range(0, tm, PROJ_SUB)]
    lats = []
    for rows in subs:
        hn = (_rms(x_ref[rows, :], pre_ref[...]) * (1.0 + scale_ref[...]) + shift_ref[...]).astype(BF16)
        lats.append(_dot(hn, win_ref[...]))
    for rows, lat in zip(subs, lats):
        cq = _rms(lat[:, :Q_LORA], qn_ref[...]).astype(BF16)
        ckv = _rms(lat[:, Q_LORA:Q_LORA + KV_LORA], kvn_ref[...]).astype(BF16)
        ta, tb = ta_ref[rows, :], tb_ref[rows, :]
        kr = _rope(lat[:, Q_LORA + KV_LORA:], ta, tb)
        q = _dot(cq, wq_ref[...]) * ((QK_NOPE + QK_ROPE) ** -0.5 * LOG2_E)
        kn = _dot(ckv, wk_ref[...])
        v_ref[rows, :] = _dot(ckv, wv_ref[...]).astype(BF16)
        for h in range(MLA_HEADS):
            sl = slice(h * HEAD_SLOT, (h + 1) * HEAD_SLOT)
            q_ref[rows, sl] = _rope(q[:, sl], ta, tb).astype(BF16)
            k_ref[rows, sl] = (kn[:, sl] + kr).astype(BF16)


def _head_slot(nope, rope, like):
    split = HEAD_SLOT // 2 - ROPE_HALF
    zeros = lambda n: jnp.zeros(like.shape[:-1] + (n,), like.dtype)
    nope = zeros(QK_NOPE) if nope is None else nope
    rope = zeros(QK_ROPE) if rope is None else rope
    return jnp.concatenate([rope[..., :ROPE_HALF], nope[..., :split], rope[..., ROPE_HALF:], nope[..., split:],
                            zeros(HEAD_SLOT - QK_NOPE - QK_ROPE)], axis=-1)


def _rope_tables():
    pos = jnp.arange(SEQ, dtype=F32)
    freqs = ROPE_THETA ** (-jnp.arange(ROPE_HALF, dtype=F32) / ROPE_HALF)
    ang = pos[:, None] * freqs[None, :]
    cos, sin = jnp.cos(ang), jnp.sin(ang)
    ones = jnp.ones((SEQ, QK_NOPE), F32)
    ta = _head_slot(ones, jnp.concatenate([cos, cos], axis=1), cos)
    tb = _head_slot(None, jnp.concatenate([-sin, sin], axis=1), sin)
    return ta, tb


def _mla_weights(w_in, w_q_up, w_kv_up):
    win = jnp.concatenate([w_in[:, :Q_LORA + KV_LORA],
                           _head_slot(None, w_in[:, Q_LORA + KV_LORA:], w_in)], axis=1).astype(BF16)
    wq = w_q_up.reshape(Q_LORA, MLA_HEADS, QK_NOPE + QK_ROPE)
    wq = _head_slot(wq[..., :QK_NOPE], wq[..., QK_NOPE:], wq)
    wq = wq.reshape(Q_LORA, MLA_HEADS * HEAD_SLOT).astype(BF16)
    wkv = w_kv_up.reshape(KV_LORA, MLA_HEADS, QK_NOPE + V_HEAD)
    wk = _head_slot(wkv[..., :QK_NOPE], None, wkv).reshape(KV_LORA, MLA_HEADS * HEAD_SLOT).astype(BF16)
    wv = wkv[..., QK_NOPE:].reshape(KV_LORA, MLA_HEADS * V_HEAD).astype(BF16)
    return win, wq, wk, wv


def _mla_proj(x, mod, pre, win, qn, kvn, wq, wk, wv, tables, layer):
    tm = TM_PROJ
    tiles_per_batch = SEQ // tm
    full = lambda a: pl.BlockSpec(a.shape, lambda t: (0,) * a.ndim)
    tab_spec = pl.BlockSpec((tm, HEAD_SLOT), lambda t: (t % tiles_per_batch, 0))
    wide = MLA_HEADS * HEAD_SLOT
    return pl.pallas_call(
        _mla_proj_kernel,
        grid=(TOKENS // tm,),
        in_specs=[
            pl.BlockSpec((tm, D_MODEL), lambda t: (t, 0)),
            _mod_spec(layer, 1, 0, tm), _mod_spec(layer, 1, 1, tm),
            _row_spec(layer * N_SUB + 1),
            full(win), full(qn), full(kvn), full(wq), full(wk), full(wv),
            tab_spec, tab_spec,
        ],
        out_specs=[
            pl.BlockSpec((tm, wide), lambda t: (t, 0)),
            pl.BlockSpec((tm, wide), lambda t: (t, 0)),
            pl.BlockSpec((tm, MLA_HEADS * V_HEAD), lambda t: (t, 0)),
        ],
        out_shape=[
            jax.ShapeDtypeStruct((TOKENS, wide), BF16),
            jax.ShapeDtypeStruct((TOKENS, wide), BF16),
            jax.ShapeDtypeStruct((TOKENS, MLA_HEADS * V_HEAD), BF16),
        ],
        compiler_params=_params("arbitrary"),
        name="mla_proj",
    )(x, mod, mod, pre, win, qn, kvn, wq, wk, wv, *tables)


def _mla_attn_kernel(q_ref, k_ref, v_ref, *refs, n_cast):
    cast_in, (o_ref, *cast_out), (vt_ref, ot_ref) = (
        refs[:n_cast], refs[n_cast:2 * n_cast + 1], refs[2 * n_cast + 1:])
    tq = TQ_MLA
    key = lax.broadcasted_iota(jnp.int32, (tq, tq), 0)
    query = lax.broadcasted_iota(jnp.int32, (tq, tq), 1)
    causal = key <= query
    vt_ref[:2 * V_HEAD, :] = v_ref[...].astype(F32).T.astype(BF16)
    vt_ref[2 * V_HEAD:, :] = jnp.ones((MLA_SUM_ROWS, SEQ), BF16)

    def logits(hh, qi, c):
        hs = slice(hh * HEAD_SLOT, (hh + 1) * HEAD_SLOT)
        s = _dot_nt(k_ref[c * tq:(c + 1) * tq, hs], q_ref[qi * tq:(qi + 1) * tq, hs])
        return jnp.where(causal, s, MASK_VALUE) if c == qi else s

    chunks = [(hh, qi, c) for hh in range(2) for qi in range(SEQ // tq) for c in range(qi + 1)]
    queued = [logits(*ch) for ch in chunks[:MLA_LOOKAHEAD]]
    for i, (hh, qi, c) in enumerate(chunks):
        if i + MLA_LOOKAHEAD < len(chunks):
            queued.append(logits(*chunks[i + MLA_LOOKAHEAD]))
        s = queued.pop(0)
        m_chunk = jnp.max(s, axis=0, keepdims=True)
        if c == 0:
            m = m_chunk
            acc = _dot(vt_ref[:, 0:tq], jnp.exp2(s - m).astype(BF16))
        else:
            m_new = jnp.maximum(m, m_chunk)
            acc = acc * jnp.exp2(m - m_new) + _dot(vt_ref[:, c * tq:(c + 1) * tq],
                                                   jnp.exp2(s - m_new).astype(BF16))
            m = m_new
        if c == qi:
            rows = slice(hh * V_HEAD, (hh + 1) * V_HEAD)
            ot_ref[rows, qi * tq:(qi + 1) * tq] = acc[rows, :] * (1.0 / acc[2 * V_HEAD:2 * V_HEAD + 1, :])
    o_ref[...] = ot_ref[...].T.astype(BF16)
    for src_ref, dst_ref in zip(cast_in, cast_out):
        dst_ref[...] = src_ref[...].astype(BF16)


def _mla_attn(q, k, v, casts):
    pairs = MLA_HEADS // 2
    cast_args, cast_in_specs, cast_out_specs, cast_shapes = [], [], [], []
    for array, first_set, block_rows in casts:
        sets, rows, cols = array.shape
        first_block, n_blocks = first_set * rows // block_rows, (sets - first_set) * rows // block_rows
        assert first_set * rows % block_rows == 0 and rows % block_rows == 0 and n_blocks <= BATCH * pairs

        def block(b, p, first=0, n_blocks=n_blocks):
            return first + jnp.minimum(b * pairs + p, n_blocks - 1), 0
        cast_args.append(array.reshape(sets * rows, cols))
        cast_in_specs.append(pl.BlockSpec((block_rows, cols), functools.partial(block, first=first_block)))
        cast_out_specs.append(pl.BlockSpec((block_rows, cols), block))
        cast_shapes.append(jax.ShapeDtypeStruct(((sets - first_set) * rows, cols), BF16))
    att, *copies = pl.pallas_call(
        functools.partial(_mla_attn_kernel, n_cast=len(casts)),
        grid=(BATCH, pairs),
        in_specs=[
            pl.BlockSpec((SEQ, 2 * HEAD_SLOT), lambda b, p: (b, p)),
            pl.BlockSpec((SEQ, 2 * HEAD_SLOT), lambda b, p: (b, p)),
            pl.BlockSpec((SEQ, 2 * V_HEAD), lambda b, p: (b, p)),
            *cast_in_specs,
        ],
        out_specs=[pl.BlockSpec((SEQ, 2 * V_HEAD), lambda b, p: (b, p)), *cast_out_specs],
        out_shape=[jax.ShapeDtypeStruct((TOKENS, MLA_HEADS * V_HEAD), BF16), *cast_shapes],
        scratch_shapes=[pltpu.VMEM((2 * V_HEAD + MLA_SUM_ROWS, SEQ), BF16),
                        pltpu.VMEM((2 * V_HEAD, SEQ), F32)],
        compiler_params=_params("arbitrary", "arbitrary"),
        name="mla_attn",
    )(q, k, v, *cast_args)
    return att, [copy.reshape(-1, *array.shape[1:]) for copy, (array, _, _) in zip(copies, casts)]


def _t5_bucket(dist):
    max_exact = N_BUCKETS // 2
    d = np.maximum(dist, 1).astype(np.float32)
    scale = np.float32(math.log(MAX_DISTANCE / max_exact))
    large = max_exact + (np.log(d / np.float32(max_exact)) / scale
                         * np.float32(N_BUCKETS - max_exact)).astype(np.int32)
    large = np.minimum(large, N_BUCKETS - 1)
    return np.where(dist < max_exact, dist, large)


def _bias_selectors():
    j = np.arange(2 * DIL_BLOCK)
    rel = DIL_BLOCK - j
    sel = []
    for window, dilation in DIL_GROUPS:
        assert window // dilation == DIL_BLOCK
        bucket = _t5_bucket(np.maximum(rel, 0) * dilation)
        onehot = (bucket[None, :] == np.arange(N_BUCKETS)[:, None]) & (rel >= 0)[None, :]
        sel.append(onehot.astype(np.float32))
    return jnp.asarray(np.stack(sel))


def _bias_kernel(rbt_ref, sel_ref, o_ref):
    rbt = rbt_ref[...]
    sel = sel_ref[...]
    row0 = jnp.zeros((DIL_HEADS, 2 * DIL_BLOCK), F32)
    for b in range(N_BUCKETS):
        row0 = row0 + rbt[:, b:b + 1] * sel[b:b + 1, :]
    future = jnp.sum(sel, axis=0, keepdims=True) < 0.5
    row0 = jnp.where(future, MASK_VALUE, row0 * LOG2_E)
    for h in range(DIL_HEADS):
        t = jnp.broadcast_to(row0[h:h + 1, :], (DIL_BLOCK, 2 * DIL_BLOCK))
        o_ref[h] = pltpu.roll(t, 0, 1, stride=1, stride_axis=0)


def _bias_tables(rel_bias):
    rbt = rel_bias.T.reshape(N_GROUPS, DIL_HEADS, N_BUCKETS)
    return pl.pallas_call(
        _bias_kernel,
        grid=(N_GROUPS,),
        in_specs=[
            pl.BlockSpec((None, DIL_HEADS, N_BUCKETS), lambda g: (g, 0, 0)),
            pl.BlockSpec((None, N_BUCKETS, 2 * DIL_BLOCK), lambda g: (g, 0, 0)),
        ],
        out_specs=pl.BlockSpec((DIL_HEADS, DIL_BLOCK, 2 * DIL_BLOCK), lambda g: (g, 0, 0)),
        out_shape=jax.ShapeDtypeStruct((N_GROUPS * DIL_HEADS, DIL_BLOCK, 2 * DIL_BLOCK), F32),
        compiler_params=_params("arbitrary"),
        name="dil_bias",
    )(rbt, _bias_selectors())


def _dil_proj_kernel(x_ref, shift_ref, scale_ref, pre_ref, w_ref, o_ref, *scratch, dilation):
    tm = x_ref.shape[0]
    per = PROJ_SUB // dilation
    slabs = D_MODEL // LANES
    hbs = []
    for i, r0 in enumerate(range(0, tm, PROJ_SUB)):
        hn = _rms(x_ref[r0:r0 + PROJ_SUB, :], pre_ref[...]) * (1.0 + scale_ref[...]) + shift_ref[...]
        if dilation == 1:
            hbs.append(hn.astype(BF16))
            continue
        slab_ref, = scratch
        for s in range(slabs):
            slab_ref[i, s] = hn[:, s * LANES:(s + 1) * LANES]
        hbs.append(jnp.concatenate(
            [jnp.concatenate([slab_ref[i, s, pl.ds(r, per, stride=dilation), :] for s in range(slabs)],
                             axis=1).astype(BF16)
             for r in range(dilation)], axis=0))
    for i, hb in enumerate(hbs):
        for j in range(3):
            cols = slice(j * DIL_WIDTH, (j + 1) * DIL_WIDTH)
            y = _dot(hb, w_ref[:, cols])
            if j == 0:
                y = y * (DIL_HEAD_DIM ** -0.5 * LOG2_E)
            y = y.astype(BF16)
            for r in range(dilation):
                o_ref[r, i * per:(i + 1) * per, cols] = y[r * per:(r + 1) * per, :]


def _dil_proj(x, mod, pre, w, layer, group):
    _, dilation = DIL_GROUPS[group]
    tm = TM_DIL_PROJ
    tiles_per_batch = SEQ // tm
    sub = tm // dilation
    scratch = [] if dilation == 1 else [
        pltpu.VMEM((tm // PROJ_SUB, D_MODEL // LANES, PROJ_SUB, LANES), F32)]
    return pl.pallas_call(
        functools.partial(_dil_proj_kernel, dilation=dilation),
        grid=(TOKENS // tm,),
        in_specs=[
            pl.BlockSpec((tm, D_MODEL), lambda t: (t, 0)),
            _mod_spec(layer, 1, 0, tm), _mod_spec(layer, 1, 1, tm),
            _row_spec(layer * N_SUB + 1),
            pl.BlockSpec((D_MODEL, 3 * DIL_WIDTH), lambda t: (0, group), pipeline_mode=pl.Buffered(1)),
        ],
        out_specs=pl.BlockSpec((None, dilation, sub, 3 * DIL_WIDTH),
                               lambda t: (t // tiles_per_batch, 0, t % tiles_per_batch, 0)),
        out_shape=jax.ShapeDtypeStruct((BATCH, dilation, SEQ // dilation, 3 * DIL_WIDTH), BF16),
        scratch_shapes=scratch,
        compiler_params=_params("arbitrary"),
        name=f"dil_proj_g{group}",
    )(x, mod, mod, pre, w)


DIL_HEADS_PER_STEP = 8
DIL_PAIRS_PER_STEP = DIL_HEADS_PER_STEP // 2
DIL_UNITS_PER_TRIP = 8


STAT_MAX_SHIFT = 32


def _stat_lane(head):
    return head if head % 2 else DIL_HEAD_DIM + head


def _dil_attn_kernel(q_ref, k_ref, v_ref, bias_ref, o_ref, stat_ref, s_ref, acc_ref, *, dilation, n_blocks):
    blk = DIL_BLOCK
    lane = lax.broadcasted_iota(jnp.int32, (blk, LANES), 1)
    first_head = lane < DIL_HEAD_DIM
    head0 = pl.program_id(1) * DIL_HEADS_PER_STEP

    def where_unit(i, first):
        if first:
            return i, 0
        later = n_blocks - 1
        if dilation == 1:
            return 0, i + 1
        i = jnp.asarray(i, jnp.int32)
        return lax.div(i, jnp.int32(later)), lax.rem(i, jnp.int32(later)) + 1

    def window(ref, r, n, first, cs):
        if first:
            return ref[r, 0:blk, cs]
        return ref[r, pl.ds(pl.multiple_of(n * blk - blk, blk), 2 * blk), cs]

    def logits(i, slot, first):
        r, n = where_unit(i, first)
        kw = blk if first else 2 * blk
        q_rows = slice(0, blk) if first else pl.ds(pl.multiple_of(n * blk, blk), blk)
        for hp in range(DIL_PAIRS_PER_STEP):
            cs = slice(hp * LANES, (hp + 1) * LANES)
            q2 = q_ref[r, q_rows, cs]
            k2 = window(k_ref, r, n, first, cs)
            for hh in range(2):
                h = 2 * hp + hh
                qh = jnp.where(first_head if hh == 0 else jnp.logical_not(first_head), q2, 0)
                bias = bias_ref[h, :, blk:] if first else bias_ref[h]
                s_ref[slot, h, :, 0:kw] = _dot_nt(qh, k2) + bias

    def attend(i, slot, first):
        r, n = where_unit(i, first)
        kw = blk if first else 2 * blk
        t0 = n * (blk * dilation) + r
        rows = pl.ds(t0, blk) if dilation == 1 else pl.ds(t0, blk, stride=dilation)
        stat_tile = jnp.ones((blk, LANES), F32)
        for hp in range(DIL_PAIRS_PER_STEP):
            cs = slice(hp * LANES, (hp + 1) * LANES)
            v2 = window(v_ref, r, n, first, cs)
            accs = []
            for hh in range(2):
                s = s_ref[slot, 2 * hp + hh, :, 0:kw]
                m = jnp.max(s, axis=-1, keepdims=True)
                key_lane = lax.broadcasted_iota(jnp.int32, (kw, LANES), 1)
                own = key_lane < DIL_HEAD_DIM if hh == 0 else key_lane >= DIL_HEAD_DIM
                acc = _dot(jnp.exp2(s - m).astype(BF16), jnp.where(own, v2, 1))
                sum_lane = head0 + _stat_lane(2 * hp + hh)
                stat_tile = jnp.where(lane == sum_lane, acc, stat_tile)
                stat_tile = jnp.where(lane == sum_lane + STAT_MAX_SHIFT, m, stat_tile)
                accs.append(acc)
            acc_ref[hp, rows, :] = jnp.where(first_head, accs[0], accs[1])
        stat_ref[rows, :] = stat_tile

    def run(count, first):
        trips, rest = divmod(count, DIL_UNITS_PER_TRIP)

        def units(base, n, more):
            for u in range(n):
                if u + 1 < n or more:
                    logits(base + u + 1, (u + 1) % 2, first)
                attend(base + u, u % 2, first)

        logits(0, 0, first)
        if trips:
            def body(j, carry):
                units(j * DIL_UNITS_PER_TRIP, DIL_UNITS_PER_TRIP, True)
                return carry
            lax.fori_loop(0, trips - (0 if rest else 1), body, 0)
            if not rest:
                units((trips - 1) * DIL_UNITS_PER_TRIP, DIL_UNITS_PER_TRIP, False)
        if rest:
            units(trips * DIL_UNITS_PER_TRIP, rest, False)

    run(dilation, True)
    if n_blocks > 1:
        run(dilation * (n_blocks - 1), False)
    for hp in range(DIL_PAIRS_PER_STEP):
        o_ref[:, hp * LANES:(hp + 1) * LANES] = acc_ref[hp].astype(BF16)


def _dil_attn(proj, bias, group):
    _, dilation = DIL_GROUPS[group]
    length = SEQ // dilation
    halves = DIL_HEADS // DIL_HEADS_PER_STEP
    width = DIL_HEADS_PER_STEP * DIL_HEAD_DIM
    qkv_spec = lambda kind: pl.BlockSpec(
        (None, dilation, length, width), lambda b, u: (b, 0, 0, kind * halves + u))
    return pl.pallas_call(
        functools.partial(_dil_attn_kernel, dilation=dilation, n_blocks=length // DIL_BLOCK),
        grid=(BATCH, halves),
        in_specs=[
            qkv_spec(0), qkv_spec(1), qkv_spec(2),
            pl.BlockSpec((DIL_HEADS_PER_STEP, DIL_BLOCK, 2 * DIL_BLOCK),
                         lambda b, u: (group * halves + u, 0, 0)),
        ],
        out_specs=[
            pl.BlockSpec((None, SEQ, width), lambda b, u: (b, 0, u)),
            pl.BlockSpec((None, SEQ, LANES), lambda b, u: (b, 0, u)),
        ],
        out_shape=[
            jax.ShapeDtypeStruct((BATCH, SEQ, DIL_WIDTH), BF16),
            jax.ShapeDtypeStruct((BATCH, SEQ, halves * LANES), F32),
        ],
        scratch_shapes=[pltpu.VMEM((2, DIL_HEADS_PER_STEP, DIL_BLOCK, 2 * DIL_BLOCK), F32),
                        pltpu.VMEM((DIL_PAIRS_PER_STEP, SEQ, LANES), F32)],
        compiler_params=_params("arbitrary", "arbitrary"),
        name=f"dil_attn_g{group}",
    )(proj, proj, proj, bias)


def _dil_out_kernel(x_ref, o0_ref, o1_ref, o2_ref, s0_ref, s1_ref, s2_ref, e_ref, wo_ref,
                    gate_ref, post_ref, out_ref):
    lane = lax.broadcasted_iota(jnp.int32, (1, LANES), 1)
    in_first_block = functools.reduce(jnp.logical_or, [
        jnp.logical_or(lane == _stat_lane(h), lane == _stat_lane(h) + STAT_MAX_SHIFT)
        for h in range(DIL_HEADS_PER_STEP)])
    e = e_ref[...]
    for r0 in range(0, x_ref.shape[0], OUT_SUB):
        rows = slice(r0, r0 + OUT_SUB)
        sums = [jnp.where(in_first_block, r[rows, :LANES], r[rows, LANES:]) for r in (s0_ref, s1_ref, s2_ref)]
        maxes = [pltpu.roll(s, LANES - STAT_MAX_SHIFT, 1) for s in sums]
        m = jnp.maximum(jnp.maximum(maxes[0], maxes[1]), maxes[2])
        es = [jnp.exp2(mg - m) for mg in maxes]
        inv = 1.0 / (sums[0] * es[0] + sums[1] * es[1] + sums[2] * es[2])
        mix = None
        for eg, o_ref in zip(es, (o0_ref, o1_ref, o2_ref)):
            alpha = eg * inv
            hi = alpha.astype(BF16)
            lo = (alpha - hi.astype(F32)).astype(BF16)
            term = _dot(jnp.concatenate([hi, lo], axis=1), e) * o_ref[rows, :].astype(F32)
            mix = term if mix is None else mix + term
        y = _dot(mix.astype(BF16), wo_ref[...])
        out_ref[rows, :] = x_ref[rows, :] + gate_ref[...] * _rms(y, post_ref[...])


def _dil_out(x, outs, stats, mod, post, w_o, layer):
    tm = TM_OUT
    tiles_per_batch = SEQ // tm
    halves = DIL_HEADS // DIL_HEADS_PER_STEP
    stat_lane_of_col = jnp.array([_stat_lane(c // DIL_HEAD_DIM) for c in range(DIL_WIDTH)], jnp.int32)
    lane_of_row = jnp.arange(2 * LANES) % LANES
    expand = (lane_of_row[:, None] == stat_lane_of_col[None, :]).astype(BF16)
    batch_rows = lambda width: pl.BlockSpec(
        (None, tm, width), lambda t: (t // tiles_per_batch, t % tiles_per_batch, 0))
    tok = pl.BlockSpec((tm, D_MODEL), lambda t: (t, 0))
    return pl.pallas_call(
        _dil_out_kernel,
        grid=(TOKENS // tm,),
        in_specs=[
            tok, *([batch_rows(DIL_WIDTH)] * 3), *([batch_rows(halves * LANES)] * 3),
            pl.BlockSpec(expand.shape, lambda t: (0, 0)), pl.BlockSpec(w_o.shape, lambda t: (0, 0)),
            _mod_spec(layer, 1, 2, tm), _row_spec(layer * N_SUB + 1),
        ],
        out_specs=tok,
        out_shape=jax.ShapeDtypeStruct((TOKENS, D_MODEL), F32),
        compiler_params=_params("arbitrary"),
        name="dil_out",
    )(x, *outs, *stats, expand, w_o, mod, post)


def kernel(x, c, norm_pre, norm_post, w_mod, b_mod, ffn_w_gate, ffn_w_up, ffn_w_down,
           mla_w_in, mla_q_norm, mla_w_q_up, mla_kv_norm, mla_w_kv_up, mla_w_o,
           dil_w_in, dil_w_o, rel_bias):
    assert x.shape == (BATCH, SEQ, D_MODEL) and x.dtype == F32
    assert DEPTH == 2
    h = x.reshape(TOKENS, D_MODEL)
    mod = _modulation(c, w_mod, b_mod)
    pre = norm_pre.reshape(DEPTH * N_SUB, 1, D_MODEL)
    post = norm_post.reshape(DEPTH * N_SUB, 1, D_MODEL)
    tables = _rope_tables()
    bias = _bias_tables(rel_bias)
    ffn_f32 = [w.reshape(DEPTH * 2, *w.shape[2:]) for w in (ffn_w_gate, ffn_w_up, ffn_w_down)]

    h = _ffn(h, mod, pre, post, (*[w[:1].astype(BF16) for w in ffn_f32], 0), 0, 0)
    win, wq, wk, wv = _mla_weights(mla_w_in[0], mla_w_q_up[0], mla_w_kv_up[0])
    q, k, v = _mla_proj(h, mod, pre, win, mla_q_norm[0][None, :], mla_kv_norm[0][None, :],
                        wq, wk, wv, tables, 0)
    casts = [(ffn_f32[0], 1, 64), (ffn_f32[1], 1, 64), (ffn_f32[2], 1, 256), (dil_w_in, 0, 16)]
    att, (wg, wu, wd, dil_in) = _mla_attn(q, k, v, casts)
    h = _ffn(h, mod, pre, post, (wg, wu, wd, 0), 0, 1, mla=(att, mla_w_o[0].astype(BF16)))

    h = _ffn(h, mod, pre, post, (wg, wu, wd, 1), 1, 0)
    outs, stats = zip(*[_dil_attn(_dil_proj(h, mod, pre, dil_in[0], 1, g), bias, g) for g in range(N_GROUPS)])
    h = _dil_out(h, outs, stats, mod, post, dil_w_o[0].astype(BF16), 1)
    h = _ffn(h, mod, pre, post, (wg, wu, wd, 2), 1, 1)
    return h.reshape(BATCH, SEQ, D_MODEL)
```

```python
import functools
import math

import jax
import jax.numpy as jnp
import numpy as np
from jax import lax
from jax.experimental import pallas as pl
from jax.experimental.pallas import tpu as pltpu

F32 = jnp.float32
BF16 = jnp.bfloat16

D_MODEL = 1024
BATCH = 8
SEQ = 2048
DEPTH = 2
N_SUB = 3
D_FF = 2816
FFN_RES = 0.5
EPS = 1e-6

MLA_HEADS = 16
Q_LORA = 384
KV_LORA = 256
QK_NOPE = 64
QK_ROPE = 32
V_HEAD = 64
ROPE_THETA = 10000.0

DIL_GROUPS = ((128, 1), (512, 4), (2048, 16))
N_GROUPS = 3
DIL_HEADS = 16
DIL_HEAD_DIM = 64
DIL_BLOCK = 128
DIL_WIDTH = DIL_HEADS * DIL_HEAD_DIM
N_BUCKETS = 32
MAX_DISTANCE = 2048

TOKENS = BATCH * SEQ
LANES = 128
HEAD_SLOT = 128
ROPE_HALF = QK_ROPE // 2
MASK_VALUE = -1e30
LOG2_E = math.log2(math.e)
VMEM_LIMIT = 56 * 1024 * 1024

TM_FFN = 1024
FFN_SUB = 256
TM_PROJ = 1024
PROJ_SUB = 256
TM_DIL_PROJ = 1024
TM_OUT = 1024
OUT_SUB = 512
TQ_MLA = 256
MLA_SUM_ROWS = 16
MLA_LOOKAHEAD = 6
TN_MOD = 1536
FF_CHUNK = 1536


def _params(*sem):
    return pltpu.CompilerParams(dimension_semantics=sem, vmem_limit_bytes=VMEM_LIMIT)


def _rms(x, g):
    ms = jnp.mean(x * x, axis=-1, keepdims=True)
    return x * lax.rsqrt(ms + EPS) * g


def _dot(a, b):
    return jnp.dot(a, b, preferred_element_type=F32)


def _dot_nt(a, b):
    return lax.dot_general(a, b, (((1,), (1,)), ((), ())), preferred_element_type=F32)


def _silu(x):
    return x * jax.nn.sigmoid(x)


def _mod_spec(layer, sub, kind, tm):
    tiles_per_batch = SEQ // tm
    return pl.BlockSpec(
        (None, 1, D_MODEL),
        lambda t, *_: ((layer * BATCH + t // tiles_per_batch) * (N_SUB * 3) + sub * 3 + kind, 0, 0))


def _row_spec(index):
    return pl.BlockSpec((None, 1, D_MODEL), lambda *_: (index, 0, 0))


def _mod_kernel(c_ref, w_ref, b_ref, o_ref):
    h = _silu(c_ref[...])
    hi = h.astype(BF16)
    lo = (h - hi.astype(F32)).astype(BF16)
    y = _dot(jnp.concatenate([hi, lo], axis=0), w_ref[...].astype(BF16))
    o_ref[...] = y[:BATCH] + y[BATCH:] + b_ref[...]


def _modulation(c, w_mod, b_mod):
    n = N_SUB * 3 * D_MODEL
    out = pl.pallas_call(
        _mod_kernel,
        grid=(DEPTH, n // TN_MOD),
        in_specs=[
            pl.BlockSpec((BATCH, D_MODEL), lambda i, j: (0, 0)),
            pl.BlockSpec((None, D_MODEL, TN_MOD), lambda i, j: (i, 0, j)),
            pl.BlockSpec((None, 1, TN_MOD), lambda i, j: (i, 0, j)),
        ],
        out_specs=pl.BlockSpec((None, BATCH, TN_MOD), lambda i, j: (i, 0, j)),
        out_shape=jax.ShapeDtypeStruct((DEPTH, BATCH, n), F32),
        compiler_params=_params("arbitrary", "arbitrary"),
        name="modulation",
    )(c, w_mod, b_mod.reshape(DEPTH, 1, n))
    return out.reshape(DEPTH * BATCH * N_SUB * 3, 1, D_MODEL)


def _mla_mix(rows, att_ref, wo_ref):
    return _dot(att_ref[rows, :], wo_ref[...])


def _ffn_kernel(x_ref, *refs, after_mla):
    if after_mla:
        att_ref, wo_ref, mgate_ref, mpost_ref, *refs = refs
    shift_ref, scale_ref, gate_ref, pre_ref, post_ref, wg_ref, wu_ref, wd_ref, o_ref, a_ref = refs
    tm = x_ref.shape[0]
    subs = [slice(r0, r0 + FFN_SUB) for r0 in range(0, tm, FFN_SUB)]
    hns = []
    for rows in subs:
        x = x_ref[rows, :]
        if after_mla:
            x = x + mgate_ref[...] * _rms(_mla_mix(rows, att_ref, wo_ref), mpost_ref[...])
            o_ref[rows, :] = x
        hns.append((_rms(x, pre_ref[...]) * (1.0 + scale_ref[...]) + shift_ref[...]).astype(BF16))
    for rows, hn in zip(subs, hns):
        for c0 in range(0, D_FF, FF_CHUNK):
            c1 = min(c0 + FF_CHUNK, D_FF)
            g = _dot(hn, wg_ref[:, c0:c1])
            u = _dot(hn, wu_ref[:, c0:c1])
            a_ref[rows, c0:c1] = (_silu(g) * u).astype(BF16)
    for rows in subs:
        y = _dot(a_ref[rows, :], wd_ref[...])
        x = o_ref[rows, :] if after_mla else x_ref[rows, :]
        o_ref[rows, :] = x + FFN_RES * gate_ref[...] * _rms(y, post_ref[...])


def _ffn(x, mod, pre, post, weights, layer, which, mla=None):
    wg, wu, wd, w_set = weights
    sub = 0 if which == 0 else 2
    tm = TM_FFN
    w_in_spec = pl.BlockSpec((None, D_MODEL, D_FF), lambda t: (w_set, 0, 0), pipeline_mode=pl.Buffered(1))
    w_out_spec = pl.BlockSpec((None, D_FF, D_MODEL), lambda t: (w_set, 0, 0), pipeline_mode=pl.Buffered(1))
    mla_args, mla_specs = [], []
    if mla is not None:
        att, w_o = mla
        mla_args = [att, w_o, mod, post]
        mla_specs = [pl.BlockSpec((tm, att.shape[1]), lambda t: (t, 0)),
                     pl.BlockSpec(w_o.shape, lambda t: (0, 0)),
                     _mod_spec(layer, 1, 2, tm), _row_spec(layer * N_SUB + 1)]
    return pl.pallas_call(
        functools.partial(_ffn_kernel, after_mla=mla is not None),
        grid=(TOKENS // tm,),
        in_specs=[
            pl.BlockSpec((tm, D_MODEL), lambda t: (t, 0)),
            *mla_specs,
            _mod_spec(layer, sub, 0, tm), _mod_spec(layer, sub, 1, tm), _mod_spec(layer, sub, 2, tm),
            _row_spec(layer * N_SUB + sub), _row_spec(layer * N_SUB + sub),
            w_in_spec, w_in_spec, w_out_spec,
        ],
        out_specs=pl.BlockSpec((tm, D_MODEL), lambda t: (t, 0)),
        out_shape=jax.ShapeDtypeStruct((TOKENS, D_MODEL), F32),
        scratch_shapes=[pltpu.VMEM((tm, D_FF), BF16)],
        compiler_params=_params("arbitrary"),
        name="ffn" if mla is None else "mla_out_ffn",
    )(x, *mla_args, mod, mod, mod, pre, post, wg, wu, wd)


def _rope(z, ta, tb):
    return z * ta + pltpu.roll(z, HEAD_SLOT // 2, 1) * tb


def _mla_proj_kernel(x_ref, shift_ref, scale_ref, pre_ref, win_ref, qn_ref, kvn_ref,
                     wq_ref, wk_ref, wv_ref, ta_ref, tb_ref, q_ref, k_ref, v_ref):
    tm = x_ref.shape[0]
    subs = [slice(r0, r0 + PROJ_SUB) for r0 in range(0, tm, PROJ_SUB)]
    lats = []
    for rows in subs:
        hn = (_rms(x_ref[rows, :], pre_ref[...]) * (1.0 + scale_ref[...]) + shift_ref[...]).astype(BF16)
        lats.append(_dot(hn, win_ref[...]))
    for rows, lat in zip(subs, lats):
        cq = _rms(lat[:, :Q_LORA], qn_ref[...]).astype(BF16)
        ckv = _rms(lat[:, Q_LORA:Q_LORA + KV_LORA], kvn_ref[...]).astype(BF16)
        ta, tb = ta_ref[rows, :], tb_ref[rows, :]
        kr = _rope(lat[:, Q_LORA + KV_LORA:], ta, tb)
        q = _dot(cq, wq_ref[...]) * ((QK_NOPE + QK_ROPE) ** -0.5 * LOG2_E)
        kn = _dot(ckv, wk_ref[...])
        v_ref[rows, :] = _dot(ckv, wv_ref[...]).astype(BF16)
        for h in range(MLA_HEADS):
            sl = slice(h * HEAD_SLOT, (h + 1) * HEAD_SLOT)
            q_ref[rows, sl] = _rope(q[:, sl], ta, tb).astype(BF16)
            k_ref[rows, sl] = (kn[:, sl] + kr).astype(BF16)


def _head_slot(nope, rope, like):
    split = HEAD_SLOT // 2 - ROPE_HALF
    zeros = lambda n: jnp.zeros(like.shape[:-1] + (n,), like.dtype)
    nope = zeros(QK_NOPE) if nope is None else nope
    rope = zeros(QK_ROPE) if rope is None else rope
    return jnp.concatenate([rope[..., :ROPE_HALF], nope[..., :split], rope[..., ROPE_HALF:], nope[..., split:],
                            zeros(HEAD_SLOT - QK_NOPE - QK_ROPE)], axis=-1)


def _rope_tables():
    pos = jnp.arange(SEQ, dtype=F32)
    freqs = ROPE_THETA ** (-jnp.arange(ROPE_HALF, dtype=F32) / ROPE_HALF)
    ang = pos[:, None] * freqs[None, :]
    cos, sin = jnp.cos(ang), jnp.sin(ang)
    ones = jnp.ones((SEQ, QK_NOPE), F32)
    ta = _head_slot(ones, jnp.concatenate([cos, cos], axis=1), cos)
    tb = _head_slot(None, jnp.concatenate([-sin, sin], axis=1), sin)
    return ta, tb


def _mla_weights(w_in, w_q_up, w_kv_up):
    win = jnp.concatenate([w_in[:, :Q_LORA + KV_LORA],
                           _head_slot(None, w_in[:, Q_LORA + KV_LORA:], w_in)], axis=1).astype(BF16)
    wq = w_q_up.reshape(Q_LORA, MLA_HEADS, QK_NOPE + QK_ROPE)
    wq = _head_slot(wq[..., :QK_NOPE], wq[..., QK_NOPE:], wq)
    wq = wq.reshape(Q_LORA, MLA_HEADS * HEAD_SLOT).astype(BF16)
    wkv = w_kv_up.reshape(KV_LORA, MLA_HEADS, QK_NOPE + V_HEAD)
    wk = _head_slot(wkv[..., :QK_NOPE], None, wkv).reshape(KV_LORA, MLA_HEADS * HEAD_SLOT).astype(BF16)
    wv = wkv[..., QK_NOPE:].reshape(KV_LORA, MLA_HEADS * V_HEAD).astype(BF16)
    return win, wq, wk, wv


def _mla_proj(x, mod, pre, win, qn, kvn, wq, wk, wv, tables, layer):
    tm = TM_PROJ
    tiles_per_batch = SEQ // tm
    full = lambda a: pl.BlockSpec(a.shape, lambda t: (0,) * a.ndim)
    tab_spec = pl.BlockSpec((tm, HEAD_SLOT), lambda t: (t % tiles_per_batch, 0))
    wide = MLA_HEADS * HEAD_SLOT
    return pl.pallas_call(
        _mla_proj_kernel,
        grid=(TOKENS // tm,),
        in_specs=[
            pl.BlockSpec((tm, D_MODEL), lambda t: (t, 0)),
            _mod_spec(layer, 1, 0, tm), _mod_spec(layer, 1, 1, tm),
            _row_spec(layer * N_SUB + 1),
            full(win), full(qn), full(kvn), full(wq), full(wk), full(wv),
            tab_spec, tab_spec,
        ],
        out_specs=[
            pl.BlockSpec((tm, wide), lambda t: (t, 0)),
            pl.BlockSpec((tm, wide), lambda t: (t, 0)),
            pl.BlockSpec((tm, MLA_HEADS * V_HEAD), lambda t: (t, 0)),
        ],
        out_shape=[
            jax.ShapeDtypeStruct((TOKENS, wide), BF16),
            jax.ShapeDtypeStruct((TOKENS, wide), BF16),
            jax.ShapeDtypeStruct((TOKENS, MLA_HEADS * V_HEAD), BF16),
        ],
        compiler_params=_params("arbitrary"),
        name="mla_proj",
    )(x, mod, mod, pre, win, qn, kvn, wq, wk, wv, *tables)


def _mla_attn_kernel(q_ref, k_ref, v_ref, *refs, n_cast):
    cast_in, (o_ref, *cast_out), (vt_ref, ot_ref) = (
        refs[:n_cast], refs[n_cast:2 * n_cast + 1], refs[2 * n_cast + 1:])
    tq = TQ_MLA
    key = lax.broadcasted_iota(jnp.int32, (tq, tq), 0)
    query = lax.broadcasted_iota(jnp.int32, (tq, tq), 1)
    causal = key <= query
    vt_ref[:2 * V_HEAD, :] = v_ref[...].astype(F32).T.astype(BF16)
    vt_ref[2 * V_HEAD:, :] = jnp.ones((MLA_SUM_ROWS, SEQ), BF16)

    def logits(hh, qi, c):
        hs = slice(hh * HEAD_SLOT, (hh + 1) * HEAD_SLOT)
        s = _dot_nt(k_ref[c * tq:(c + 1) * tq, hs], q_ref[qi * tq:(qi + 1) * tq, hs])
        return jnp.where(causal, s, MASK_VALUE) if c == qi else s

    chunks = [(hh, qi, c) for hh in range(2) for qi in range(SEQ // tq) for c in range(qi + 1)]
    queued = [logits(*ch) for ch in chunks[:MLA_LOOKAHEAD]]
    for i, (hh, qi, c) in enumerate(chunks):
        if i + MLA_LOOKAHEAD < len(chunks):
            queued.append(logits(*chunks[i + MLA_LOOKAHEAD]))
        s = queued.pop(0)
        m_chunk = jnp.max(s, axis=0, keepdims=True)
        if c == 0:
            m = m_chunk
            acc = _dot(vt_ref[:, 0:tq], jnp.exp2(s - m).astype(BF16))
        else:
            m_new = jnp.maximum(m, m_chunk)
            acc = acc * jnp.exp2(m - m_new) + _dot(vt_ref[:, c * tq:(c + 1) * tq],
                                                   jnp.exp2(s - m_new).astype(BF16))
            m = m_new
        if c == qi:
            rows = slice(hh * V_HEAD, (hh + 1) * V_HEAD)
            ot_ref[rows, qi * tq:(qi + 1) * tq] = acc[rows, :] * (1.0 / acc[2 * V_HEAD:2 * V_HEAD + 1, :])
    o_ref[...] = ot_ref[...].T.astype(BF16)
    for src_ref, dst_ref in zip(cast_in, cast_out):
        dst_ref[...] = src_ref[...].astype(BF16)


def _mla_attn(q, k, v, casts):
    pairs = MLA_HEADS // 2
    cast_args, cast_in_specs, cast_out_specs, cast_shapes = [], [], [], []
    for array, first_set, block_rows in casts:
        sets, rows, cols = array.shape
        first_block, n_blocks = first_set * rows // block_rows, (sets - first_set) * rows // block_rows
        assert first_set * rows % block_rows == 0 and rows % block_rows == 0 and n_blocks <= BATCH * pairs

        def block(b, p, first=0, n_blocks=n_blocks):
            return first + jnp.minimum(b * pairs + p, n_blocks - 1), 0
        cast_args.append(array.reshape(sets * rows, cols))
        cast_in_specs.append(pl.BlockSpec((block_rows, cols), functools.partial(block, first=first_block)))
        cast_out_specs.append(pl.BlockSpec((block_rows, cols), block))
        cast_shapes.append(jax.ShapeDtypeStruct(((sets - first_set) * rows, cols), BF16))
    att, *copies = pl.pallas_call(
        functools.partial(_mla_attn_kernel, n_cast=len(casts)),
        grid=(BATCH, pairs),
        in_specs=[
            pl.BlockSpec((SEQ, 2 * HEAD_SLOT), lambda b, p: (b, p)),
            pl.BlockSpec((SEQ, 2 * HEAD_SLOT), lambda b, p: (b, p)),
            pl.BlockSpec((SEQ, 2 * V_HEAD), lambda b, p: (b, p)),
            *cast_in_specs,
        ],
        out_specs=[pl.BlockSpec((SEQ, 2 * V_HEAD), lambda b, p: (b, p)), *cast_out_specs],
        out_shape=[jax.ShapeDtypeStruct((TOKENS, MLA_HEADS * V_HEAD), BF16), *cast_shapes],
        scratch_shapes=[pltpu.VMEM((2 * V_HEAD + MLA_SUM_ROWS, SEQ), BF16),
                        pltpu.VMEM((2 * V_HEAD, SEQ), F32)],
        compiler_params=_params("arbitrary", "arbitrary"),
        name="mla_attn",
    )(q, k, v, *cast_args)
    return att, [copy.reshape(-1, *array.shape[1:]) for copy, (array, _, _) in zip(copies, casts)]


def _t5_bucket(dist):
    max_exact = N_BUCKETS // 2
    d = np.maximum(dist, 1).astype(np.float32)
    scale = np.float32(math.log(MAX_DISTANCE / max_exact))
    large = max_exact + (np.log(d / np.float32(max_exact)) / scale
                         * np.float32(N_BUCKETS - max_exact)).astype(np.int32)
    large = np.minimum(large, N_BUCKETS - 1)
    return np.where(dist < max_exact, dist, large)


def _bias_selectors():
    j = np.arange(2 * DIL_BLOCK)
    rel = DIL_BLOCK - j
    sel = []
    for window, dilation in DIL_GROUPS:
        assert window // dilation == DIL_BLOCK
        bucket = _t5_bucket(np.maximum(rel, 0) * dilation)
        onehot = (bucket[None, :] == np.arange(N_BUCKETS)[:, None]) & (rel >= 0)[None, :]
        sel.append(onehot.astype(np.float32))
    return jnp.asarray(np.stack(sel))


def _bias_kernel(rbt_ref, sel_ref, o_ref):
    rbt = rbt_ref[...]
    sel = sel_ref[...]
    row0 = jnp.zeros((DIL_HEADS, 2 * DIL_BLOCK), F32)
    for b in range(N_BUCKETS):
        row0 = row0 + rbt[:, b:b + 1] * sel[b:b + 1, :]
    future = jnp.sum(sel, axis=0, keepdims=True) < 0.5
    row0 = jnp.where(future, MASK_VALUE, row0 * LOG2_E)
    for h in range(DIL_HEADS):
        t = jnp.broadcast_to(row0[h:h + 1, :], (DIL_BLOCK, 2 * DIL_BLOCK))
        o_ref[h] = pltpu.roll(t, 0, 1, stride=1, stride_axis=0)


def _bias_tables(rel_bias):
    rbt = rel_bias.T.reshape(N_GROUPS, DIL_HEADS, N_BUCKETS)
    return pl.pallas_call(
        _bias_kernel,
        grid=(N_GROUPS,),
        in_specs=[
            pl.BlockSpec((None, DIL_HEADS, N_BUCKETS), lambda g: (g, 0, 0)),
            pl.BlockSpec((None, N_BUCKETS, 2 * DIL_BLOCK), lambda g: (g, 0, 0)),
        ],
        out_specs=pl.BlockSpec((DIL_HEADS, DIL_BLOCK, 2 * DIL_BLOCK), lambda g: (g, 0, 0)),
        out_shape=jax.ShapeDtypeStruct((N_GROUPS * DIL_HEADS, DIL_BLOCK, 2 * DIL_BLOCK), F32),
        compiler_params=_params("arbitrary"),
        name="dil_bias",
    )(rbt, _bias_selectors())


def _dil_proj_kernel(x_ref, shift_ref, scale_ref, pre_ref, w_ref, o_ref, *scratch, dilation):
    tm = x_ref.shape[0]
    per = PROJ_SUB // dilation
    slabs = D_MODEL // LANES
    hbs = []
    for i, r0 in enumerate(range(0, tm, PROJ_SUB)):
        hn = _rms(x_ref[r0:r0 + PROJ_SUB, :], pre_ref[...]) * (1.0 + scale_ref[...]) + shift_ref[...]
        if dilation == 1:
            hbs.append(hn.astype(BF16))
            continue
        slab_ref, = scratch
        for s in range(slabs):
            slab_ref[i, s] = hn[:, s * LANES:(s + 1) * LANES]
        hbs.append(jnp.concatenate(
            [jnp.concatenate([slab_ref[i, s, pl.ds(r, per, stride=dilation), :] for s in range(slabs)],
                             axis=1).astype(BF16)
             for r in range(dilation)], axis=0))
    for i, hb in enumerate(hbs):
        for j in range(3):
            cols = slice(j * DIL_WIDTH, (j + 1) * DIL_WIDTH)
            y = _dot(hb, w_ref[:, cols])
            if j == 0:
                y = y * (DIL_HEAD_DIM ** -0.5 * LOG2_E)
            y = y.astype(BF16)
            for r in range(dilation):
                o_ref[r, i * per:(i + 1) * per, cols] = y[r * per:(r + 1) * per, :]


def _dil_proj(x, mod, pre, w, layer, group):
    _, dilation = DIL_GROUPS[group]
    tm = TM_DIL_PROJ
    tiles_per_batch = SEQ // tm
    sub = tm // dilation
    scratch = [] if dilation == 1 else [
        pltpu.VMEM((tm // PROJ_SUB, D_MODEL // LANES, PROJ_SUB, LANES), F32)]
    return pl.pallas_call(
        functools.partial(_dil_proj_kernel, dilation=dilation),
        grid=(TOKENS // tm,),
        in_specs=[
            pl.BlockSpec((tm, D_MODEL), lambda t: (t, 0)),
            _mod_spec(layer, 1, 0, tm), _mod_spec(layer, 1, 1, tm),
            _row_spec(layer * N_SUB + 1),
            pl.BlockSpec((D_MODEL, 3 * DIL_WIDTH), lambda t: (0, group), pipeline_mode=pl.Buffered(1)),
        ],
        out_specs=pl.BlockSpec((None, dilation, sub, 3 * DIL_WIDTH),
                               lambda t: (t // tiles_per_batch, 0, t % tiles_per_batch, 0)),
        out_shape=jax.ShapeDtypeStruct((BATCH, dilation, SEQ // dilation, 3 * DIL_WIDTH), BF16),
        scratch_shapes=scratch,
        compiler_params=_params("arbitrary"),
        name=f"dil_proj_g{group}",
    )(x, mod, mod, pre, w)


DIL_HEADS_PER_STEP = 8
DIL_PAIRS_PER_STEP = DIL_HEADS_PER_STEP // 2
DIL_UNITS_PER_TRIP = 8


STAT_MAX_SHIFT = 32


def _stat_lane(head):
    return head if head % 2 else DIL_HEAD_DIM + head


def _dil_attn_kernel(q_ref, k_ref, v_ref, bias_ref, o_ref, stat_ref, s_ref, acc_ref, *, dilation, n_blocks):
    blk = DIL_BLOCK
    lane = lax.broadcasted_iota(jnp.int32, (blk, LANES), 1)
    first_head = lane < DIL_HEAD_DIM
    head0 = pl.program_id(1) * DIL_HEADS_PER_STEP

    def where_unit(i, first):
        if first:
            return i, 0
        later = n_blocks - 1
        if dilation == 1:
            return 0, i + 1
        i = jnp.asarray(i, jnp.int32)
        return lax.div(i, jnp.int32(later)), lax.rem(i, jnp.int32(later)) + 1

    def window(ref, r, n, first, cs):
        if first:
            return ref[r, 0:blk, cs]
        return ref[r, pl.ds(pl.multiple_of(n * blk - blk, blk), 2 * blk), cs]

    def logits(i, slot, first):
        r, n = where_unit(i, first)
        kw = blk if first else 2 * blk
        q_rows = slice(0, blk) if first else pl.ds(pl.multiple_of(n * blk, blk), blk)
        for hp in range(DIL_PAIRS_PER_STEP):
            cs = slice(hp * LANES, (hp + 1) * LANES)
            q2 = q_ref[r, q_rows, cs]
            k2 = window(k_ref, r, n, first, cs)
            for hh in range(2):
                h = 2 * hp + hh
                qh = jnp.where(first_head if hh == 0 else jnp.logical_not(first_head), q2, 0)
                bias = bias_ref[h, :, blk:] if first else bias_ref[h]
                s_ref[slot, h, :, 0:kw] = _dot_nt(qh, k2) + bias

    def attend(i, slot, first):
        r, n = where_unit(i, first)
        kw = blk if first else 2 * blk
        t0 = n * (blk * dilation) + r
        rows = pl.ds(t0, blk) if dilation == 1 else pl.ds(t0, blk, stride=dilation)
        stat_tile = jnp.ones((blk, LANES), F32)
        for hp in range(DIL_PAIRS_PER_STEP):
            cs = slice(hp * LANES, (hp + 1) * LANES)
            v2 = window(v_ref, r, n, first, cs)
            accs = []
            for hh in range(2):
                s = s_ref[slot, 2 * hp + hh, :, 0:kw]
                m = jnp.max(s, axis=-1, keepdims=True)
                key_lane = lax.broadcasted_iota(jnp.int32, (kw, LANES), 1)
                own = key_lane < DIL_HEAD_DIM if hh == 0 else key_lane >= DIL_HEAD_DIM
                acc = _dot(jnp.exp2(s - m).astype(BF16), jnp.where(own, v2, 1))
                sum_lane = head0 + _stat_lane(2 * hp + hh)
                stat_tile = jnp.where(lane == sum_lane, acc, stat_tile)
                stat_tile = jnp.where(lane == sum_lane + STAT_MAX_SHIFT, m, stat_tile)
                accs.append(acc)
            acc_ref[hp, rows, :] = jnp.where(first_head, accs[0], accs[1])
        stat_ref[rows, :] = stat_tile

    def run(count, first):
        trips, rest = divmod(count, DIL_UNITS_PER_TRIP)

        def units(base, n, more):
            for u in range(n):
                if u + 1 < n or more:
                    logits(base + u + 1, (u + 1) % 2, first)
                attend(base + u, u % 2, first)

        logits(0, 0, first)
        if trips:
            def body(j, carry):
                units(j * DIL_UNITS_PER_TRIP, DIL_UNITS_PER_TRIP, True)
                return carry
            lax.fori_loop(0, trips - (0 if rest else 1), body, 0)
            if not rest:
                units((trips - 1) * DIL_UNITS_PER_TRIP, DIL_UNITS_PER_TRIP, False)
        if rest:
            units(trips * DIL_UNITS_PER_TRIP, rest, False)

    run(dilation, True)
    if n_blocks > 1:
        run(dilation * (n_blocks - 1), False)
    for hp in range(DIL_PAIRS_PER_STEP):
        o_ref[:, hp * LANES:(hp + 1) * LANES] = acc_ref[hp].astype(BF16)


def _dil_attn(proj, bias, group):
    _, dilation = DIL_GROUPS[group]
    length = SEQ // dilation
    halves = DIL_HEADS // DIL_HEADS_PER_STEP
    width = DIL_HEADS_PER_STEP * DIL_HEAD_DIM
    qkv_spec = lambda kind: pl.BlockSpec(
        (None, dilation, length, width), lambda b, u: (b, 0, 0, kind * halves + u))
    return pl.pallas_call(
        functools.partial(_dil_attn_kernel, dilation=dilation, n_blocks=length // DIL_BLOCK),
        grid=(BATCH, halves),
        in_specs=[
            qkv_spec(0), qkv_spec(1), qkv_spec(2),
            pl.BlockSpec((DIL_HEADS_PER_STEP, DIL_BLOCK, 2 * DIL_BLOCK),
                         lambda b, u: (group * halves + u, 0, 0)),
        ],
        out_specs=[
            pl.BlockSpec((None, SEQ, width), lambda b, u: (b, 0, u)),
            pl.BlockSpec((None, SEQ, LANES), lambda b, u: (b, 0, u)),
        ],
        out_shape=[
            jax.ShapeDtypeStruct((BATCH, SEQ, DIL_WIDTH), BF16),
            jax.ShapeDtypeStruct((BATCH, SEQ, halves * LANES), F32),
        ],
        scratch_shapes=[pltpu.VMEM((2, DIL_HEADS_PER_STEP, DIL_BLOCK, 2 * DIL_BLOCK), F32),
                        pltpu.VMEM((DIL_PAIRS_PER_STEP, SEQ, LANES), F32)],
        compiler_params=_params("arbitrary", "arbitrary"),
        name=f"dil_attn_g{group}",
    )(proj, proj, proj, bias)


def _dil_out_kernel(x_ref, o0_ref, o1_ref, o2_ref, s0_ref, s1_ref, s2_ref, e_ref, wo_ref,
                    gate_ref, post_ref, out_ref):
    lane = lax.broadcasted_iota(jnp.int32, (1, LANES), 1)
    in_first_block = functools.reduce(jnp.logical_or, [
        jnp.logical_or(lane == _stat_lane(h), lane == _stat_lane(h) + STAT_MAX_SHIFT)
        for h in range(DIL_HEADS_PER_STEP)])
    e = e_ref[...]
    for r0 in range(0, x_ref.shape[0], OUT_SUB):
        rows = slice(r0, r0 + OUT_SUB)
        sums = [jnp.where(in_first_block, r[rows, :LANES], r[rows, LANES:]) for r in (s0_ref, s1_ref, s2_ref)]
        maxes = [pltpu.roll(s, LANES - STAT_MAX_SHIFT, 1) for s in sums]
        m = jnp.maximum(jnp.maximum(maxes[0], maxes[1]), maxes[2])
        es = [jnp.exp2(mg - m) for mg in maxes]
        inv = 1.0 / (sums[0] * es[0] + sums[1] * es[1] + sums[2] * es[2])
        mix = None
        for eg, o_ref in zip(es, (o0_ref, o1_ref, o2_ref)):
            alpha = eg * inv
            hi = alpha.astype(BF16)
            lo = (alpha - hi.astype(F32)).astype(BF16)
            term = _dot(jnp.concatenate([hi, lo], axis=1), e) * o_ref[rows, :].astype(F32)
            mix = term if mix is None else mix + term
        y = _dot(mix.astype(BF16), wo_ref[...])
        out_ref[rows, :] = x_ref[rows, :] + gate_ref[...] * _rms(y, post_ref[...])


def _dil_out(x, outs, stats, mod, post, w_o, layer):
    tm = TM_OUT
    tiles_per_batch = SEQ // tm
    halves = DIL_HEADS // DIL_HEADS_PER_STEP
    stat_lane_of_col = jnp.array([_stat_lane(c // DIL_HEAD_DIM) for c in range(DIL_WIDTH)], jnp.int32)
    lane_of_row = jnp.arange(2 * LANES) % LANES
    expand = (lane_of_row[:, None] == stat_lane_of_col[None, :]).astype(BF16)
    batch_rows = lambda width: pl.BlockSpec(
        (None, tm, width), lambda t: (t // tiles_per_batch, t % tiles_per_batch, 0))
    tok = pl.BlockSpec((tm, D_MODEL), lambda t: (t, 0))
    return pl.pallas_call(
        _dil_out_kernel,
        grid=(TOKENS // tm,),
        in_specs=[
            tok, *([batch_rows(DIL_WIDTH)] * 3), *([batch_rows(halves * LANES)] * 3),
            pl.BlockSpec(expand.shape, lambda t: (0, 0)), pl.BlockSpec(w_o.shape, lambda t: (0, 0)),
            _mod_spec(layer, 1, 2, tm), _row_spec(layer * N_SUB + 1),
        ],
        out_specs=tok,
        out_shape=jax.ShapeDtypeStruct((TOKENS, D_MODEL), F32),
        compiler_params=_params("arbitrary"),
        name="dil_out",
    )(x, *outs, *stats, expand, w_o, mod, post)


def kernel(x, c, norm_pre, norm_post, w_mod, b_mod, ffn_w_gate, ffn_w_up, ffn_w_down,
           mla_w_in, mla_q_norm, mla_w_q_up, mla_kv_norm, mla_w_kv_up, mla_w_o,
           dil_w_in, dil_w_o, rel_bias):
    assert x.shape == (BATCH, SEQ, D_MODEL) and x.dtype == F32
    assert DEPTH == 2
    h = x.reshape(TOKENS, D_MODEL)
    mod = _modulation(c, w_mod, b_mod)
    pre = norm_pre.reshape(DEPTH * N_SUB, 1, D_MODEL)
    post = norm_post.reshape(DEPTH * N_SUB, 1, D_MODEL)
    tables = _rope_tables()
    bias = _bias_tables(rel_bias)
    ffn_f32 = [w.reshape(DEPTH * 2, *w.shape[2:]) for w in (ffn_w_gate, ffn_w_up, ffn_w_down)]

    h = _ffn(h, mod, pre, post, (*[w[:1].astype(BF16) for w in ffn_f32], 0), 0, 0)
    win, wq, wk, wv = _mla_weights(mla_w_in[0], mla_w_q_up[0], mla_w_kv_up[0])
    q, k, v = _mla_proj(h, mod, pre, win, mla_q_norm[0][None, :], mla_kv_norm[0][None, :],
                        wq, wk, wv, tables, 0)
    casts = [(ffn_f32[0], 1, 64), (ffn_f32[1], 1, 64), (ffn_f32[2], 1, 256), (dil_w_in, 0, 16)]
    att, (wg, wu, wd, dil_in) = _mla_attn(q, k, v, casts)
    h = _ffn(h, mod, pre, post, (wg, wu, wd, 0), 0, 1, mla=(att, mla_w_o[0].astype(BF16)))

    h = _ffn(h, mod, pre, post, (wg, wu, wd, 1), 1, 0)
    outs, stats = zip(*[_dil_attn(_dil_proj(h, mod, pre, dil_in[0], 1, g), bias, g) for g in range(N_GROUPS)])
    h = _dil_out(h, outs, stats, mod, post, dil_w_o[0].astype(BF16), 1)
    h = _ffn(h, mod, pre, post, (wg, wu, wd, 2), 1, 1)
    return h.reshape(BATCH, SEQ, D_MODEL)
```

```python
import functools
import math

import jax
import jax.numpy as jnp
import numpy as np
from jax import lax
from jax.experimental import pallas as pl
from jax.experimental.pallas import tpu as pltpu

F32 = jnp.float32
BF16 = jnp.bfloat16

D_MODEL = 1024
BATCH = 8
SEQ = 2048
DEPTH = 2
N_SUB = 3
D_FF = 2816
FFN_RES = 0.5
EPS = 1e-6

MLA_HEADS = 16
Q_LORA = 384
KV_LORA = 256
QK_NOPE = 64
QK_ROPE = 32
V_HEAD = 64
ROPE_THETA = 10000.0

DIL_GROUPS = ((128, 1), (512, 4), (2048, 16))
N_GROUPS = 3
DIL_HEADS = 16
DIL_HEAD_DIM = 64
DIL_BLOCK = 128
DIL_WIDTH = DIL_HEADS * DIL_HEAD_DIM
N_BUCKETS = 32
MAX_DISTANCE = 2048

TOKENS = BATCH * SEQ
LANES = 128
HEAD_SLOT = 128
ROPE_HALF = QK_ROPE // 2
MASK_VALUE = -1e30
LOG2_E = math.log2(math.e)
VMEM_LIMIT = 56 * 1024 * 1024

TM_FFN = 1024
FFN_SUB = 256
TM_PROJ = 1024
PROJ_SUB = 256
DEINTERLEAVE_STRIDE = 4
TM_DIL_PROJ = 1024
TM_OUT = 1024
OUT_SUB = 512
TQ_MLA = 256
MLA_SUM_ROWS = 16
MLA_LOOKAHEAD = 6
TN_MOD = 1536
FF_CHUNK = 1536


def _params(*sem):
    return pltpu.CompilerParams(dimension_semantics=sem, vmem_limit_bytes=VMEM_LIMIT)


def _rms(x, g):
    ms = jnp.mean(x * x, axis=-1, keepdims=True)
    return x * lax.rsqrt(ms + EPS) * g


def _dot(a, b):
    return jnp.dot(a, b, preferred_element_type=F32)


def _dot_nt(a, b):
    return lax.dot_general(a, b, (((1,), (1,)), ((), ())), preferred_element_type=F32)


def _silu(x):
    return x * jax.nn.sigmoid(x)


def _mod_spec(layer, sub, kind, tm):
    tiles_per_batch = SEQ // tm
    return pl.BlockSpec(
        (None, 1, D_MODEL),
        lambda t, *_: ((layer * BATCH + t // tiles_per_batch) * (N_SUB * 3) + sub * 3 + kind, 0, 0))


def _row_spec(index):
    return pl.BlockSpec((None, 1, D_MODEL), lambda *_: (index, 0, 0))


def _mod_kernel(c_ref, w_ref, b_ref, o_ref):
    h = _silu(c_ref[...])
    hi = h.astype(BF16)
    lo = (h - hi.astype(F32)).astype(BF16)
    y = _dot(jnp.concatenate([hi, lo], axis=0), w_ref[...].astype(BF16))
    o_ref[...] = y[:BATCH] + y[BATCH:] + b_ref[...]


def _modulation(c, w_mod, b_mod):
    n = N_SUB * 3 * D_MODEL
    out = pl.pallas_call(
        _mod_kernel,
        grid=(DEPTH, n // TN_MOD),
        in_specs=[
            pl.BlockSpec((BATCH, D_MODEL), lambda i, j: (0, 0)),
            pl.BlockSpec((None, D_MODEL, TN_MOD), lambda i, j: (i, 0, j)),
            pl.BlockSpec((None, 1, TN_MOD), lambda i, j: (i, 0, j)),
        ],
        out_specs=pl.BlockSpec((None, BATCH, TN_MOD), lambda i, j: (i, 0, j)),
        out_shape=jax.ShapeDtypeStruct((DEPTH, BATCH, n), F32),
        compiler_params=_params("arbitrary", "arbitrary"),
        name="modulation",
    )(c, w_mod, b_mod.reshape(DEPTH, 1, n))
    return out.reshape(DEPTH * BATCH * N_SUB * 3, 1, D_MODEL)


def _mla_mix(rows, att_ref, wo_ref):
    return _dot(att_ref[rows, :], wo_ref[...])


def _ffn_kernel(x_ref, *refs, after_mla):
    if after_mla:
        att_ref, wo_ref, mgate_ref, mpost_ref, *refs = refs
    shift_ref, scale_ref, gate_ref, pre_ref, post_ref, wg_ref, wu_ref, wd_ref, o_ref, a_ref = refs
    tm = x_ref.shape[0]
    subs = [slice(r0, r0 + FFN_SUB) for r0 in range(0, tm, FFN_SUB)]
    hns = []
    for rows in subs:
        x = x_ref[rows, :]
        if after_mla:
            x = x + mgate_ref[...] * _rms(_mla_mix(rows, att_ref, wo_ref), mpost_ref[...])
            o_ref[rows, :] = x
        hns.append((_rms(x, pre_ref[...]) * (1.0 + scale_ref[...]) + shift_ref[...]).astype(BF16))
    for rows, hn in zip(subs, hns):
        for c0 in range(0, D_FF, FF_CHUNK):
            c1 = min(c0 + FF_CHUNK, D_FF)
            g = _dot(hn, wg_ref[:, c0:c1])
            u = _dot(hn, wu_ref[:, c0:c1])
            a_ref[rows, c0:c1] = (_silu(g) * u).astype(BF16)
    for rows in subs:
        y = _dot(a_ref[rows, :], wd_ref[...])
        x = o_ref[rows, :] if after_mla else x_ref[rows, :]
        o_ref[rows, :] = x + FFN_RES * gate_ref[...] * _rms(y, post_ref[...])


def _ffn(x, mod, pre, post, weights, layer, which, mla=None):
    wg, wu, wd, w_set = weights
    sub = 0 if which == 0 else 2
    tm = TM_FFN
    w_in_spec = pl.BlockSpec((None, D_MODEL, D_FF), lambda t: (w_set, 0, 0), pipeline_mode=pl.Buffered(1))
    w_out_spec = pl.BlockSpec((None, D_FF, D_MODEL), lambda t: (w_set, 0, 0), pipeline_mode=pl.Buffered(1))
    mla_args, mla_specs = [], []
    if mla is not None:
        att, w_o = mla
        mla_args = [att, w_o, mod, post]
        mla_specs = [pl.BlockSpec((tm, att.shape[1]), lambda t: (t, 0)),
                     pl.BlockSpec(w_o.shape, lambda t: (0, 0)),
                     _mod_spec(layer, 1, 2, tm), _row_spec(layer * N_SUB + 1)]
    return pl.pallas_call(
        functools.partial(_ffn_kernel, after_mla=mla is not None),
        grid=(TOKENS // tm,),
        in_specs=[
            pl.BlockSpec((tm, D_MODEL), lambda t: (t, 0)),
            *mla_specs,
            _mod_spec(layer, sub, 0, tm), _mod_spec(layer, sub, 1, tm), _mod_spec(layer, sub, 2, tm),
            _row_spec(layer * N_SUB + sub), _row_spec(layer * N_SUB + sub),
            w_in_spec, w_in_spec, w_out_spec,
        ],
        out_specs=pl.BlockSpec((tm, D_MODEL), lambda t: (t, 0)),
        out_shape=jax.ShapeDtypeStruct((TOKENS, D_MODEL), F32),
        scratch_shapes=[pltpu.VMEM((tm, D_FF), BF16)],
        compiler_params=_params("arbitrary"),
        name="ffn" if mla is None else "mla_out_ffn",
    )(x, *mla_args, mod, mod, mod, pre, post, wg, wu, wd)


def _rope(z, ta, tb):
    return z * ta + pltpu.roll(z, HEAD_SLOT // 2, 1) * tb


def _mla_proj_kernel(x_ref, shift_ref, scale_ref, pre_ref, win_ref, qn_ref, kvn_ref,
                     wq_ref, wk_ref, wv_ref, ta_ref, tb_ref, q_ref, k_ref, v_ref):
    tm = x_ref.shape[0]
    subs = [slice(r0, r0 + PROJ_SUB) for r0 in range(0, tm, PROJ_SUB)]
    lats = []
    for rows in subs:
        hn = (_rms(x_ref[rows, :], pre_ref[...]) * (1.0 + scale_ref[...]) + shift_ref[...]).astype(BF16)
        lats.append(_dot(hn, win_ref[...]))
    for rows, lat in zip(subs, lats):
        cq = _rms(lat[:, :Q_LORA], qn_ref[...]).astype(BF16)
        ckv = _rms(lat[:, Q_LORA:Q_LORA + KV_LORA], kvn_ref[...]).astype(BF16)
        ta, tb = ta_ref[rows, :], tb_ref[rows, :]
        kr = _rope(lat[:, Q_LORA + KV_LORA:], ta, tb)
        q = _dot(cq, wq_ref[...]) * ((QK_NOPE + QK_ROPE) ** -0.5 * LOG2_E)
        kn = _dot(ckv, wk_ref[...])
        v_ref[rows, :] = _dot(ckv, wv_ref[...]).astype(BF16)
        for h in range(MLA_HEADS):
            sl = slice(h * HEAD_SLOT, (h + 1) * HEAD_SLOT)
            q_ref[rows, sl] = _rope(q[:, sl], ta, tb).astype(BF16)
            k_ref[rows, sl] = (kn[:, sl] + kr).astype(BF16)


def _head_slot(nope, rope, like):
    split = HEAD_SLOT // 2 - ROPE_HALF
    zeros = lambda n: jnp.zeros(like.shape[:-1] + (n,), like.dtype)
    nope = zeros(QK_NOPE) if nope is None else nope
    rope = zeros(QK_ROPE) if rope is None else rope
    return jnp.concatenate([rope[..., :ROPE_HALF], nope[..., :split], rope[..., ROPE_HALF:], nope[..., split:],
                            zeros(HEAD_SLOT - QK_NOPE - QK_ROPE)], axis=-1)


def _rope_tables():
    pos = jnp.arange(SEQ, dtype=F32)
    freqs = ROPE_THETA ** (-jnp.arange(ROPE_HALF, dtype=F32) / ROPE_HALF)
    ang = pos[:, None] * freqs[None, :]
    cos, sin = jnp.cos(ang), jnp.sin(ang)
    ones = jnp.ones((SEQ, QK_NOPE), F32)
    ta = _head_slot(ones, jnp.concatenate([cos, cos], axis=1), cos)
    tb = _head_slot(None, jnp.concatenate([-sin, sin], axis=1), sin)
    return ta, tb


def _mla_weights(w_in, w_q_up, w_kv_up):
    win = jnp.concatenate([w_in[:, :Q_LORA + KV_LORA],
                           _head_slot(None, w_in[:, Q_LORA + KV_LORA:], w_in)], axis=1).astype(BF16)
    wq = w_q_up.reshape(Q_LORA, MLA_HEADS, QK_NOPE + QK_ROPE)
    wq = _head_slot(wq[..., :QK_NOPE], wq[..., QK_NOPE:], wq)
    wq = wq.reshape(Q_LORA, MLA_HEADS * HEAD_SLOT).astype(BF16)
    wkv = w_kv_up.reshape(KV_LORA, MLA_HEADS, QK_NOPE + V_HEAD)
    wk = _head_slot(wkv[..., :QK_NOPE], None, wkv).reshape(KV_LORA, MLA_HEADS * HEAD_SLOT).astype(BF16)
    wv = wkv[..., QK_NOPE:].reshape(KV_LORA, MLA_HEADS * V_HEAD).astype(BF16)
    return win, wq, wk, wv


def _mla_proj(x, mod, pre, win, qn, kvn, wq, wk, wv, tables, layer):
    tm = TM_PROJ
    tiles_per_batch = SEQ // tm
    full = lambda a: pl.BlockSpec(a.shape, lambda t: (0,) * a.ndim)
    tab_spec = pl.BlockSpec((tm, HEAD_SLOT), lambda t: (t % tiles_per_batch, 0))
    wide = MLA_HEADS * HEAD_SLOT
    return pl.pallas_call(
        _mla_proj_kernel,
        grid=(TOKENS // tm,),
        in_specs=[
            pl.BlockSpec((tm, D_MODEL), lambda t: (t, 0)),
            _mod_spec(layer, 1, 0, tm), _mod_spec(layer, 1, 1, tm),
            _row_spec(layer * N_SUB + 1),
            full(win), full(qn), full(kvn), full(wq), full(wk), full(wv),
            tab_spec, tab_spec,
        ],
        out_specs=[
            pl.BlockSpec((tm, wide), lambda t: (t, 0)),
            pl.BlockSpec((tm, wide), lambda t: (t, 0)),
            pl.BlockSpec((tm, MLA_HEADS * V_HEAD), lambda t: (t, 0)),
        ],
        out_shape=[
            jax.ShapeDtypeStruct((TOKENS, wide), BF16),
            jax.ShapeDtypeStruct((TOKENS, wide), BF16),
            jax.ShapeDtypeStruct((TOKENS, MLA_HEADS * V_HEAD), BF16),
        ],
        compiler_params=_params("arbitrary"),
        name="mla_proj",
    )(x, mod, mod, pre, win, qn, kvn, wq, wk, wv, *tables)


def _mla_attn_kernel(q_ref, k_ref, v_ref, *refs, n_cast):
    cast_in, (o_ref, *cast_out), (vt_ref, ot_ref) = (
        refs[:n_cast], refs[n_cast:2 * n_cast + 1], refs[2 * n_cast + 1:])
    tq = TQ_MLA
    key = lax.broadcasted_iota(jnp.int32, (tq, tq), 0)
    query = lax.broadcasted_iota(jnp.int32, (tq, tq), 1)
    causal = key <= query
    vt_ref[:2 * V_HEAD, :] = v_ref[...].astype(F32).T.astype(BF16)
    vt_ref[2 * V_HEAD:, :] = jnp.ones((MLA_SUM_ROWS, SEQ), BF16)

    def logits(hh, qi, c):
        hs = slice(hh * HEAD_SLOT, (hh + 1) * HEAD_SLOT)
        s = _dot_nt(k_ref[c * tq:(c + 1) * tq, hs], q_ref[qi * tq:(qi + 1) * tq, hs])
        return jnp.where(causal, s, MASK_VALUE) if c == qi else s

    chunks = [(hh, qi, c) for hh in range(2) for qi in range(SEQ // tq) for c in range(qi + 1)]
    queued = [logits(*ch) for ch in chunks[:MLA_LOOKAHEAD]]
    for i, (hh, qi, c) in enumerate(chunks):
        if i + MLA_LOOKAHEAD < len(chunks):
            queued.append(logits(*chunks[i + MLA_LOOKAHEAD]))
        s = queued.pop(0)
        m_chunk = jnp.max(s, axis=0, keepdims=True)
        if c == 0:
            m = m_chunk
            acc = _dot(vt_ref[:, 0:tq], jnp.exp2(s - m).astype(BF16))
        else:
            m_new = jnp.maximum(m, m_chunk)
            acc = acc * jnp.exp2(m - m_new) + _dot(vt_ref[:, c * tq:(c + 1) * tq],
                                                   jnp.exp2(s - m_new).astype(BF16))
            m = m_new
        if c == qi:
            rows = slice(hh * V_HEAD, (hh + 1) * V_HEAD)
            ot_ref[rows, qi * tq:(qi + 1) * tq] = acc[rows, :] * (1.0 / acc[2 * V_HEAD:2 * V_HEAD + 1, :])
    o_ref[...] = ot_ref[...].T.astype(BF16)
    for src_ref, dst_ref in zip(cast_in, cast_out):
        dst_ref[...] = src_ref[...].astype(BF16)


def _mla_attn(q, k, v, casts):
    pairs = MLA_HEADS // 2
    cast_args, cast_in_specs, cast_out_specs, cast_shapes = [], [], [], []
    for array, first_set, block_rows in casts:
        sets, rows, cols = array.shape
        first_block, n_blocks = first_set * rows // block_rows, (sets - first_set) * rows // block_rows
        assert first_set * rows % block_rows == 0 and rows % block_rows == 0 and n_blocks <= BATCH * pairs

        def block(b, p, first=0, n_blocks=n_blocks):
            return first + jnp.minimum(b * pairs + p, n_blocks - 1), 0
        cast_args.append(array.reshape(sets * rows, cols))
        cast_in_specs.append(pl.BlockSpec((block_rows, cols), functools.partial(block, first=first_block)))
        cast_out_specs.append(pl.BlockSpec((block_rows, cols), block))
        cast_shapes.append(jax.ShapeDtypeStruct(((sets - first_set) * rows, cols), BF16))
    att, *copies = pl.pallas_call(
        functools.partial(_mla_attn_kernel, n_cast=len(casts)),
        grid=(BATCH, pairs),
        in_specs=[
            pl.BlockSpec((SEQ, 2 * HEAD_SLOT), lambda b, p: (b, p)),
            pl.BlockSpec((SEQ, 2 * HEAD_SLOT), lambda b, p: (b, p)),
            pl.BlockSpec((SEQ, 2 * V_HEAD), lambda b, p: (b, p)),
            *cast_in_specs,
        ],
        out_specs=[pl.BlockSpec((SEQ, 2 * V_HEAD), lambda b, p: (b, p)), *cast_out_specs],
        out_shape=[jax.ShapeDtypeStruct((TOKENS, MLA_HEADS * V_HEAD), BF16), *cast_shapes],
        scratch_shapes=[pltpu.VMEM((2 * V_HEAD + MLA_SUM_ROWS, SEQ), BF16),
                        pltpu.VMEM((2 * V_HEAD, SEQ), F32)],
        compiler_params=_params("arbitrary", "arbitrary"),
        name="mla_attn",
    )(q, k, v, *cast_args)
    return att, [copy.reshape(-1, *array.shape[1:]) for copy, (array, _, _) in zip(copies, casts)]


def _t5_bucket(dist):
    max_exact = N_BUCKETS // 2
    d = np.maximum(dist, 1).astype(np.float32)
    scale = np.float32(math.log(MAX_DISTANCE / max_exact))
    large = max_exact + (np.log(d / np.float32(max_exact)) / scale
                         * np.float32(N_BUCKETS - max_exact)).astype(np.int32)
    large = np.minimum(large, N_BUCKETS - 1)
    return np.where(dist < max_exact, dist, large)


def _bias_selectors():
    j = np.arange(2 * DIL_BLOCK)
    rel = DIL_BLOCK - j
    sel = []
    for window, dilation in DIL_GROUPS:
        assert window // dilation == DIL_BLOCK
        bucket = _t5_bucket(np.maximum(rel, 0) * dilation)
        onehot = (bucket[None, :] == np.arange(N_BUCKETS)[:, None]) & (rel >= 0)[None, :]
        sel.append(onehot.astype(np.float32))
    return jnp.asarray(np.stack(sel))


def _bias_kernel(rbt_ref, sel_ref, o_ref):
    rbt = rbt_ref[...]
    sel = sel_ref[...]
    row0 = jnp.zeros((DIL_HEADS, 2 * DIL_BLOCK), F32)
    for b in range(N_BUCKETS):
        row0 = row0 + rbt[:, b:b + 1] * sel[b:b + 1, :]
    future = jnp.sum(sel, axis=0, keepdims=True) < 0.5
    row0 = jnp.where(future, MASK_VALUE, row0 * LOG2_E)
    for h in range(DIL_HEADS):
        t = jnp.broadcast_to(row0[h:h + 1, :], (DIL_BLOCK, 2 * DIL_BLOCK))
        o_ref[h] = pltpu.roll(t, 0, 1, stride=1, stride_axis=0)


def _bias_tables(rel_bias):
    rbt = rel_bias.T.reshape(N_GROUPS, DIL_HEADS, N_BUCKETS)
    return pl.pallas_call(
        _bias_kernel,
        grid=(N_GROUPS,),
        in_specs=[
            pl.BlockSpec((None, DIL_HEADS, N_BUCKETS), lambda g: (g, 0, 0)),
            pl.BlockSpec((None, N_BUCKETS, 2 * DIL_BLOCK), lambda g: (g, 0, 0)),
        ],
        out_specs=pl.BlockSpec((DIL_HEADS, DIL_BLOCK, 2 * DIL_BLOCK), lambda g: (g, 0, 0)),
        out_shape=jax.ShapeDtypeStruct((N_GROUPS * DIL_HEADS, DIL_BLOCK, 2 * DIL_BLOCK), F32),
        compiler_params=_params("arbitrary"),
        name="dil_bias",
    )(rbt, _bias_selectors())


def _dil_proj_kernel(x_ref, shift_ref, scale_ref, pre_ref, w_ref, o_ref, *scratch, dilation):
    tm = x_ref.shape[0]
    per = PROJ_SUB // dilation
    slabs = D_MODEL // LANES
    hbs = []
    for i, r0 in enumerate(range(0, tm, PROJ_SUB)):
        hn = _rms(x_ref[r0:r0 + PROJ_SUB, :], pre_ref[...]) * (1.0 + scale_ref[...]) + shift_ref[...]
        if dilation == 1:
            hbs.append(hn.astype(BF16))
            continue
        slab_ref, split_ref = scratch
        for s in range(slabs):
            slab_ref[i, s] = hn[:, s * LANES:(s + 1) * LANES]
        if dilation <= DEINTERLEAVE_STRIDE:
            pick = lambda s, r: slab_ref[i, s, pl.ds(r, per, stride=dilation), :]
        else:
            quarter = PROJ_SUB // DEINTERLEAVE_STRIDE
            outer = dilation // DEINTERLEAVE_STRIDE
            for s in range(slabs):
                for a in range(DEINTERLEAVE_STRIDE):
                    split_ref[i, s, a * quarter:(a + 1) * quarter, :] = (
                        slab_ref[i, s, pl.ds(a, quarter, stride=DEINTERLEAVE_STRIDE), :])
            pick = lambda s, r: split_ref[
                i, s, pl.ds((r % DEINTERLEAVE_STRIDE) * quarter + r // DEINTERLEAVE_STRIDE, per, stride=outer), :]
        hbs.append(jnp.concatenate(
            [jnp.concatenate([pick(s, r) for s in range(slabs)], axis=1).astype(BF16)
             for r in range(dilation)], axis=0))
    for i, hb in enumerate(hbs):
        for j in range(3):
            cols = slice(j * DIL_WIDTH, (j + 1) * DIL_WIDTH)
            y = _dot(hb, w_ref[:, cols])
            if j == 0:
                y = y * (DIL_HEAD_DIM ** -0.5 * LOG2_E)
            y = y.astype(BF16)
            for r in range(dilation):
                o_ref[r, i * per:(i + 1) * per, cols] = y[r * per:(r + 1) * per, :]


def _dil_proj(x, mod, pre, w, layer, group):
    _, dilation = DIL_GROUPS[group]
    tm = TM_DIL_PROJ
    tiles_per_batch = SEQ // tm
    sub = tm // dilation
    scratch = [] if dilation == 1 else [
        pltpu.VMEM((tm // PROJ_SUB, D_MODEL // LANES, PROJ_SUB, LANES), F32)] * 2
    return pl.pallas_call(
        functools.partial(_dil_proj_kernel, dilation=dilation),
        grid=(TOKENS // tm,),
        in_specs=[
            pl.BlockSpec((tm, D_MODEL), lambda t: (t, 0)),
            _mod_spec(layer, 1, 0, tm), _mod_spec(layer, 1, 1, tm),
            _row_spec(layer * N_SUB + 1),
            pl.BlockSpec((D_MODEL, 3 * DIL_WIDTH), lambda t: (0, group), pipeline_mode=pl.Buffered(1)),
        ],
        out_specs=pl.BlockSpec((None, dilation, sub, 3 * DIL_WIDTH),
                               lambda t: (t // tiles_per_batch, 0, t % tiles_per_batch, 0)),
        out_shape=jax.ShapeDtypeStruct((BATCH, dilation, SEQ // dilation, 3 * DIL_WIDTH), BF16),
        scratch_shapes=scratch,
        compiler_params=_params("arbitrary"),
        name=f"dil_proj_g{group}",
    )(x, mod, mod, pre, w)


DIL_HEADS_PER_STEP = 8
DIL_PAIRS_PER_STEP = DIL_HEADS_PER_STEP // 2
DIL_UNITS_PER_TRIP = 8


STAT_MAX_SHIFT = 32


def _stat_lane(head):
    return head if head % 2 else DIL_HEAD_DIM + head


def _dil_attn_kernel(q_ref, k_ref, v_ref, bias_ref, o_ref, stat_ref, s_ref, acc_ref, *, dilation, n_blocks):
    blk = DIL_BLOCK
    lane = lax.broadcasted_iota(jnp.int32, (blk, LANES), 1)
    first_head = lane < DIL_HEAD_DIM
    head0 = pl.program_id(1) * DIL_HEADS_PER_STEP

    def where_unit(i, first):
        if first:
            return i, 0
        later = n_blocks - 1
        if dilation == 1:
            return 0, i + 1
        i = jnp.asarray(i, jnp.int32)
        return lax.div(i, jnp.int32(later)), lax.rem(i, jnp.int32(later)) + 1

    def window(ref, r, n, first, cs):
        if first:
            return ref[r, 0:blk, cs]
        return ref[r, pl.ds(pl.multiple_of(n * blk - blk, blk), 2 * blk), cs]

    def logits(i, slot, first):
        r, n = where_unit(i, first)
        kw = blk if first else 2 * blk
        q_rows = slice(0, blk) if first else pl.ds(pl.multiple_of(n * blk, blk), blk)
        for hp in range(DIL_PAIRS_PER_STEP):
            cs = slice(hp * LANES, (hp + 1) * LANES)
            q2 = q_ref[r, q_rows, cs]
            k2 = window(k_ref, r, n, first, cs)
            for hh in range(2):
                h = 2 * hp + hh
                qh = jnp.where(first_head if hh == 0 else jnp.logical_not(first_head), q2, 0)
                bias = bias_ref[h, :, blk:] if first else bias_ref[h]
                s_ref[slot, h, :, 0:kw] = _dot_nt(qh, k2) + bias

    def attend(i, slot, first):
        r, n = where_unit(i, first)
        kw = blk if first else 2 * blk
        t0 = n * (blk * dilation) + r
        rows = pl.ds(t0, blk) if dilation == 1 else pl.ds(t0, blk, stride=dilation)
        stat_tile = jnp.ones((blk, LANES), F32)
        for hp in range(DIL_PAIRS_PER_STEP):
            cs = slice(hp * LANES, (hp + 1) * LANES)
            v2 = window(v_ref, r, n, first, cs)
            accs = []
            for hh in range(2):
                s = s_ref[slot, 2 * hp + hh, :, 0:kw]
                m = jnp.max(s, axis=-1, keepdims=True)
                key_lane = lax.broadcasted_iota(jnp.int32, (kw, LANES), 1)
                own = key_lane < DIL_HEAD_DIM if hh == 0 else key_lane >= DIL_HEAD_DIM
                acc = _dot(jnp.exp2(s - m).astype(BF16), jnp.where(own, v2, 1))
                sum_lane = head0 + _stat_lane(2 * hp + hh)
                stat_tile = jnp.where(lane == sum_lane, acc, stat_tile)
                stat_tile = jnp.where(lane == sum_lane + STAT_MAX_SHIFT, m, stat_tile)
                accs.append(acc)
            acc_ref[hp, rows, :] = jnp.where(first_head, accs[0], accs[1])
        stat_ref[rows, :] = stat_tile

    def run(count, first):
        trips, rest = divmod(count, DIL_UNITS_PER_TRIP)

        def units(base, n, more):
            for u in range(n):
                if u + 1 < n or more:
                    logits(base + u + 1, (u + 1) % 2, first)
                attend(base + u, u % 2, first)

        logits(0, 0, first)
        if trips:
            def body(j, carry):
                units(j * DIL_UNITS_PER_TRIP, DIL_UNITS_PER_TRIP, True)
                return carry
            lax.fori_loop(0, trips - (0 if rest else 1), body, 0)
            if not rest:
                units((trips - 1) * DIL_UNITS_PER_TRIP, DIL_UNITS_PER_TRIP, False)
        if rest:
            units(trips * DIL_UNITS_PER_TRIP, rest, False)

    run(dilation, True)
    if n_blocks > 1:
        run(dilation * (n_blocks - 1), False)
    for hp in range(DIL_PAIRS_PER_STEP):
        o_ref[:, hp * LANES:(hp + 1) * LANES] = acc_ref[hp].astype(BF16)


def _dil_attn(proj, bias, group):
    _, dilation = DIL_GROUPS[group]
    length = SEQ // dilation
    halves = DIL_HEADS // DIL_HEADS_PER_STEP
    width = DIL_HEADS_PER_STEP * DIL_HEAD_DIM
    qkv_spec = lambda kind: pl.BlockSpec(
        (None, dilation, length, width), lambda b, u: (b, 0, 0, kind * halves + u))
    return pl.pallas_call(
        functools.partial(_dil_attn_kernel, dilation=dilation, n_blocks=length // DIL_BLOCK),
        grid=(BATCH, halves),
        in_specs=[
            qkv_spec(0), qkv_spec(1), qkv_spec(2),
            pl.BlockSpec((DIL_HEADS_PER_STEP, DIL_BLOCK, 2 * DIL_BLOCK),
                         lambda b, u: (group * halves + u, 0, 0)),
        ],
        out_specs=[
            pl.BlockSpec((None, SEQ, width), lambda b, u: (b, 0, u)),
            pl.BlockSpec((None, SEQ, LANES), lambda b, u: (b, 0, u)),
        ],
        out_shape=[
            jax.ShapeDtypeStruct((BATCH, SEQ, DIL_WIDTH), BF16),
            jax.ShapeDtypeStruct((BATCH, SEQ, halves * LANES), F32),
        ],
        scratch_shapes=[pltpu.VMEM((2, DIL_HEADS_PER_STEP, DIL_BLOCK, 2 * DIL_BLOCK), F32),
                        pltpu.VMEM((DIL_PAIRS_PER_STEP, SEQ, LANES), F32)],
        compiler_params=_params("arbitrary", "arbitrary"),
        name=f"dil_attn_g{group}",
    )(proj, proj, proj, bias)


def _dil_out_kernel(x_ref, o0_ref, o1_ref, o2_ref, s0_ref, s1_ref, s2_ref, e_ref, wo_ref,
                    gate_ref, post_ref, out_ref):
    lane = lax.broadcasted_iota(jnp.int32, (1, LANES), 1)
    in_first_block = functools.reduce(jnp.logical_or, [
        jnp.logical_or(lane == _stat_lane(h), lane == _stat_lane(h) + STAT_MAX_SHIFT)
        for h in range(DIL_HEADS_PER_STEP)])
    e = e_ref[...]
    for r0 in range(0, x_ref.shape[0], OUT_SUB):
        rows = slice(r0, r0 + OUT_SUB)
        sums = [jnp.where(in_first_block, r[rows, :LANES], r[rows, LANES:]) for r in (s0_ref, s1_ref, s2_ref)]
        maxes = [pltpu.roll(s, LANES - STAT_MAX_SHIFT, 1) for s in sums]
        m = jnp.maximum(jnp.maximum(maxes[0], maxes[1]), maxes[2])
        es = [jnp.exp2(mg - m) for mg in maxes]
        inv = 1.0 / (sums[0] * es[0] + sums[1] * es[1] + sums[2] * es[2])
        mix = None
        for eg, o_ref in zip(es, (o0_ref, o1_ref, o2_ref)):
            alpha = eg * inv
            hi = alpha.astype(BF16)
            lo = (alpha - hi.astype(F32)).astype(BF16)
            term = _dot(jnp.concatenate([hi, lo], axis=1), e) * o_ref[rows, :].astype(F32)
            mix = term if mix is None else mix + term
        y = _dot(mix.astype(BF16), wo_ref[...])
        out_ref[rows, :] = x_ref[rows, :] + gate_ref[...] * _rms(y, post_ref[...])


def _dil_out(x, outs, stats, mod, post, w_o, layer):
    tm = TM_OUT
    tiles_per_batch = SEQ // tm
    halves = DIL_HEADS // DIL_HEADS_PER_STEP
    stat_lane_of_col = jnp.array([_stat_lane(c // DIL_HEAD_DIM) for c in range(DIL_WIDTH)], jnp.int32)
    lane_of_row = jnp.arange(2 * LANES) % LANES
    expand = (lane_of_row[:, None] == stat_lane_of_col[None, :]).astype(BF16)
    batch_rows = lambda width: pl.BlockSpec(
        (None, tm, width), lambda t: (t // tiles_per_batch, t % tiles_per_batch, 0))
    tok = pl.BlockSpec((tm, D_MODEL), lambda t: (t, 0))
    return pl.pallas_call(
        _dil_out_kernel,
        grid=(TOKENS // tm,),
        in_specs=[
            tok, *([batch_rows(DIL_WIDTH)] * 3), *([batch_rows(halves * LANES)] * 3),
            pl.BlockSpec(expand.shape, lambda t: (0, 0)), pl.BlockSpec(w_o.shape, lambda t: (0, 0)),
            _mod_spec(layer, 1, 2, tm), _row_spec(layer * N_SUB + 1),
        ],
        out_specs=tok,
        out_shape=jax.ShapeDtypeStruct((TOKENS, D_MODEL), F32),
        compiler_params=_params("arbitrary"),
        name="dil_out",
    )(x, *outs, *stats, expand, w_o, mod, post)


def kernel(x, c, norm_pre, norm_post, w_mod, b_mod, ffn_w_gate, ffn_w_up, ffn_w_down,
           mla_w_in, mla_q_norm, mla_w_q_up, mla_kv_norm, mla_w_kv_up, mla_w_o,
           dil_w_in, dil_w_o, rel_bias):
    assert x.shape == (BATCH, SEQ, D_MODEL) and x.dtype == F32
    assert DEPTH == 2
    h = x.reshape(TOKENS, D_MODEL)
    mod = _modulation(c, w_mod, b_mod)
    pre = norm_pre.reshape(DEPTH * N_SUB, 1, D_MODEL)
    post = norm_post.reshape(DEPTH * N_SUB, 1, D_MODEL)
    tables = _rope_tables()
    bias = _bias_tables(rel_bias)
    ffn_f32 = [w.reshape(DEPTH * 2, *w.shape[2:]) for w in (ffn_w_gate, ffn_w_up, ffn_w_down)]

    h = _ffn(h, mod, pre, post, (*[w[:1].astype(BF16) for w in ffn_f32], 0), 0, 0)
    win, wq, wk, wv = _mla_weights(mla_w_in[0], mla_w_q_up[0], mla_w_kv_up[0])
    q, k, v = _mla_proj(h, mod, pre, win, mla_q_norm[0][None, :], mla_kv_norm[0][None, :],
                        wq, wk, wv, tables, 0)
    casts = [(ffn_f32[0], 1, 64), (ffn_f32[1], 1, 64), (ffn_f32[2], 1, 256), (dil_w_in, 0, 16)]
    att, (wg, wu, wd, dil_in) = _mla_attn(q, k, v, casts)
    h = _ffn(h, mod, pre, post, (wg, wu, wd, 0), 0, 1, mla=(att, mla_w_o[0].astype(BF16)))

    h = _ffn(h, mod, pre, post, (wg, wu, wd, 1), 1, 0)
    outs, stats = zip(*[_dil_attn(_dil_proj(h, mod, pre, dil_in[0], 1, g), bias, g) for g in range(N_GROUPS)])
    h = _dil_out(h, outs, stats, mod, post, dil_w_o[0].astype(BF16), 1)
    h = _ffn(h, mod, pre, post, (wg, wu, wd, 2), 1, 1)
    return h.reshape(BATCH, SEQ, D_MODEL)
```
